```python
import math
import jax
import jax.numpy as jnp
from jax import lax
import numpy as np

D_MODEL = 2048
BATCH = 16
SEQ = 256
DEPTH = 2
DEC_BATCH = 8
DEC_SEQ = 2048
PAST_LEN = 256

GRID_W = 64
CONV_W = 3
N_DIR = 2
MIX_WIDTH = D_MODEL
GROUP_WIDTH = MIX_WIDTH // 4
HY_CH = GROUP_WIDTH
HY_ORDER = 2
HY_BANDS = 8
HY_EMB = 1 + 2 * HY_BANDS
HY_FFN = 64
HY_SINS = 2
HY_DECAY_TARGET = 1e-2
HY_FAST_PCT = 0.3
HY_SLOW_PCT = 1.5
SSD_WIDTH = GROUP_WIDTH
SSD_HEAD_DIM = 64
SSD_HEADS = SSD_WIDTH // SSD_HEAD_DIM
SSD_GROUPS = 2
SSD_HPG = SSD_HEADS // SSD_GROUPS
SSD_STATE = 128
SSD_CHUNK = 128
GDN_WIDTH = GROUP_WIDTH
GDN_HEAD_DIM = 128
GDN_HEADS = GDN_WIDTH // GDN_HEAD_DIM
GDN_CHUNK = 64
RET_WIDTH = GROUP_WIDTH
RET_HEAD_DIM = 128
RET_HEADS = RET_WIDTH // RET_HEAD_DIM
RET_CHUNK = 128
ROPE_BASE = 10000.0
D_FF = 5632
CONV_SIZES = (HY_CH, HY_CH, HY_CH, SSD_WIDTH, SSD_GROUPS * SSD_STATE, SSD_GROUPS * SSD_STATE, GDN_WIDTH, GDN_WIDTH, GDN_WIDTH)
REST_SIZES = (SSD_WIDTH, N_DIR * SSD_HEADS, GDN_WIDTH, N_DIR * GDN_HEADS, N_DIR * GDN_HEADS, RET_WIDTH, RET_WIDTH, RET_WIDTH, RET_WIDTH)
N_CONV = sum(CONV_SIZES)
N_IN = N_CONV + sum(REST_SIZES)
DN_ALPHA = (2 * DEPTH) ** 0.25
DN_BETA = (8 * DEPTH) ** -0.25
LN_EPS = 1e-5
RMS_EPS = 1e-6

kernel_name = 'hybrid_diffusion_hyena_ssd_gdn_retention_step'


def norm_plain(x):
    mu = jnp.mean(x, -1, keepdims=True)
    var = jnp.mean(jnp.square(x - mu), -1, keepdims=True)
    return (x - mu) * lax.rsqrt(var + LN_EPS)


def layer_norm(x, g, b):
    return norm_plain(x.astype(jnp.float32)).astype(x.dtype) * g + b


def rms_norm(x, g):
    xf = x.astype(jnp.float32)
    return (xf * lax.rsqrt(jnp.mean(xf * xf, -1, keepdims=True) + RMS_EPS)).astype(x.dtype) * g


def l2norm(x):
    return x * lax.rsqrt(jnp.sum(x * x, -1, keepdims=True) + 1e-6)


def split_sizes(x, sizes):
    return jnp.split(x, np.cumsum(sizes)[:-1].tolist(), axis=-1)


def dwconv(x, w, b):
    y = lax.conv_general_dilated(x, w[:, None, :].astype(x.dtype), window_strides=(1,),
                                 padding=((CONV_W // 2, CONV_W // 2),),
                                 dimension_numbers=('NWC', 'WIO', 'NWC'),
                                 feature_group_count=x.shape[-1])
    return y + b.astype(x.dtype)


def run_direction(fn, d, seqs, *args):
    if d == 1:
        seqs = tuple(jnp.flip(t, 1) for t in seqs)
    y, s = fn(*seqs, *args)
    if d == 1:
        y = jnp.flip(y, 1)
    return y, s


def to_chunks(t, q):
    b, L, h = t.shape[:3]
    return jnp.moveaxis(t.reshape(b, L // q, q, h, *t.shape[3:]), 3, 2)


def from_chunks(o):
    b, nc, h, q, v = o.shape
    return jnp.moveaxis(o, 2, 3).reshape(b, nc * q, h, v)


def chunk_state_scan(s0, q_dec, k_dec, val, attn, tot, w=None):
    arrs = (q_dec, k_dec, val, attn, tot) if w is None else (q_dec, k_dec, val, attn, tot, w)
    xs = tuple(jnp.moveaxis(t, 1, 0) for t in arrs)

    def step(s, xc):
        qd, kd, vv, at, tt = xc[:5]
        if w is not None:
            vv = vv - jnp.einsum('bhqk,bhkv->bhqv', xc[5], s)
        o = jnp.einsum('bhqk,bhkv->bhqv', qd, s) + jnp.einsum('bhij,bhjv->bhiv', at, vv)
        s = s * tt[..., None, None] + jnp.einsum('bhqk,bhqv->bhkv', kd, vv)
        return s, o

    s_fin, o = lax.scan(step, s0, xs)
    return jnp.moveaxis(o, 0, 1), s_fin


def hyena_order2(v, x1, x2, w1, b1, w2, b2, w3, freq, bias):
    b, L, C = v.shape
    f32 = jnp.float32
    pos = jnp.arange(L, dtype=f32)
    t = pos / (L - 1)
    bands = jnp.linspace(1e-4, HY_BANDS - 1, HY_BANDS, dtype=f32)
    ang = (2.0 * math.pi / L) * pos[:, None] * bands
    feats = jnp.concatenate([t[:, None], jnp.cos(ang), -jnp.sin(ang)], -1)
    h = jnp.sin(freq[0].astype(f32) * (feats @ w1.astype(f32) + b1.astype(f32)))
    h = jnp.sin(freq[1].astype(f32) * (h @ w2.astype(f32) + b2.astype(f32)))
    filt = (h @ w3.astype(f32)).reshape(L, HY_ORDER, N_DIR, C)
    deltas = jnp.abs(jnp.linspace(math.log(HY_DECAY_TARGET) / HY_SLOW_PCT,
                                  math.log(HY_DECAY_TARGET) / HY_FAST_PCT, C, dtype=f32))
    filt = filt * jnp.exp(-t[:, None] * deltas)[:, None, None, :]
    kern = jnp.concatenate([filt[:, :, 0], jnp.zeros((1, HY_ORDER, C), f32), filt[:0:-1, :, 1]], 0)
    kern = kern / jnp.sum(jnp.abs(kern), 0, keepdims=True)
    kf = jnp.fft.rfft(kern, axis=0)

    def long_conv(z, o):
        zf = jnp.fft.rfft(z.astype(f32), n=2 * L, axis=1)
        y = jnp.fft.irfft(zf * kf[:, o], n=2 * L, axis=1)[:, :L]
        return (y + z.astype(f32) * bias[o].astype(f32)).astype(v.dtype)

    z = x1 * long_conv(v, 0)
    return x2 * long_conv(z, 1)


def ssd_chunked(x, dt, a, bm, cm, h0):
    b, L, G, E, P = x.shape
    N = bm.shape[-1]
    Q = SSD_CHUNK
    nc = L // Q
    xs = (x * dt[..., None]).reshape(b, nc, Q, G, E, P)
    bc = bm.reshape(b, nc, Q, G, N)
    cc = cm.reshape(b, nc, Q, G, N)
    acs = jnp.cumsum(a.reshape(b, nc, Q, G, E), axis=2)
    causal = jnp.tril(jnp.ones((Q, Q), bool))[:, :, None, None]
    lmat = jnp.exp(jnp.where(causal, acs[:, :, :, None] - acs[:, :, None, :], -jnp.inf))
    cb = jnp.einsum('bclgn,bcsgn->bclsg', cc, bc)
    y_diag = jnp.einsum('bclsg,bclsge,bcsgep->bclgep', cb, lmat, xs)
    decay_end = jnp.exp(acs[:, :, -1:] - acs)
    states = jnp.einsum('bcsgn,bcsge,bcsgep->bcgepn', bc, decay_end, xs)
    tot = jnp.exp(acs[:, :, -1])

    def step(h, inp):
        st, tt = inp
        return h * tt[..., None, None] + st, h

    h_fin, h_start = lax.scan(step, h0, (jnp.moveaxis(states, 1, 0), jnp.moveaxis(tot, 1, 0)))
    h_start = jnp.moveaxis(h_start, 0, 1)
    y_off = jnp.einsum('bclgn,bcgepn,bclge->bclgep', cc, h_start, jnp.exp(acs))
    return (y_diag + y_off).reshape(b, L, G, E, P), h_fin


def gdn_chunked(q, k, v, g, beta, s0):
    Q = GDN_CHUNK
    q, k, v, g, beta = (to_chunks(t, Q) for t in (q, k, v, g, beta))
    gc = jnp.cumsum(g, -1)
    idx = jnp.arange(Q)
    causal = idx[:, None] >= idx[None, :]
    strict = idx[:, None] > idx[None, :]
    dmask = jnp.exp(jnp.where(causal, gc[..., :, None] - gc[..., None, :], -jnp.inf))
    a_strict = jnp.where(strict, beta[..., :, None] * jnp.einsum('bchik,bchjk->bchij', k, k) * dmask, 0.0)
    m = a_strict + jnp.eye(Q, dtype=a_strict.dtype)
    rhs = jnp.concatenate([v * beta[..., None], k * (beta * jnp.exp(gc))[..., None]], -1)
    sol = lax.linalg.triangular_solve(m, rhs, left_side=True, lower=True, unit_diagonal=True)
    nv = v.shape[-1]
    u, w = sol[..., :nv], sol[..., nv:]
    attn = jnp.einsum('bchik,bchjk->bchij', q, k) * dmask
    q_dec = q * jnp.exp(gc)[..., None]
    k_dec = k * jnp.exp(gc[..., -1:] - gc)[..., None]
    o, s_fin = chunk_state_scan(s0, q_dec, k_dec, u, attn, jnp.exp(gc[..., -1]), w)
    return from_chunks(o), s_fin


def retention_chunked(q, k, v, log_gamma, s0):
    Q = RET_CHUNK
    b = q.shape[0]
    q, k, v = (to_chunks(t, Q) for t in (q, k, v))
    nc = q.shape[1]
    pos = jnp.arange(Q, dtype=jnp.float32)
    rel = pos[:, None] - pos[None, :]
    dmask = jnp.exp(jnp.where(rel >= 0, rel * log_gamma[:, None, None], -jnp.inf))
    attn = jnp.einsum('bchik,bchjk->bchij', q, k) * dmask
    q_dec = q * jnp.exp((pos + 1.0) * log_gamma[:, None])[..., None]
    k_dec = k * jnp.exp((Q - 1.0 - pos) * log_gamma[:, None])[..., None]
    tot = jnp.broadcast_to(jnp.exp(Q * log_gamma), (b, nc) + log_gamma.shape)
    o, s_fin = chunk_state_scan(s0, q_dec, k_dec, v, attn, tot)
    return from_chunks(o), s_fin


def grid_rope(L):
    rows = L // GRID_W
    r = jnp.repeat(jnp.arange(rows), GRID_W).astype(jnp.float32)
    col = jnp.tile(jnp.arange(GRID_W), rows).astype(jnp.float32)
    nf = RET_HEAD_DIM // 4
    inv = jnp.power(ROPE_BASE, -jnp.arange(nf, dtype=jnp.float32) / nf)
    ang = jnp.concatenate([r[:, None] * inv, col[:, None] * inv], -1)
    return jnp.cos(ang), jnp.sin(ang)


def apply_rope(x, cos, sin):
    half = x.shape[-1] // 2
    x1, x2 = x[..., :half], x[..., half:]
    c, s = cos[None, :, None], sin[None, :, None]
    return jnp.concatenate([x1 * c - x2 * s, x1 * s + x2 * c], -1)


def token_mixing(u, p, l, s_ssd, s_gdn, s_ret, rope):
    b, L, _ = u.shape
    f32 = jnp.float32
    proj = u @ p['w_in'][l]
    conv_out = dwconv(proj[..., :N_CONV], p['conv_w'][l], p['conv_b'][l])
    hy_v, hy_x1, hy_x2, s_x, s_b, s_c, g_q, g_k, g_v = split_sizes(conv_out, CONV_SIZES)
    s_z, s_dt, g_z, g_beta, g_a, r_q, r_k, r_v, r_g = split_sizes(proj[..., N_CONV:], REST_SIZES)

    y_hy = hyena_order2(hy_v, hy_x1, hy_x2, p['hy_w1'][l], p['hy_b1'][l], p['hy_w2'][l], p['hy_b2'][l],
                        p['hy_w3'][l], p['hy_freq'][l], p['hy_bias'][l])

    xh = jax.nn.silu(s_x.astype(f32)).reshape(b, L, SSD_GROUPS, SSD_HPG, SSD_HEAD_DIM)
    bm = jax.nn.silu(s_b.astype(f32)).reshape(b, L, SSD_GROUPS, SSD_STATE)
    cm = jax.nn.silu(s_c.astype(f32)).reshape(b, L, SSD_GROUPS, SSD_STATE)
    dt = jax.nn.softplus(s_dt.astype(f32).reshape(b, L, N_DIR, SSD_HEADS) + p['ssd_dt_bias'][l].astype(f32))
    a_rate = -jnp.exp(p['ssd_A_log'][l].astype(f32))
    d_skip = p['ssd_D'][l].astype(f32)
    ys, ssd_new = [], []
    for d in range(N_DIR):
        dt_d = dt[:, :, d].reshape(b, L, SSD_GROUPS, SSD_HPG)
        h0 = s_ssd[:, d].astype(f32).reshape(b, SSD_GROUPS, SSD_HPG, SSD_HEAD_DIM, SSD_STATE)
        y, hf = run_direction(ssd_chunked, d, (xh, dt_d, dt_d * a_rate[d].reshape(SSD_GROUPS, SSD_HPG), bm, cm), h0)
        ys.append(y + d_skip[d].reshape(SSD_GROUPS, SSD_HPG, 1) * xh)
        ssd_new.append(hf.reshape(b, SSD_HEADS, SSD_HEAD_DIM, SSD_STATE))
    y_ssd = (ys[0] + ys[1]).reshape(b, L, SSD_WIDTH).astype(u.dtype)
    y_ssd = rms_norm(y_ssd * jax.nn.silu(s_z), p['ssd_norm_w'][l])

    q = l2norm(jax.nn.silu(g_q.astype(f32)).reshape(b, L, GDN_HEADS, GDN_HEAD_DIM)) * GDN_HEAD_DIM ** -0.5
    k = l2norm(jax.nn.silu(g_k.astype(f32)).reshape(b, L, GDN_HEADS, GDN_HEAD_DIM))
    v = jax.nn.silu(g_v.astype(f32)).reshape(b, L, GDN_HEADS, GDN_HEAD_DIM)
    beta = jax.nn.sigmoid(g_beta.astype(f32).reshape(b, L, N_DIR, GDN_HEADS))
    g_log = -jnp.exp(p['gdn_A_log'][l].astype(f32)) * jax.nn.softplus(
        g_a.astype(f32).reshape(b, L, N_DIR, GDN_HEADS) + p['gdn_dt_bias'][l].astype(f32))
    outs, gdn_new = [], []
    for d in range(N_DIR):
        o, sf = run_direction(gdn_chunked, d, (q, k, v, g_log[:, :, d], beta[:, :, d]), s_gdn[:, d].astype(f32))
        outs.append(o)
        gdn_new.append(sf)
    o = (outs[0] + outs[1]).astype(u.dtype)
    y_gdn = (rms_norm(o, p['gdn_norm_w'][l]) *
             jax.nn.silu(g_z).reshape(b, L, GDN_HEADS, GDN_HEAD_DIM)).reshape(b, L, GDN_WIDTH)

    rq = r_q.astype(f32).reshape(b, L, RET_HEADS, RET_HEAD_DIM)
    rk = r_k.astype(f32).reshape(b, L, RET_HEADS, RET_HEAD_DIM) * RET_HEAD_DIM ** -0.5
    rv = r_v.astype(f32).reshape(b, L, RET_HEADS, RET_HEAD_DIM)
    if rope is not None:
        rq = apply_rope(rq, *rope)
        rk = apply_rope(rk, *rope)
    log_gamma = -jnp.exp(p['ret_decay'][l].astype(f32))
    outs, ret_new = [], []
    for d in range(N_DIR):
        o, sf = run_direction(retention_chunked, d, (rq, rk, rv), log_gamma[d], s_ret[:, d].astype(f32))
        outs.append(o)
        ret_new.append(sf)
    y_ret = norm_plain(outs[0] + outs[1]).astype(u.dtype).reshape(b, L, RET_WIDTH) * jax.nn.silu(r_g)

    out = jnp.concatenate([y_hy, y_ssd, y_gdn, y_ret], -1) @ p['w_out'][l]
    return out, (jnp.stack(ssd_new, 1), jnp.stack(gdn_new, 1), jnp.stack(ret_new, 1))


def conv_ffn(u, w_up, cw, cb, w_down):
    h = dwconv(u @ w_up, cw, cb)
    gate, val = jnp.split(h, 2, axis=-1)
    return (jax.nn.silu(gate) * val) @ w_down


def trunk_layer(x, cond, p, l, s_ssd, s_gdn, s_ret, rope):
    mod = jax.nn.silu(cond) @ p['w_mod'][l] + p['b_mod'][l]
    sh1, sc1, g1, sh2, sc2, g2 = jnp.split(mod[:, None, :], 6, axis=-1)
    y, states = token_mixing(x * (1.0 + sc1) + sh1, p, l, s_ssd, s_gdn, s_ret, rope)
    x = layer_norm(DN_ALPHA * x + g1 * y, p['ln1_g'][l], p['ln1_b'][l])
    f = conv_ffn(x * (1.0 + sc2) + sh2, p['w_up'][l], p['ffn_conv_w'][l], p['ffn_conv_b'][l], p['w_down'][l])
    x = layer_norm(DN_ALPHA * x + g2 * f, p['ln2_g'][l], p['ln2_b'][l])
    return x, states


def setup_inputs(seed: int = 0) -> dict:
    key = jax.random.key(seed)
    ks = iter(jax.random.split(key, 64))
    f32 = jnp.float32

    def nrm(shape, scale):
        return scale * jax.random.normal(next(ks), shape, f32)

    def unif(shape, lo, hi):
        return jax.random.uniform(next(ks), shape, f32, lo, hi)

    def dt_bias(shape):
        dt = jnp.exp(unif(shape, math.log(1e-3), math.log(1e-1)))
        return dt + jnp.log(-jnp.expm1(-dt))

    ret_base = jnp.asarray(np.log(-np.log(1.0 - 2.0 ** (-5.0 - np.arange(RET_HEADS)))), f32)
    return {
        'x_prompt': nrm((BATCH, SEQ, D_MODEL), 1.0),
        'x_sample': nrm((DEC_BATCH, DEC_SEQ, D_MODEL), 1.0),
        'state_ssd': nrm((DEC_BATCH, DEPTH, N_DIR, SSD_HEADS, SSD_HEAD_DIM, SSD_STATE), 0.1),
        'state_gdn': nrm((DEC_BATCH, DEPTH, N_DIR, GDN_HEADS, GDN_HEAD_DIM, GDN_HEAD_DIM), 0.3),
        'state_ret': nrm((DEC_BATCH, DEPTH, N_DIR, RET_HEADS, RET_HEAD_DIM, RET_HEAD_DIM), 1.0),
        'c': nrm((DEC_BATCH, D_MODEL), 1.0),
        'c_ctx': nrm((D_MODEL,), 1.0),
        'w_mod': nrm((DEPTH, D_MODEL, 6 * D_MODEL), 0.5 * D_MODEL ** -0.5),
        'b_mod': nrm((DEPTH, 6 * D_MODEL), 0.02),
        'w_in': nrm((DEPTH, D_MODEL, N_IN), D_MODEL ** -0.5),
        'conv_w': nrm((DEPTH, CONV_W, N_CONV), CONV_W ** -0.5),
        'conv_b': nrm((DEPTH, N_CONV), 0.02),
        'hy_w1': nrm((DEPTH, HY_EMB, HY_FFN), HY_EMB ** -0.5),
        'hy_b1': nrm((DEPTH, HY_FFN), 0.1),
        'hy_w2': nrm((DEPTH, HY_FFN, HY_FFN), HY_FFN ** -0.5),
        'hy_b2': nrm((DEPTH, HY_FFN), 0.1),
        'hy_w3': nrm((DEPTH, HY_FFN, HY_ORDER * N_DIR * HY_CH), HY_FFN ** -0.5),
        'hy_freq': 1.0 + nrm((DEPTH, HY_SINS, HY_FFN), 0.1),
        'hy_bias': nrm((DEPTH, HY_ORDER, HY_CH), 0.1),
        'ssd_A_log': jnp.log(unif((DEPTH, N_DIR, SSD_HEADS), 1.0, 16.0)),
        'ssd_dt_bias': dt_bias((DEPTH, N_DIR, SSD_HEADS)),
        'ssd_D': 1.0 + nrm((DEPTH, N_DIR, SSD_HEADS), 0.1),
        'ssd_norm_w': 1.0 + nrm((DEPTH, SSD_WIDTH), 0.02),
        'gdn_A_log': jnp.log(unif((DEPTH, N_DIR, GDN_HEADS), 1.0, 16.0)),
        'gdn_dt_bias': dt_bias((DEPTH, N_DIR, GDN_HEADS)),
        'gdn_norm_w': 1.0 + nrm((DEPTH, GDN_HEAD_DIM), 0.02),
        'ret_decay': ret_base + nrm((DEPTH, N_DIR, RET_HEADS), 0.05),
        'w_out': nrm((DEPTH, MIX_WIDTH, D_MODEL), DN_BETA * MIX_WIDTH ** -0.5),
        'ln1_g': 1.0 + nrm((DEPTH, D_MODEL), 0.02),
        'ln1_b': nrm((DEPTH, D_MODEL), 0.02),
        'w_up': nrm((DEPTH, D_MODEL, 2 * D_FF), D_MODEL ** -0.5),
        'ffn_conv_w': nrm((DEPTH, CONV_W, 2 * D_FF), CONV_W ** -0.5),
        'ffn_conv_b': nrm((DEPTH, 2 * D_FF), 0.02),
        'w_down': nrm((DEPTH, D_FF, D_MODEL), DN_BETA * D_FF ** -0.5),
        'ln2_g': 1.0 + nrm((DEPTH, D_MODEL), 0.02),
        'ln2_b': nrm((DEPTH, D_MODEL), 0.02),
    }


def reference(x_prompt, x_sample, state_ssd, state_gdn, state_ret, c, c_ctx, w_mod, b_mod, w_in, conv_w, conv_b,
              hy_w1, hy_b1, hy_w2, hy_b2, hy_w3, hy_freq, hy_bias, ssd_A_log, ssd_dt_bias, ssd_D, ssd_norm_w,
              gdn_A_log, gdn_dt_bias, gdn_norm_w, ret_decay, w_out, ln1_g, ln1_b, w_up, ffn_conv_w, ffn_conv_b,
              w_down, ln2_g, ln2_b):
    p = {'w_mod': w_mod, 'b_mod': b_mod, 'w_in': w_in, 'conv_w': conv_w, 'conv_b': conv_b,
         'hy_w1': hy_w1, 'hy_b1': hy_b1, 'hy_w2': hy_w2, 'hy_b2': hy_b2, 'hy_w3': hy_w3,
         'hy_freq': hy_freq, 'hy_bias': hy_bias, 'ssd_A_log': ssd_A_log, 'ssd_dt_bias': ssd_dt_bias,
         'ssd_D': ssd_D, 'ssd_norm_w': ssd_norm_w, 'gdn_A_log': gdn_A_log, 'gdn_dt_bias': gdn_dt_bias,
         'gdn_norm_w': gdn_norm_w, 'ret_decay': ret_decay, 'w_out': w_out, 'ln1_g': ln1_g, 'ln1_b': ln1_b,
         'w_up': w_up, 'ffn_conv_w': ffn_conv_w, 'ffn_conv_b': ffn_conv_b, 'w_down': w_down,
         'ln2_g': ln2_g, 'ln2_b': ln2_b}

    nb = x_prompt.shape[0]
    h = x_prompt
    ctx_cond = c_ctx[None, :]
    ssd_l, gdn_l, ret_l = [], [], []
    for l in range(DEPTH):
        z_ssd = jnp.zeros((nb, N_DIR, SSD_HEADS, SSD_HEAD_DIM, SSD_STATE), jnp.float32)
        z_gdn = jnp.zeros((nb, N_DIR, GDN_HEADS, GDN_HEAD_DIM, GDN_HEAD_DIM), jnp.float32)
        z_ret = jnp.zeros((nb, N_DIR, RET_HEADS, RET_HEAD_DIM, RET_HEAD_DIM), jnp.float32)
        h, (s1, s2, s3) = trunk_layer(h, ctx_cond, p, l, z_ssd, z_gdn, z_ret, None)
        ssd_l.append(s1)
        gdn_l.append(s2)
        ret_l.append(s3)
    y_prompt = h
    new_state_ssd = jnp.stack(ssd_l, 1)
    new_state_gdn = jnp.stack(gdn_l, 1)
    new_state_ret = jnp.stack(ret_l, 1)

    rope = grid_rope(x_sample.shape[1])
    h = x_sample
    for l in range(DEPTH):
        h, _ = trunk_layer(h, c, p, l, state_ssd[:, l], state_gdn[:, l], state_ret[:, l], rope)
    y_sample = h
    return (y_prompt, y_sample, new_state_ssd, new_state_gdn, new_state_ret)
```

```python
import functools
import math

import numpy as np
import jax
import jax.numpy as jnp
from jax import lax
from jax.experimental import pallas as pl
from jax.experimental.pallas import tpu as pltpu

F32 = jnp.float32
BF16 = jnp.bfloat16
HIGHEST = lax.Precision.HIGHEST

N_DIR = 2
CONV_W = 3
GROUP_WIDTH = 512
HY_ORDER = 2
HY_BANDS = 8
HY_EMB = 1 + 2 * HY_BANDS
HY_FFN = 64
HY_DECAY_TARGET = 1e-2
HY_FAST_PCT = 0.3
HY_SLOW_PCT = 1.5
SSD_HEADS = 8
SSD_HEAD_DIM = 64
SSD_GROUPS = 2
SSD_HPG = SSD_HEADS // SSD_GROUPS
SSD_STATE = 128
SSD_CHUNK = 128
GDN_HEADS = 4
GDN_HEAD_DIM = 128
GDN_CHUNK = 64
RET_HEADS = 4
RET_HEAD_DIM = 128
RET_CHUNK = 128
GRID_W = 64
ROPE_BASE = 10000.0
LN_EPS = 1e-5
RMS_EPS = 1e-6
N_CONV = 8 * GROUP_WIDTH

LANES = 128
SUBLANES = 8
VMEM_LIMIT_BYTES = 60 * 1024 * 1024

COL_HY_V, COL_HY_X1, COL_HY_X2, COL_S_X, COL_S_BC, COL_G_Q, COL_G_K, COL_G_V = range(8)
COL_S_Z, COL_G_Z, COL_R_Q, COL_R_K, COL_R_V, COL_R_G = range(8, 14)
N_MAIN = 14 * GROUP_WIDTH
SM_SDT = 0
SM_GBETA = 16
SM_GA = 24


def _silu(x):
    return x * jax.nn.sigmoid(x)


def _softplus(x):
    return jnp.maximum(x, 0.0) + jnp.log1p(jnp.exp(-jnp.abs(x)))


def _dot(a, b):
    return jnp.dot(a.astype(BF16), b.astype(BF16), preferred_element_type=F32)


def _dot_nt(a, b):
    return lax.dot_general(a.astype(BF16), b.astype(BF16), (((1,), (1,)), ((), ())), preferred_element_type=F32)


def _dot_tn(a, b):
    return lax.dot_general(a.astype(BF16), b.astype(BF16), (((0,), (0,)), ((), ())), preferred_element_type=F32)


def _dot_hi(a, b):
    return jnp.dot(a, b, preferred_element_type=F32, precision=HIGHEST)


def _params(*semantics):
    return pltpu.CompilerParams(dimension_semantics=semantics, vmem_limit_bytes=VMEM_LIMIT_BYTES)


def _tri(n, lower):
    r = lax.broadcasted_iota(jnp.int32, (n, n), 0)
    c = lax.broadcasted_iota(jnp.int32, (n, n), 1)
    return jnp.where((r >= c) if lower else (r <= c), 1.0, 0.0).astype(F32)


def _mod_kernel(c_ref, w_ref, b_ref, o_ref):
    a = _silu(c_ref[...])
    o_ref[...] = _dot(a, w_ref[...]) + b_ref[...]


def _mod_call(cond, w_mod, b_mod):
    depth, d, n = w_mod.shape
    rows = cond.shape[0]
    tn = 1024
    return pl.pallas_call(
        _mod_kernel,
        out_shape=jax.ShapeDtypeStruct((depth, rows, n), F32),
        grid=(depth, n // tn),
        in_specs=[
            pl.BlockSpec((rows, d), lambda l, j: (0, 0)),
            pl.BlockSpec((None, d, tn), lambda l, j: (l, 0, j)),
            pl.BlockSpec((None, 1, tn), lambda l, j: (l, 0, j)),
        ],
        out_specs=pl.BlockSpec((None, rows, tn), lambda l, j: (l, 0, j)),
        compiler_params=_params("arbitrary", "arbitrary"),
        name="adaln_mod",
    )(cond, w_mod, b_mod.reshape(depth, 1, n))


def _mod_spec(layer, k, d, rows_per_tile, l_sample, n_dec):
    tiles_per_seq = l_sample // rows_per_tile

    def index_map(i, *_):
        return (layer, jnp.minimum(i // tiles_per_seq, n_dec), 0, k)

    return pl.BlockSpec((None, None, 1, d), index_map)


def _modulate_kernel(xs_ref, xp_ref, sh_ref, sc_ref, o_ref, *, n_sample_tiles):
    i = pl.program_id(0)
    x = jnp.where(i < n_sample_tiles, xs_ref[...], xp_ref[...])
    o_ref[...] = (x * (1.0 + sc_ref[...]) + sh_ref[...]).astype(BF16)


def _two_group_specs(tm, d, n_sample_tiles):
    return [
        pl.BlockSpec((tm, d), lambda i, *_: (jnp.minimum(i, n_sample_tiles - 1), 0)),
        pl.BlockSpec((tm, d), lambda i, *_: (jnp.maximum(i - n_sample_tiles, 0), 0)),
    ]


def _modulate_call(xs, xp, mod4, layer, cfg):
    tm, d = 256, cfg["d"]
    t = cfg["t"]
    ns = cfg["ts"] // tm
    return pl.pallas_call(
        functools.partial(_modulate_kernel, n_sample_tiles=ns),
        out_shape=jax.ShapeDtypeStruct((t, d), BF16),
        grid=(t // tm,),
        in_specs=_two_group_specs(tm, d, ns)
        + [_mod_spec(layer, 0, d, tm, cfg["ls"], cfg["n_dec"]), _mod_spec(layer, 1, d, tm, cfg["ls"], cfg["n_dec"])],
        out_specs=pl.BlockSpec((tm, d), lambda i: (i, 0)),
        compiler_params=_params("arbitrary"),
        name="modulate_in",
    )(xs, xp, mod4, mod4)


def _dwconv_rows(h, cw, cb, seq_len):
    rows = h.shape[0]
    pos = lax.broadcasted_iota(jnp.int32, (rows, 1), 0) & (seq_len - 1)
    prev = jnp.where(pos == 0, 0.0, pltpu.roll(h, 1, axis=0))
    nxt = jnp.where(pos == seq_len - 1, 0.0, pltpu.roll(h, rows - 1, axis=0))
    return prev * cw[0:1] + h * cw[1:2] + nxt * cw[2:3] + cb


def _inproj_kernel(u_ref, w_ref, ws_ref, cw_ref, cb_ref, o_ref, os_ref, *, n_conv_tiles, n_sample_tiles, ls, lp):
    i = pl.program_id(0)
    j = pl.program_id(1)
    seq_len = jnp.where(i < n_sample_tiles, ls, lp)
    u = u_ref[...]
    h = jnp.dot(u, w_ref[...], preferred_element_type=F32)

    @pl.when(j < n_conv_tiles)
    def _():
        o_ref[...] = _dwconv_rows(h, cw_ref[...], cb_ref[...], seq_len)

    @pl.when(j >= n_conv_tiles)
    def _():
        o_ref[...] = h

    @pl.when(j == 0)
    def _():
        os_ref[...] = jnp.dot(u, ws_ref[...], preferred_element_type=F32)


def _inproj_call(u, w_main, w_small, cw, cb, cfg):
    t, d = u.shape
    tm, tn = cfg["ls"], 512
    n_conv_tiles = N_CONV // tn
    kern = functools.partial(
        _inproj_kernel, n_conv_tiles=n_conv_tiles, n_sample_tiles=cfg["ts"] // tm, ls=cfg["ls"], lp=cfg["lp"]
    )
    conv_col = lambda i, j: (0, jnp.minimum(j, n_conv_tiles - 1))
    return pl.pallas_call(
        kern,
        out_shape=[jax.ShapeDtypeStruct((t, N_MAIN), F32), jax.ShapeDtypeStruct((t, LANES), F32)],
        grid=(t // tm, N_MAIN // tn),
        in_specs=[
            pl.BlockSpec((tm, d), lambda i, j: (i, 0)),
            pl.BlockSpec((d, tn), lambda i, j: (0, j)),
            pl.BlockSpec((d, LANES), lambda i, j: (0, 0)),
            pl.BlockSpec((CONV_W, tn), conv_col),
            pl.BlockSpec((1, tn), conv_col),
        ],
        out_specs=[pl.BlockSpec((tm, tn), lambda i, j: (i, j)), pl.BlockSpec((tm, LANES), lambda i, j: (i, 0))],
        compiler_params=_params("arbitrary", "arbitrary"),
        name="in_proj_conv",
    )(u, w_main, w_small, cw, cb)


def _dft_tables(l):
    n = 2 * l
    k = jnp.arange(l, dtype=jnp.int32)
    m = (k[:, None] * k[None, :]) % n
    ang = (2.0 * math.pi / n) * m.astype(F32)
    cos, sin = jnp.cos(ang), jnp.sin(ang)
    alt = jnp.where(k % 2 == 0, 1.0, -1.0).astype(F32)
    a_im = jnp.where(k[:, None] == 0, alt[None, :], -sin)
    a = jnp.concatenate([cos, a_im], axis=0)
    ck = jnp.where(k == 0, 1.0, 2.0).astype(F32) / n
    b_re = cos * ck[None, :]
    b_im = jnp.where(k[None, :] == 0, alt[:, None] / n, -sin * (2.0 / n))
    b = jnp.concatenate([b_re, b_im], axis=1)
    return a.astype(BF16), b.astype(BF16)


def _hy_feats(l):
    pos = jnp.arange(l, dtype=F32)
    t = pos / (l - 1)
    bands = jnp.linspace(1e-4, HY_BANDS - 1, HY_BANDS, dtype=F32)
    ang = (2.0 * math.pi / l) * pos[:, None] * bands
    feats = jnp.concatenate([t[:, None], jnp.cos(ang), -jnp.sin(ang)], -1)
    feats = jnp.pad(feats, ((0, 0), (0, LANES - HY_EMB)))
    return feats, t[:, None]


def _hy_deltas(c):
    d = jnp.abs(
        jnp.linspace(math.log(HY_DECAY_TARGET) / HY_SLOW_PCT, math.log(HY_DECAY_TARGET) / HY_FAST_PCT, c, dtype=F32)
    )
    return d[None, :]


def _hy_filter_kernel(
    feats_ref, t_ref, dl_ref, w1_ref, b1_ref, w2_ref, b2_ref, w3_ref, fq_ref, are_ref, aim_ref,
    p_ref, r_ref, s_ref, sum_scr, dif_scr, nyq_scr,
):
    f = pl.program_id(0)
    l = feats_ref.shape[0]
    c = dl_ref.shape[1]
    kb = are_ref.shape[0]

    @pl.when(f == 0)
    def _():
        fq = fq_ref[...]
        h = jnp.sin(fq[0:1] * (_dot_hi(feats_ref[...], w1_ref[...]) + b1_ref[...]))
        h = jnp.sin(fq[1:2] * (_dot_hi(h, w2_ref[...]) + b2_ref[...]))
        decay = jnp.exp(-t_ref[...] * dl_ref[...])
        row = lax.broadcasted_iota(jnp.int32, (l, 1), 0)
        sign = jnp.where((row & 1) == 0, 1.0, -1.0)
        for o in range(HY_ORDER):
            fwd = _dot_hi(h, w3_ref[:, (2 * o) * c:(2 * o + 1) * c]) * decay
            bwd = _dot_hi(h, w3_ref[:, (2 * o + 1) * c:(2 * o + 2) * c]) * decay
            bwd = jnp.where(row == 0, 0.0, bwd)
            norm = jnp.sum(jnp.abs(fwd), axis=0, keepdims=True) + jnp.sum(jnp.abs(bwd), axis=0, keepdims=True)
            fwd = fwd / norm
            bwd = bwd / norm
            ssum = fwd + bwd
            sum_scr[o] = ssum.astype(BF16)
            dif_scr[o] = (fwd - bwd).astype(BF16)
            nyq_scr[o] = jnp.broadcast_to(jnp.sum(ssum * sign, axis=0, keepdims=True), (SUBLANES, c))

    grow = f * kb + lax.broadcasted_iota(jnp.int32, (kb, 1), 0)
    for o in range(HY_ORDER):
        kre = jnp.dot(are_ref[...], sum_scr[o], preferred_element_type=F32)
        kim = jnp.dot(aim_ref[...], dif_scr[o], preferred_element_type=F32)
        p_ref[o] = kre
        r_ref[o] = jnp.where(grow == 0, 0.0, kim)
        s_ref[o] = jnp.where(grow == 0, nyq_scr[o][0:1], kre)


def _hy_filter_call(l, a_tab, w1, b1, w2, b2, w3, freq):
    c = GROUP_WIDTH
    kb = min(l, 256)
    nf = l // kb
    feats, t = _hy_feats(l)
    w1p = jnp.pad(w1, ((0, LANES - HY_EMB), (0, 0)))
    full = lambda shape: pl.BlockSpec(shape, lambda f: tuple(0 for _ in shape))
    out = jax.ShapeDtypeStruct((HY_ORDER, l, c), F32)
    out_spec = pl.BlockSpec((HY_ORDER, kb, c), lambda f: (0, f, 0))
    return pl.pallas_call(
        _hy_filter_kernel,
        out_shape=[out, out, out],
        grid=(nf,),
        in_specs=[
            full((l, LANES)), full((l, 1)), full((1, c)), full((LANES, HY_FFN)), full((1, HY_FFN)),
            full((HY_FFN, HY_FFN)), full((1, HY_FFN)), full((HY_FFN, 2 * HY_ORDER * c)), full((2, HY_FFN)),
            pl.BlockSpec((kb, l), lambda f: (f, 0)),
            pl.BlockSpec((kb, l), lambda f: (f + nf, 0)),
        ],
        out_specs=[out_spec, out_spec, out_spec],
        scratch_shapes=[
            pltpu.VMEM((HY_ORDER, l, c), BF16), pltpu.VMEM((HY_ORDER, l, c), BF16),
            pltpu.VMEM((HY_ORDER, SUBLANES, c), F32),
        ],
        compiler_params=_params("arbitrary"),
        name="hyena_filter",
    )(feats, t, _hy_deltas(c), w1p, b1[None, :], w2, b2[None, :], w3, freq, a_tab, a_tab)


def _hy_conv_kernel(z_ref, g_ref, bias_ref, are_ref, aim_ref, bre_ref, bim_ref, p_ref, r_ref, s_ref, o_ref, zb_scr):
    f = pl.program_id(2)

    @pl.when(f == 0)
    def _():
        zb_scr[...] = z_ref[...].astype(BF16)
        o_ref[...] = jnp.zeros_like(o_ref)

    zb = zb_scr[...]
    re = jnp.dot(are_ref[...], zb, preferred_element_type=F32)
    im = jnp.dot(aim_ref[...], zb, preferred_element_type=F32)
    p, r, s = p_ref[...], r_ref[...], s_ref[...]
    re2 = re * p - im * r
    im2 = re * r + im * s
    o_ref[...] += _dot(bre_ref[...], re2) + _dot(bim_ref[...], im2)

    @pl.when(f == pl.num_programs(2) - 1)
    def _():
        o_ref[...] = g_ref[...] * (o_ref[...] + z_ref[...] * bias_ref[...])


def _hy_conv_call(z, z_col, z_row0, g, g_col, g_row0, bias, a_tab, b_tab, prs, order, l, n_seq):
    c = GROUP_WIDTH
    ct = 256
    ncb = c // ct
    kb = min(l, 256)
    nf = l // kb
    p, r, s = prs
    zspec = pl.BlockSpec((l, ct), lambda b, cb, f: (z_row0 // l + b, z_col * ncb + cb))
    gspec = pl.BlockSpec((l, ct), lambda b, cb, f: (g_row0 // l + b, g_col * ncb + cb))
    kspec = pl.BlockSpec((None, kb, ct), lambda b, cb, f: (order, f, cb))
    return pl.pallas_call(
        _hy_conv_kernel,
        out_shape=jax.ShapeDtypeStruct((n_seq * l, c), F32),
        grid=(n_seq, ncb, nf),
        in_specs=[
            zspec, gspec,
            pl.BlockSpec((None, 1, ct), lambda b, cb, f: (order, 0, cb)),
            pl.BlockSpec((kb, l), lambda b, cb, f: (f, 0)),
            pl.BlockSpec((kb, l), lambda b, cb, f: (f + nf, 0)),
            pl.BlockSpec((l, kb), lambda b, cb, f: (0, f)),
            pl.BlockSpec((l, kb), lambda b, cb, f: (0, f + nf)),
            kspec, kspec, kspec,
        ],
        out_specs=pl.BlockSpec((l, ct), lambda b, cb, f: (b, cb)),
        scratch_shapes=[pltpu.VMEM((l, ct), BF16)],
        compiler_params=_params("arbitrary", "arbitrary", "arbitrary"),
        name="hyena_conv",
    )(z, g, bias.reshape(HY_ORDER, 1, c), a_tab, a_tab, b_tab, b_tab, p, r, s)


def _chunk_tables(cfg, q):
    fwd, bwd, first, last, seq = [], [], [], [], []
    sid = 0
    for n_seq, l, row0 in ((cfg["n_dec"], cfg["ls"], 0), (cfg["n_ctx"], cfg["lp"], cfg["ts"])):
        nc = l // q
        for b in range(n_seq):
            base = (row0 + b * l) // q
            for c in range(nc):
                fwd.append(base + c)
                bwd.append(base + nc - 1 - c)
                first.append(int(c == 0))
                last.append(int(c == nc - 1))
                seq.append(sid)
            sid += 1
    return tuple(jnp.asarray(np.asarray(a, np.int32)) for a in (fwd, bwd, first, last, seq))


def _scan_masks(q, d):
    r = lax.broadcasted_iota(jnp.int32, (q, q), 0)
    c = lax.broadcasted_iota(jnp.int32, (q, q), 1)
    return ((r >= c), (r > c)) if d == 0 else ((r <= c), (r < c))


def _cumsum_pair(a_col, a_row, q, d):
    lo, up = _tri(q, True), _tri(q, False)
    if d == 0:
        return _dot_hi(lo, a_col), _dot_hi(a_row, up)
    return _dot_hi(up, a_col), _dot_hi(a_row, lo)


def _ssd_kernel(
    fwd_t, bwd_t, first_t, last_t, seq_t,
    xf_ref, bcf_ref, smf_ref, xb_ref, bcb_ref, smb_ref, pr_ref, pc_ref, dsk_ref, h0_ref,
    yf_ref, yb_ref, hout_ref, h_scr,
):
    s = pl.program_id(0)
    q, p, n = SSD_CHUNK, SSD_HEAD_DIM, SSD_STATE

    @pl.when(first_t[s] == 1)
    def _():
        h_scr[...] = h0_ref[...]

    bias_r, alog_r = pr_ref[0:1], pr_ref[1:2]
    bias_c, alog_c = pc_ref[:, 0:1], pc_ref[:, 1:2]
    for d in range(N_DIR):
        x_ref, bc_ref, sm_ref, y_ref = (xf_ref, bcf_ref, smf_ref, yf_ref) if d == 0 else (xb_ref, bcb_ref, smb_ref, yb_ref)
        x = _silu(x_ref[...])
        bc = _silu(bc_ref[...])
        sm = sm_ref[...]
        dt_c = _softplus(sm + bias_r)
        a_c = dt_c * (-jnp.exp(alog_r))
        sm_t = sm.T
        a_r = _softplus(sm_t + bias_c) * (-jnp.exp(alog_c))
        acs_c, acs_r = _cumsum_pair(a_c, a_r[SM_SDT:SM_SDT + N_DIR * SSD_HEADS], q, d)
        incl, _ = _scan_masks(q, d)
        end = q - 1 if d == 0 else 0
        ys = []
        for g in range(SSD_GROUPS):
            bm = bc[:, g * n:(g + 1) * n]
            cm = bc[:, (SSD_GROUPS + g) * n:(SSD_GROUPS + g + 1) * n]
            cb = _dot_nt(cm, bm)
            for e in range(SSD_HPG):
                hd = g * SSD_HPG + e
                idx = d * SSD_HEADS + hd
                col = acs_c[:, idx:idx + 1]
                row = acs_r[idx:idx + 1, :]
                lmat = jnp.exp(jnp.where(incl, col - row, -1e30))
                xh = x[:, hd * p:(hd + 1) * p]
                xs = xh * dt_c[:, idx:idx + 1]
                hst = h_scr[d, hd]
                y = _dot(cb * lmat, xs) + _dot_nt(cm, hst) * jnp.exp(col)
                ys.append(y + dsk_ref[d:d + 1, hd * p:(hd + 1) * p] * xh)
                tot = col[end:end + 1]
                states = _dot_tn(xs * jnp.exp(tot - col), bm)
                h_scr[d, hd] = hst * jnp.exp(tot) + states
        y_ref[...] = jnp.concatenate(ys, axis=-1)

    @pl.when(last_t[s] == 1)
    def _():
        hout_ref[...] = h_scr[...]


def _ssd_call(proj, small, tables, h0, dt_bias, a_log, d_skip, cfg):
    q, w = SSD_CHUNK, GROUP_WIDTH
    t = proj.shape[0]
    n_steps = tables[0].shape[0]
    nlane = N_DIR * SSD_HEADS
    pr = jnp.zeros((SUBLANES, LANES), F32)
    pr = pr.at[0, SM_SDT:SM_SDT + nlane].set(dt_bias.reshape(-1)).at[1, SM_SDT:SM_SDT + nlane].set(a_log.reshape(-1))
    pc = pr.T
    dsk = jnp.repeat(d_skip, SSD_HEAD_DIM, axis=-1)
    st_shape = (N_DIR, SSD_HEADS, SSD_HEAD_DIM, SSD_STATE)
    fmap = lambda col: (lambda s, fw, bw, fi, la, sq: (fw[s], col))
    bmap = lambda col: (lambda s, fw, bw, fi, la, sq: (bw[s], col))
    const = lambda shape: pl.BlockSpec(shape, lambda s, *_: tuple(0 for _ in shape))
    st_spec = pl.BlockSpec((None,) + st_shape, lambda s, fw, bw, fi, la, sq: (sq[s], 0, 0, 0, 0))
    grid_spec = pltpu.PrefetchScalarGridSpec(
        num_scalar_prefetch=5,
        grid=(n_steps,),
        in_specs=[
            pl.BlockSpec((q, w), fmap(COL_S_X)), pl.BlockSpec((q, w), fmap(COL_S_BC)), pl.BlockSpec((q, LANES), fmap(0)),
            pl.BlockSpec((q, w), bmap(COL_S_X)), pl.BlockSpec((q, w), bmap(COL_S_BC)), pl.BlockSpec((q, LANES), bmap(0)),
            const((SUBLANES, LANES)), const((LANES, SUBLANES)), const((N_DIR, w)), st_spec,
        ],
        out_specs=[pl.BlockSpec((q, w), fmap(0)), pl.BlockSpec((q, w), bmap(0)), st_spec],
        scratch_shapes=[pltpu.VMEM(st_shape, F32)],
    )
    n_seq = cfg["n_dec"] + cfg["n_ctx"]
    return pl.pallas_call(
        _ssd_kernel,
        out_shape=[
            jax.ShapeDtypeStruct((t, w), F32), jax.ShapeDtypeStruct((t, w), F32),
            jax.ShapeDtypeStruct((n_seq,) + st_shape, F32),
        ],
        grid_spec=grid_spec,
        compiler_params=_params("arbitrary"),
        name="ssd_scan",
    )(*tables, proj, proj, small, proj, proj, small, pr, pc, dsk, h0)


def _unit_tri_inverse(a):
    n = a.shape[0]
    r = lax.broadcasted_iota(jnp.int32, (n, n), 0)
    c = lax.broadcasted_iota(jnp.int32, (n, n), 1)
    m = -a
    inv = jnp.where(r == c, 1.0, 0.0) + m
    k = 2
    while k < n:
        m = _dot_hi(m, m)
        inv = inv + _dot_hi(inv, m)
        k *= 2
    return inv


def _l2norm_heads(x, heads, dim, scale):
    outs = []
    for h in range(heads):
        xh = x[:, h * dim:(h + 1) * dim]
        outs.append(xh * (lax.rsqrt(jnp.sum(xh * xh, axis=-1, keepdims=True) + 1e-6) * scale))
    return outs


def _gdn_kernel(
    fwd_t, bwd_t, first_t, last_t, seq_t,
    qf_ref, kf_ref, vf_ref, smf_ref, qb_ref, kb_ref, vb_ref, smb_ref, pr_ref, pc_ref, s0_ref,
    of_ref, ob_ref, sout_ref, s_scr,
):
    s = pl.program_id(0)
    q, dim = GDN_CHUNK, GDN_HEAD_DIM

    @pl.when(first_t[s] == 1)
    def _():
        s_scr[...] = s0_ref[...]

    bias_r, alog_r = pr_ref[0:1], pr_ref[1:2]
    bias_c, alog_c = pc_ref[:, 0:1], pc_ref[:, 1:2]
    for d in range(N_DIR):
        q_ref, k_ref, v_ref, sm_ref, o_ref = (
            (qf_ref, kf_ref, vf_ref, smf_ref, of_ref) if d == 0 else (qb_ref, kb_ref, vb_ref, smb_ref, ob_ref)
        )
        qs = _l2norm_heads(_silu(q_ref[...]), GDN_HEADS, dim, dim ** -0.5)
        ks = _l2norm_heads(_silu(k_ref[...]), GDN_HEADS, dim, 1.0)
        v = _silu(v_ref[...])
        sm = sm_ref[...]
        beta_c = jax.nn.sigmoid(sm)
        g_c = -jnp.exp(alog_r) * _softplus(sm + bias_r)
        g_r = -jnp.exp(alog_c) * _softplus(sm.T + bias_c)
        gc_c, gc_r = _cumsum_pair(g_c, g_r[SM_GA:SM_GA + N_DIR * GDN_HEADS], q, d)
        incl, strict = _scan_masks(q, d)
        end = q - 1 if d == 0 else 0
        outs = []
        for h in range(GDN_HEADS):
            idx = d * GDN_HEADS + h
            col = gc_c[:, SM_GA + idx:SM_GA + idx + 1]
            row = gc_r[idx:idx + 1, :]
            beta = beta_c[:, SM_GBETA + idx:SM_GBETA + idx + 1]
            dmask = jnp.exp(jnp.where(incl, col - row, -1e30))
            kh, qh = ks[h], qs[h]
            vh = v[:, h * dim:(h + 1) * dim]
            a = jnp.where(strict, beta * _dot_nt(kh, kh) * dmask, 0.0)
            inv = _unit_tri_inverse(a)
            ecol = jnp.exp(col)
            u = _dot_hi(inv, vh * beta)
            w = _dot_hi(inv, kh * (beta * ecol))
            attn = _dot_nt(qh, kh) * dmask
            st = s_scr[d, h]
            vv = u - _dot(w, st)
            outs.append(_dot(qh * ecol, st) + _dot(attn, vv))
            tot = col[end:end + 1]
            s_scr[d, h] = st * jnp.exp(tot) + _dot_tn(kh * jnp.exp(tot - col), vv)
        o_ref[...] = jnp.concatenate(outs, axis=-1)

    @pl.when(last_t[s] == 1)
    def _():
        sout_ref[...] = s_scr[...]


def _gdn_call(proj, small, tables, s0, a_log, dt_bias, cfg):
    q, w = GDN_CHUNK, GROUP_WIDTH
    t = proj.shape[0]
    n_steps = tables[0].shape[0]
    nlane = N_DIR * GDN_HEADS
    pr = jnp.zeros((SUBLANES, LANES), F32)
    pr = pr.at[0, SM_GA:SM_GA + nlane].set(dt_bias.reshape(-1)).at[1, SM_GA:SM_GA + nlane].set(a_log.reshape(-1))
    pc = pr.T
    st_shape = (N_DIR, GDN_HEADS, GDN_HEAD_DIM, GDN_HEAD_DIM)
    fmap = lambda col: (lambda s, fw, bw, fi, la, sq: (fw[s], col))
    bmap = lambda col: (lambda s, fw, bw, fi, la, sq: (bw[s], col))
    const = lambda shape: pl.BlockSpec(shape, lambda s, *_: tuple(0 for _ in shape))
    st_spec = pl.BlockSpec((None,) + st_shape, lambda s, fw, bw, fi, la, sq: (sq[s], 0, 0, 0, 0))
    blk = lambda m, col: pl.BlockSpec((q, w), m(col))
    grid_spec = pltpu.PrefetchScalarGridSpec(
        num_scalar_prefetch=5,
        grid=(n_steps,),
        in_specs=[
            blk(fmap, COL_G_Q), blk(fmap, COL_G_K), blk(fmap, COL_G_V), pl.BlockSpec((q, LANES), fmap(0)),
            blk(bmap, COL_G_Q), blk(bmap, COL_G_K), blk(bmap, COL_G_V), pl.BlockSpec((q, LANES), bmap(0)),
            const((SUBLANES, LANES)), const((LANES, SUBLANES)), st_spec,
        ],
        out_specs=[pl.BlockSpec((q, w), fmap(0)), pl.BlockSpec((q, w), bmap(0)), st_spec],
        scratch_shapes=[pltpu.VMEM(st_shape, F32)],
    )
    n_seq = cfg["n_dec"] + cfg["n_ctx"]
    return pl.pallas_call(
        _gdn_kernel,
        out_shape=[
            jax.ShapeDtypeStruct((t, w), F32), jax.ShapeDtypeStruct((t, w), F32),
            jax.ShapeDtypeStruct((n_seq,) + st_shape, F32),
        ],
        grid_spec=grid_spec,
        compiler_params=_params("arbitrary"),
        name="gdn_scan",
    )(*tables, proj, proj, proj, small, proj, proj, proj, small, pr, pc, s0)


def _rope_tables(cfg):
    l = cfg["ls"]
    pos = jnp.arange(l)
    r = (pos // GRID_W).astype(F32)
    col = (pos % GRID_W).astype(F32)
    nf = RET_HEAD_DIM // 4
    inv = jnp.power(ROPE_BASE, -jnp.arange(nf, dtype=F32) / nf)
    ang = jnp.concatenate([r[:, None] * inv, col[:, None] * inv], -1)
    cos, sin = jnp.cos(ang), jnp.sin(ang)
    cos2 = jnp.concatenate([cos, cos], -1)
    sin2 = jnp.concatenate([-sin, sin], -1)
    ident = jnp.ones((RET_CHUNK, RET_HEAD_DIM), F32)
    return jnp.concatenate([cos2, ident], 0), jnp.concatenate([sin2, 0.0 * ident], 0)


def _rope_block_table(cfg):
    q = RET_CHUNK
    nc_s, nc_p = cfg["ls"] // q, cfg["lp"] // q
    fwd = [c for _ in range(cfg["n_dec"]) for c in range(nc_s)] + [nc_s] * (cfg["n_ctx"] * nc_p)
    bwd = [nc_s - 1 - c for _ in range(cfg["n_dec"]) for c in range(nc_s)] + [nc_s] * (cfg["n_ctx"] * nc_p)
    return jnp.asarray(np.asarray(fwd, np.int32)), jnp.asarray(np.asarray(bwd, np.int32))


def _ret_kernel(
    fwd_t, bwd_t, first_t, last_t, seq_t, rf_t, rb_t,
    qf_ref, kf_ref, vf_ref, cf_ref, sf_ref, qb_ref, kb_ref, vb_ref, cb_ref, sb_ref, dec_ref, s0_ref,
    of_ref, ob_ref, sout_ref, s_scr,
):
    s = pl.program_id(0)
    q, dim = RET_CHUNK, RET_HEAD_DIM

    @pl.when(first_t[s] == 1)
    def _():
        s_scr[...] = s0_ref[...]

    lg_all = -jnp.exp(dec_ref[...])
    ri = lax.broadcasted_iota(jnp.int32, (q, q), 0)
    ci = lax.broadcasted_iota(jnp.int32, (q, q), 1)
    rpos = lax.broadcasted_iota(jnp.int32, (q, 1), 0)
    for d in range(N_DIR):
        q_ref, k_ref, v_ref, c_ref, sn_ref, o_ref = (
            (qf_ref, kf_ref, vf_ref, cf_ref, sf_ref, of_ref) if d == 0 else (qb_ref, kb_ref, vb_ref, cb_ref, sb_ref, ob_ref)
        )
        qa, ka, va = q_ref[...], k_ref[...], v_ref[...]
        cos, sin = c_ref[...], sn_ref[...]
        rel = ((ri - ci) if d == 0 else (ci - ri)).astype(F32)
        pos = (rpos if d == 0 else (q - 1 - rpos)).astype(F32)
        outs = []
        for h in range(RET_HEADS):
            idx = d * RET_HEADS + h
            lg = lg_all[idx:idx + 1, :]
            lg1 = lg[:, 0:1]
            qh = qa[:, h * dim:(h + 1) * dim]
            kh = ka[:, h * dim:(h + 1) * dim] * dim ** -0.5
            vh = va[:, h * dim:(h + 1) * dim]
            qh = qh * cos + pltpu.roll(qh, dim // 2, axis=1) * sin
            kh = kh * cos + pltpu.roll(kh, dim // 2, axis=1) * sin
            dmask = jnp.exp(jnp.where(rel >= 0, rel * lg, -1e30))
            attn = _dot_nt(qh, kh) * dmask
            st = s_scr[d, h]
            outs.append(_dot(qh * jnp.exp((pos + 1.0) * lg1), st) + _dot(attn, vh))
            s_scr[d, h] = st * jnp.exp(q * lg) + _dot_tn(kh * jnp.exp((q - 1.0 - pos) * lg1), vh)
        o_ref[...] = jnp.concatenate(outs, axis=-1)

    @pl.when(last_t[s] == 1)
    def _():
        sout_ref[...] = s_scr[...]


def _ret_call(proj, tables, rope_tabs, rope_blocks, s0, ret_decay, cfg):
    q, w, dim = RET_CHUNK, GROUP_WIDTH, RET_HEAD_DIM
    t = proj.shape[0]
    n_steps = tables[0].shape[0]
    dec = jnp.broadcast_to(ret_decay.reshape(-1, 1), (N_DIR * RET_HEADS, LANES))
    st_shape = (N_DIR, RET_HEADS, dim, dim)
    fmap = lambda col: (lambda s, fw, bw, fi, la, sq, rf, rb: (fw[s], col))
    bmap = lambda col: (lambda s, fw, bw, fi, la, sq, rf, rb: (bw[s], col))
    rfmap = lambda s, fw, bw, fi, la, sq, rf, rb: (rf[s], 0)
    rbmap = lambda s, fw, bw, fi, la, sq, rf, rb: (rb[s], 0)
    st_spec = pl.BlockSpec((None,) + st_shape, lambda s, fw, bw, fi, la, sq, rf, rb: (sq[s], 0, 0, 0, 0))
    blk = lambda m, col: pl.BlockSpec((q, w), m(col))
    rope = lambda m: pl.BlockSpec((q, dim), m)
    grid_spec = pltpu.PrefetchScalarGridSpec(
        num_scalar_prefetch=7,
        grid=(n_steps,),
        in_specs=[
            blk(fmap, COL_R_Q), blk(fmap, COL_R_K), blk(fmap, COL_R_V), rope(rfmap), rope(rfmap),
            blk(bmap, COL_R_Q), blk(bmap, COL_R_K), blk(bmap, COL_R_V), rope(rbmap), rope(rbmap),
            pl.BlockSpec((N_DIR * RET_HEADS, LANES), lambda s, *_: (0, 0)), st_spec,
        ],
        out_specs=[pl.BlockSpec((q, w), fmap(0)), pl.BlockSpec((q, w), bmap(0)), st_spec],
        scratch_shapes=[pltpu.VMEM(st_shape, F32)],
    )
    n_seq = cfg["n_dec"] + cfg["n_ctx"]
    cos2, sin2 = rope_tabs
    return pl.pallas_call(
        _ret_kernel,
        out_shape=[
            jax.ShapeDtypeStruct((t, w), F32), jax.ShapeDtypeStruct((t, w), F32),
            jax.ShapeDtypeStruct((n_seq,) + st_shape, F32),
        ],
        grid_spec=grid_spec,
        compiler_params=_params("arbitrary"),
        name="retention_scan",
    )(*tables, *rope_blocks, proj, proj, proj, cos2, sin2, proj, proj, proj, cos2, sin2, dec, s0)


def _layer_norm_rows(y, g, b):
    mu = jnp.mean(y, axis=-1, keepdims=True)
    yc = y - mu
    var = jnp.mean(yc * yc, axis=-1, keepdims=True)
    return yc * lax.rsqrt(var + LN_EPS) * g + b


def _outproj_kernel(
    hys_ref, hyp_ref, sf_ref, sb_ref, sz_ref, gf_ref, gb_ref, gz_ref, rf_ref, rb_ref, rg_ref,
    snw_ref, gnw_ref, w_ref, xs_ref, xp_ref, g1_ref, sh2_ref, sc2_ref, lng_ref, lnb_ref,
    x1_ref, u2_ref, *, n_sample_tiles, alpha, two_group_x,
):
    i = pl.program_id(0)
    is_sample = i < n_sample_tiles
    w = GROUP_WIDTH
    y_hy = jnp.where(is_sample, hys_ref[...], hyp_ref[...])
    y_ssd = (sf_ref[...] + sb_ref[...]) * _silu(sz_ref[...])
    y_ssd = y_ssd * lax.rsqrt(jnp.mean(y_ssd * y_ssd, axis=-1, keepdims=True) + RMS_EPS) * snw_ref[...]
    og = gf_ref[...] + gb_ref[...]
    orr = rf_ref[...] + rb_ref[...]
    gz = _silu(gz_ref[...])
    rg = _silu(rg_ref[...])
    gnw = gnw_ref[...]
    acc = _dot(y_hy, w_ref[0:w, :]) + _dot(y_ssd, w_ref[w:2 * w, :])
    for h in range(GDN_HEADS):
        sl = slice(h * GDN_HEAD_DIM, (h + 1) * GDN_HEAD_DIM)
        o = og[:, sl]
        o = o * lax.rsqrt(jnp.mean(o * o, axis=-1, keepdims=True) + RMS_EPS) * gnw * gz[:, sl]
        acc += _dot(o, w_ref[2 * w + h * GDN_HEAD_DIM:2 * w + (h + 1) * GDN_HEAD_DIM, :])
    for h in range(RET_HEADS):
        sl = slice(h * RET_HEAD_DIM, (h + 1) * RET_HEAD_DIM)
        o = orr[:, sl]
        mu = jnp.mean(o, axis=-1, keepdims=True)
        oc = o - mu
        o = oc * lax.rsqrt(jnp.mean(oc * oc, axis=-1, keepdims=True) + LN_EPS) * rg[:, sl]
        acc += _dot(o, w_ref[3 * w + h * RET_HEAD_DIM:3 * w + (h + 1) * RET_HEAD_DIM, :])
    x = jnp.where(is_sample, xs_ref[...], xp_ref[...]) if two_group_x else xs_ref[...]
    x1 = _layer_norm_rows(alpha * x + g1_ref[...] * acc, lng_ref[...], lnb_ref[...])
    x1_ref[...] = x1
    u2_ref[...] = (x1 * (1.0 + sc2_ref[...]) + sh2_ref[...]).astype(BF16)


def _outproj_call(hy_s, hy_p, ssd, gdn, ret, proj, ssd_nw, gdn_nw, w_out, x_in, mod4, layer, ln_g, ln_b, cfg):
    t, d, w = cfg["t"], cfg["d"], GROUP_WIDTH
    tm = 256
    ns = cfg["ts"] // tm
    two_group_x = isinstance(x_in, tuple)
    row = lambda col: pl.BlockSpec((tm, w), lambda i: (i, col))
    vec = lambda n: pl.BlockSpec((1, n), lambda i: (0, 0))
    mspec = lambda k: _mod_spec(layer, k, d, tm, cfg["ls"], cfg["n_dec"])
    if two_group_x:
        x_specs = _two_group_specs(tm, d, ns)
        x_args = list(x_in)
    else:
        x_specs = [pl.BlockSpec((tm, d), lambda i: (i, 0))] * 2
        x_args = [x_in, x_in]
    kern = functools.partial(_outproj_kernel, n_sample_tiles=ns, alpha=cfg["alpha"], two_group_x=two_group_x)
    return pl.pallas_call(
        kern,
        out_shape=[jax.ShapeDtypeStruct((t, d), F32), jax.ShapeDtypeStruct((t, d), BF16)],
        grid=(t // tm,),
        in_specs=_two_group_specs(tm, w, ns)
        + [row(0), row(0), row(COL_S_Z), row(0), row(0), row(COL_G_Z), row(0), row(0), row(COL_R_G)]
        + [vec(w), vec(GDN_HEAD_DIM), pl.BlockSpec((d, d), lambda i: (0, 0))]
        + x_specs
        + [mspec(2), mspec(3), mspec(4), vec(d), vec(d)],
        out_specs=[pl.BlockSpec((tm, d), lambda i: (i, 0)), pl.BlockSpec((tm, d), lambda i: (i, 0))],
        compiler_params=_params("arbitrary"),
        name="out_proj_ln",
    )(
        hy_s, hy_p, ssd[0], ssd[1], proj, gdn[0], gdn[1], proj, ret[0], ret[1], proj,
        ssd_nw[None, :], gdn_nw[None, :], w_out, *x_args, mod4, mod4, mod4, ln_g[None, :], ln_b[None, :],
    )


def _ffn_up_kernel(u_ref, wg_ref, wv_ref, cwg_ref, cbg_ref, cwv_ref, cbv_ref, o_ref, *, n_sample_tiles, ls, lp):
    i = pl.program_id(0)
    seq_len = jnp.where(i < n_sample_tiles, ls, lp)
    u = u_ref[...]
    gate = _dwconv_rows(jnp.dot(u, wg_ref[...], preferred_element_type=F32), cwg_ref[...], cbg_ref[...], seq_len)
    val = _dwconv_rows(jnp.dot(u, wv_ref[...], preferred_element_type=F32), cwv_ref[...], cbv_ref[...], seq_len)
    o_ref[...] = (_silu(gate) * val).astype(BF16)


def _ffn_up_call(u2, w_up, cw, cb, cfg):
    t, d = u2.shape
    dff = w_up.shape[1] // 2
    tm, tn = cfg["ls"], 256
    nj = dff // tn
    kern = functools.partial(_ffn_up_kernel, n_sample_tiles=cfg["ts"] // tm, ls=cfg["ls"], lp=cfg["lp"])
    gcol = lambda i, j: (0, j)
    vcol = lambda i, j: (0, j + nj)
    return pl.pallas_call(
        kern,
        out_shape=jax.ShapeDtypeStruct((t, dff), BF16),
        grid=(t // tm, nj),
        in_specs=[
            pl.BlockSpec((tm, d), lambda i, j: (i, 0)),
            pl.BlockSpec((d, tn), gcol), pl.BlockSpec((d, tn), vcol),
            pl.BlockSpec((CONV_W, tn), gcol), pl.BlockSpec((1, tn), gcol),
            pl.BlockSpec((CONV_W, tn), vcol), pl.BlockSpec((1, tn), vcol),
        ],
        out_specs=pl.BlockSpec((tm, tn), lambda i, j: (i, j)),
        compiler_params=_params("arbitrary", "arbitrary"),
        name="ffn_up_conv_glu",
    )(u2, w_up, w_up, cw, cb, cw, cb)


def _ffn_down_kernel(a_ref, w_ref, x1_ref, g2_ref, shn_ref, scn_ref, lng_ref, lnb_ref, x2_ref, un_ref, acc_scr, *, alpha):
    k = pl.program_id(1)

    @pl.when(k == 0)
    def _():
        acc_scr[...] = jnp.zeros_like(acc_scr)

    acc_scr[...] += jnp.dot(a_ref[...], w_ref[...], preferred_element_type=F32)

    @pl.when(k == pl.num_programs(1) - 1)
    def _():
        x2 = _layer_norm_rows(alpha * x1_ref[...] + g2_ref[...] * acc_scr[...], lng_ref[...], lnb_ref[...])
        x2_ref[...] = x2
        un_ref[...] = (x2 * (1.0 + scn_ref[...]) + shn_ref[...]).astype(BF16)


def _ffn_down_call(act, w_down, x1, mod4, layer, next_layer, ln_g, ln_b, cfg):
    t, dff = act.shape
    d = cfg["d"]
    tm, tk = 512, 512
    if dff % tk:
        tk = dff
    vec = lambda n: pl.BlockSpec((1, n), lambda i, k: (0, 0))
    mspec = lambda lay, kk: _mod_spec(lay, kk, d, tm, cfg["ls"], cfg["n_dec"])
    return pl.pallas_call(
        functools.partial(_ffn_down_kernel, alpha=cfg["alpha"]),
        out_shape=[jax.ShapeDtypeStruct((t, d), F32), jax.ShapeDtypeStruct((t, d), BF16)],
        grid=(t // tm, dff // tk),
        in_specs=[
            pl.BlockSpec((tm, tk), lambda i, k: (i, k)),
            pl.BlockSpec((tk, d), lambda i, k: (k, 0)),
            pl.BlockSpec((tm, d), lambda i, k: (i, 0)),
            mspec(layer, 5), mspec(next_layer, 0), mspec(next_layer, 1), vec(d), vec(d),
        ],
        out_specs=[pl.BlockSpec((tm, d), lambda i, k: (i, 0)), pl.BlockSpec((tm, d), lambda i, k: (i, 0))],
        scratch_shapes=[pltpu.VMEM((tm, d), F32)],
        compiler_params=_params("arbitrary", "arbitrary"),
        name="ffn_down_ln",
    )(act, w_down, x1, mod4, mod4, mod4, ln_g[None, :], ln_b[None, :])


def _reorder_w_in(w_in):
    w = GROUP_WIDTH
    o = N_CONV
    s_z = w_in[..., o:o + w]
    o += w
    s_dt = w_in[..., o:o + N_DIR * SSD_HEADS]
    o += N_DIR * SSD_HEADS
    g_z = w_in[..., o:o + w]
    o += w
    g_beta = w_in[..., o:o + N_DIR * GDN_HEADS]
    o += N_DIR * GDN_HEADS
    g_a = w_in[..., o:o + N_DIR * GDN_HEADS]
    o += N_DIR * GDN_HEADS
    rest = w_in[..., o:]
    main = jnp.concatenate([w_in[..., :N_CONV], s_z, g_z, rest], axis=-1).astype(BF16)
    small = jnp.concatenate([s_dt, g_beta, g_a], axis=-1)
    small = jnp.pad(small, ((0, 0), (0, 0), (0, LANES - small.shape[-1]))).astype(BF16)
    return main, small


def kernel(x_prompt, x_sample, state_ssd, state_gdn, state_ret, c, c_ctx, w_mod, b_mod, w_in, conv_w, conv_b,
           hy_w1, hy_b1, hy_w2, hy_b2, hy_w3, hy_freq, hy_bias, ssd_A_log, ssd_dt_bias, ssd_D, ssd_norm_w,
           gdn_A_log, gdn_dt_bias, gdn_norm_w, ret_decay, w_out, ln1_g, ln1_b, w_up, ffn_conv_w, ffn_conv_b,
           w_down, ln2_g, ln2_b):
    n_ctx, lp, d = x_prompt.shape
    n_dec, ls, _ = x_sample.shape
    depth = w_in.shape[0]
    ts, tp = n_dec * ls, n_ctx * lp
    assert ls & (ls - 1) == 0 and lp & (lp - 1) == 0 and tp % ls == 0 and ls % lp == 0
    assert d == 4 * GROUP_WIDTH and w_in.shape[2] == N_MAIN + 2 * N_DIR * (SSD_HEADS // 2 + GDN_HEADS)
    cfg = dict(d=d, ls=ls, lp=lp, n_dec=n_dec, n_ctx=n_ctx, ts=ts, tp=tp, t=ts + tp, alpha=(2 * depth) ** 0.25)

    xs = x_sample.reshape(ts, d)
    xp = x_prompt.reshape(tp, d)

    mod_rows = -(-(n_dec + 1) // SUBLANES) * SUBLANES
    cond = jnp.concatenate([c, c_ctx[None, :], jnp.zeros((mod_rows - n_dec - 1, d), F32)], axis=0)
    mod = _mod_call(cond, w_mod, b_mod)
    mod4 = mod.reshape(depth, mod_rows, 1, 6 * d)

    w_main, w_small = _reorder_w_in(w_in)
    w_out_b, w_up_b, w_down_b = w_out.astype(BF16), w_up.astype(BF16), w_down.astype(BF16)

    dft = {l: _dft_tables(l) for l in (ls, lp)}
    rope_tabs = _rope_tables(cfg)
    rope_blocks = _rope_block_table(cfg)
    tabs = {q: _chunk_tables(cfg, q) for q in sorted({SSD_CHUNK, GDN_CHUNK, RET_CHUNK})}

    def init_state(st, l):
        zeros = jnp.zeros((n_ctx,) + st.shape[2:], F32)
        return jnp.concatenate([st[:, l], zeros], axis=0)

    u = _modulate_call(xs, xp, mod4, 0, cfg)
    x_res = (xs, xp)
    new_ssd, new_gdn, new_ret = [], [], []
    for l in range(depth):
        proj, small = _inproj_call(u, w_main[l], w_small[l], conv_w[l], conv_b[l][None, :], cfg)

        hy = {}
        for name, seq_len, n_seq, row0 in (("s", ls, n_dec, 0), ("p", lp, n_ctx, ts)):
            a_tab, b_tab = dft[seq_len]
            prs = _hy_filter_call(seq_len, a_tab, hy_w1[l], hy_b1[l], hy_w2[l], hy_b2[l], hy_w3[l], hy_freq[l])
            z1 = _hy_conv_call(proj, COL_HY_V, row0, proj, COL_HY_X1, row0, hy_bias[l], a_tab, b_tab, prs, 0, seq_len, n_seq)
            hy[name] = _hy_conv_call(z1, 0, 0, proj, COL_HY_X2, row0, hy_bias[l], a_tab, b_tab, prs, 1, seq_len, n_seq)

        ssd = _ssd_call(proj, small, tabs[SSD_CHUNK], init_state(state_ssd, l), ssd_dt_bias[l], ssd_A_log[l], ssd_D[l], cfg)
        gdn = _gdn_call(proj, small, tabs[GDN_CHUNK], init_state(state_gdn, l), gdn_A_log[l], gdn_dt_bias[l], cfg)
        ret = _ret_call(proj, tabs[RET_CHUNK], rope_tabs, rope_blocks, init_state(state_ret, l), ret_decay[l], cfg)
        new_ssd.append(ssd[2][n_dec:])
        new_gdn.append(gdn[2][n_dec:])
        new_ret.append(ret[2][n_dec:])

        x1, u2 = _outproj_call(hy["s"], hy["p"], ssd, gdn, ret, proj, ssd_norm_w[l], gdn_norm_w[l], w_out_b[l],
                               x_res, mod4, l, ln1_g[l], ln1_b[l], cfg)
        act = _ffn_up_call(u2, w_up_b[l], ffn_conv_w[l], ffn_conv_b[l][None, :], cfg)
        x_res, u = _ffn_down_call(act, w_down_b[l], x1, mod4, l, min(l + 1, depth - 1), ln2_g[l], ln2_b[l], cfg)

    y_sample = x_res[:ts].reshape(n_dec, ls, d)
    y_prompt = x_res[ts:].reshape(n_ctx, lp, d)
    return (y_prompt, y_sample, jnp.stack(new_ssd, 1), jnp.stack(new_gdn, 1), jnp.stack(new_ret, 1))
```

```python
import functools
import math

import numpy as np
import jax
import jax.numpy as jnp
from jax import lax
from jax.experimental import pallas as pl
from jax.experimental.pallas import tpu as pltpu

F32 = jnp.float32
BF16 = jnp.bfloat16
HIGHEST = lax.Precision.HIGHEST

N_DIR = 2
CONV_W = 3
GROUP_WIDTH = 512
HY_ORDER = 2
HY_BANDS = 8
HY_EMB = 1 + 2 * HY_BANDS
HY_FFN = 64
HY_DECAY_TARGET = 1e-2
HY_FAST_PCT = 0.3
HY_SLOW_PCT = 1.5
SSD_HEADS = 8
SSD_HEAD_DIM = 64
SSD_GROUPS = 2
SSD_HPG = SSD_HEADS // SSD_GROUPS
SSD_STATE = 128
SSD_CHUNK = 128
GDN_HEADS = 4
GDN_HEAD_DIM = 128
GDN_CHUNK = 64
RET_HEADS = 4
RET_HEAD_DIM = 128
RET_CHUNK = 128
GRID_W = 64
ROPE_BASE = 10000.0
LN_EPS = 1e-5
RMS_EPS = 1e-6
N_CONV = 8 * GROUP_WIDTH

LANES = 128
SUBLANES = 8
VMEM_LIMIT_BYTES = 60 * 1024 * 1024

COL_HY_V, COL_HY_X1, COL_HY_X2, COL_S_X, COL_S_BC, COL_G_Q, COL_G_K, COL_G_V = range(8)
COL_S_Z, COL_G_Z, COL_R_Q, COL_R_K, COL_R_V, COL_R_G = range(8, 14)
N_MAIN = 14 * GROUP_WIDTH
SM_SDT = 0
SM_GBETA = 16
SM_GA = 24


def _silu(x):
    return x * jax.nn.sigmoid(x)


def _softplus(x):
    return jnp.maximum(x, 0.0) + jnp.log1p(jnp.exp(-jnp.abs(x)))


def _dot(a, b):
    return jnp.dot(a.astype(BF16), b.astype(BF16), preferred_element_type=F32)


def _dot_nt(a, b):
    return lax.dot_general(a.astype(BF16), b.astype(BF16), (((1,), (1,)), ((), ())), preferred_element_type=F32)


def _dot_tn(a, b):
    return lax.dot_general(a.astype(BF16), b.astype(BF16), (((0,), (0,)), ((), ())), preferred_element_type=F32)


def _dot_hi(a, b):
    return jnp.dot(a, b, preferred_element_type=F32, precision=HIGHEST)


def _params(*semantics):
    return pltpu.CompilerParams(dimension_semantics=semantics, vmem_limit_bytes=VMEM_LIMIT_BYTES)


def _tri(n, lower):
    r = lax.broadcasted_iota(jnp.int32, (n, n), 0)
    c = lax.broadcasted_iota(jnp.int32, (n, n), 1)
    return jnp.where((r >= c) if lower else (r <= c), 1.0, 0.0).astype(F32)


def _mod_kernel(c_ref, w_ref, b_ref, o_ref):
    a = _silu(c_ref[...])
    o_ref[...] = _dot(a, w_ref[...]) + b_ref[...]


def _mod_call(cond, w_mod, b_mod):
    depth, d, n = w_mod.shape
    rows = cond.shape[0]
    tn = 1024
    return pl.pallas_call(
        _mod_kernel,
        out_shape=jax.ShapeDtypeStruct((depth, rows, n), F32),
        grid=(depth, n // tn),
        in_specs=[
            pl.BlockSpec((rows, d), lambda l, j: (0, 0)),
            pl.BlockSpec((None, d, tn), lambda l, j: (l, 0, j)),
            pl.BlockSpec((None, 1, tn), lambda l, j: (l, 0, j)),
        ],
        out_specs=pl.BlockSpec((None, rows, tn), lambda l, j: (l, 0, j)),
        compiler_params=_params("arbitrary", "arbitrary"),
        name="adaln_mod",
    )(cond, w_mod, b_mod.reshape(depth, 1, n))


def _mod_spec(layer, k, d, rows_per_tile, l_sample, n_dec):
    tiles_per_seq = l_sample // rows_per_tile

    def index_map(i, *_):
        return (layer, jnp.minimum(i // tiles_per_seq, n_dec), 0, k)

    return pl.BlockSpec((None, None, 1, d), index_map)


def _modulate_kernel(xs_ref, xp_ref, sh_ref, sc_ref, o_ref, *, n_sample_tiles):
    i = pl.program_id(0)
    x = jnp.where(i < n_sample_tiles, xs_ref[...], xp_ref[...])
    o_ref[...] = (x * (1.0 + sc_ref[...]) + sh_ref[...]).astype(BF16)


def _two_group_specs(tm, d, n_sample_tiles):
    return [
        pl.BlockSpec((tm, d), lambda i, *_: (jnp.minimum(i, n_sample_tiles - 1), 0)),
        pl.BlockSpec((tm, d), lambda i, *_: (jnp.maximum(i - n_sample_tiles, 0), 0)),
    ]


def _modulate_call(xs, xp, mod4, layer, cfg):
    tm, d = 256, cfg["d"]
    t = cfg["t"]
    ns = cfg["ts"] // tm
    return pl.pallas_call(
        functools.partial(_modulate_kernel, n_sample_tiles=ns),
        out_shape=jax.ShapeDtypeStruct((t, d), BF16),
        grid=(t // tm,),
        in_specs=_two_group_specs(tm, d, ns)
        + [_mod_spec(layer, 0, d, tm, cfg["ls"], cfg["n_dec"]), _mod_spec(layer, 1, d, tm, cfg["ls"], cfg["n_dec"])],
        out_specs=pl.BlockSpec((tm, d), lambda i: (i, 0)),
        compiler_params=_params("arbitrary"),
        name="modulate_in",
    )(xs, xp, mod4, mod4)


def _dwconv_rows(h, cw, cb, seq_len):
    rows = h.shape[0]
    pos = lax.broadcasted_iota(jnp.int32, (rows, 1), 0) & (seq_len - 1)
    prev = jnp.where(pos == 0, 0.0, pltpu.roll(h, 1, axis=0))
    nxt = jnp.where(pos == seq_len - 1, 0.0, pltpu.roll(h, rows - 1, axis=0))
    return prev * cw[0:1] + h * cw[1:2] + nxt * cw[2:3] + cb


def _inproj_kernel(u_ref, w_ref, ws_ref, cw_ref, cb_ref, o_ref, os_ref, *, n_conv_tiles, n_sample_tiles, ls, lp):
    i = pl.program_id(0)
    j = pl.program_id(1)
    seq_len = jnp.where(i < n_sample_tiles, ls, lp)
    u = u_ref[...]
    h = jnp.dot(u, w_ref[...], preferred_element_type=F32)

    @pl.when(j < n_conv_tiles)
    def _():
        o_ref[...] = _dwconv_rows(h, cw_ref[...], cb_ref[...], seq_len)

    @pl.when(j >= n_conv_tiles)
    def _():
        o_ref[...] = h

    @pl.when(j == 0)
    def _():
        os_ref[...] = jnp.dot(u, ws_ref[...], preferred_element_type=F32)


def _inproj_call(u, w_main, w_small, cw, cb, cfg):
    t, d = u.shape
    tm, tn = cfg["ls"], 512
    n_conv_tiles = N_CONV // tn
    kern = functools.partial(
        _inproj_kernel, n_conv_tiles=n_conv_tiles, n_sample_tiles=cfg["ts"] // tm, ls=cfg["ls"], lp=cfg["lp"]
    )
    conv_col = lambda i, j: (0, jnp.minimum(j, n_conv_tiles - 1))
    return pl.pallas_call(
        kern,
        out_shape=[jax.ShapeDtypeStruct((t, N_MAIN), F32), jax.ShapeDtypeStruct((t, LANES), F32)],
        grid=(t // tm, N_MAIN // tn),
        in_specs=[
            pl.BlockSpec((tm, d), lambda i, j: (i, 0)),
            pl.BlockSpec((d, tn), lambda i, j: (0, j)),
            pl.BlockSpec((d, LANES), lambda i, j: (0, 0)),
            pl.BlockSpec((CONV_W, tn), conv_col),
            pl.BlockSpec((1, tn), conv_col),
        ],
        out_specs=[pl.BlockSpec((tm, tn), lambda i, j: (i, j)), pl.BlockSpec((tm, LANES), lambda i, j: (i, 0))],
        compiler_params=_params("arbitrary", "arbitrary"),
        name="in_proj_conv",
    )(u, w_main, w_small, cw, cb)


def _dft_tables(l):
    n = 2 * l
    k = jnp.arange(l, dtype=jnp.int32)
    m = (k[:, None] * k[None, :]) % n
    ang = (2.0 * math.pi / n) * m.astype(F32)
    cos, sin = jnp.cos(ang), jnp.sin(ang)
    alt = jnp.where(k % 2 == 0, 1.0, -1.0).astype(F32)
    a_im = jnp.where(k[:, None] == 0, alt[None, :], -sin)
    a = jnp.concatenate([cos, a_im], axis=0)
    ck = jnp.where(k == 0, 1.0, 2.0).astype(F32) / n
    b_re = cos * ck[None, :]
    b_im = jnp.where(k[None, :] == 0, alt[:, None] / n, -sin * (2.0 / n))
    b = jnp.concatenate([b_re, b_im], axis=1)
    return a.astype(BF16), b.astype(BF16)


def _hy_feats(l):
    pos = jnp.arange(l, dtype=F32)
    t = pos / (l - 1)
    bands = jnp.linspace(1e-4, HY_BANDS - 1, HY_BANDS, dtype=F32)
    ang = (2.0 * math.pi / l) * pos[:, None] * bands
    feats = jnp.concatenate([t[:, None], jnp.cos(ang), -jnp.sin(ang)], -1)
    feats = jnp.pad(feats, ((0, 0), (0, LANES - HY_EMB)))
    return feats, t[:, None]


def _hy_deltas(c):
    d = jnp.abs(
        jnp.linspace(math.log(HY_DECAY_TARGET) / HY_SLOW_PCT, math.log(HY_DECAY_TARGET) / HY_FAST_PCT, c, dtype=F32)
    )
    return d[None, :]


def _hy_filter_kernel(
    feats_ref, t_ref, dl_ref, w1_ref, b1_ref, w2_ref, b2_ref, w3_ref, fq_ref, are_ref, aim_ref,
    p_ref, r_ref, s_ref, sum_scr, dif_scr, nyq_scr,
):
    f = pl.program_id(0)
    l = feats_ref.shape[0]
    c = dl_ref.shape[1]
    kb = are_ref.shape[0]

    @pl.when(f == 0)
    def _():
        fq = fq_ref[...]
        h = jnp.sin(fq[0:1] * (_dot_hi(feats_ref[...], w1_ref[...]) + b1_ref[...]))
        h = jnp.sin(fq[1:2] * (_dot_hi(h, w2_ref[...]) + b2_ref[...]))
        decay = jnp.exp(-t_ref[...] * dl_ref[...])
        row = lax.broadcasted_iota(jnp.int32, (l, 1), 0)
        sign = jnp.where((row & 1) == 0, 1.0, -1.0)
        for o in range(HY_ORDER):
            fwd = _dot_hi(h, w3_ref[:, (2 * o) * c:(2 * o + 1) * c]) * decay
            bwd = _dot_hi(h, w3_ref[:, (2 * o + 1) * c:(2 * o + 2) * c]) * decay
            bwd = jnp.where(row == 0, 0.0, bwd)
            norm = jnp.sum(jnp.abs(fwd), axis=0, keepdims=True) + jnp.sum(jnp.abs(bwd), axis=0, keepdims=True)
            fwd = fwd / norm
            bwd = bwd / norm
            ssum = fwd + bwd
            sum_scr[o] = ssum.astype(BF16)
            dif_scr[o] = (fwd - bwd).astype(BF16)
            nyq_scr[o] = jnp.broadcast_to(jnp.sum(ssum * sign, axis=0, keepdims=True), (SUBLANES, c))

    grow = f * kb + lax.broadcasted_iota(jnp.int32, (kb, 1), 0)
    for o in range(HY_ORDER):
        kre = jnp.dot(are_ref[...], sum_scr[o], preferred_element_type=F32)
        kim = jnp.dot(aim_ref[...], dif_scr[o], preferred_element_type=F32)
        p_ref[o] = kre
        r_ref[o] = jnp.where(grow == 0, 0.0, kim)
        s_ref[o] = jnp.where(grow == 0, nyq_scr[o][0:1], kre)


def _hy_filter_call(l, a_tab, w1, b1, w2, b2, w3, freq):
    c = GROUP_WIDTH
    kb = min(l, 256)
    nf = l // kb
    feats, t = _hy_feats(l)
    w1p = jnp.pad(w1, ((0, LANES - HY_EMB), (0, 0)))
    full = lambda shape: pl.BlockSpec(shape, lambda f: tuple(0 for _ in shape))
    out = jax.ShapeDtypeStruct((HY_ORDER, l, c), F32)
    out_spec = pl.BlockSpec((HY_ORDER, kb, c), lambda f: (0, f, 0))
    return pl.pallas_call(
        _hy_filter_kernel,
        out_shape=[out, out, out],
        grid=(nf,),
        in_specs=[
            full((l, LANES)), full((l, 1)), full((1, c)), full((LANES, HY_FFN)), full((1, HY_FFN)),
            full((HY_FFN, HY_FFN)), full((1, HY_FFN)), full((HY_FFN, 2 * HY_ORDER * c)), full((2, HY_FFN)),
            pl.BlockSpec((kb, l), lambda f: (f, 0)),
            pl.BlockSpec((kb, l), lambda f: (f + nf, 0)),
        ],
        out_specs=[out_spec, out_spec, out_spec],
        scratch_shapes=[
            pltpu.VMEM((HY_ORDER, l, c), BF16), pltpu.VMEM((HY_ORDER, l, c), BF16),
            pltpu.VMEM((HY_ORDER, SUBLANES, c), F32),
        ],
        compiler_params=_params("arbitrary"),
        name="hyena_filter",
    )(feats, t, _hy_deltas(c), w1p, b1[None, :], w2, b2[None, :], w3, freq, a_tab, a_tab)


def _hy_conv_kernel(z_ref, g_ref, bias_ref, are_ref, aim_ref, bre_ref, bim_ref, p_ref, r_ref, s_ref, o_ref, zb_scr):
    f = pl.program_id(2)

    @pl.when(f == 0)
    def _():
        zb_scr[...] = z_ref[...].astype(BF16)
        o_ref[...] = jnp.zeros_like(o_ref)

    zb = zb_scr[...]
    re = jnp.dot(are_ref[...], zb, preferred_element_type=F32)
    im = jnp.dot(aim_ref[...], zb, preferred_element_type=F32)
    p, r, s = p_ref[...], r_ref[...], s_ref[...]
    re2 = re * p - im * r
    im2 = re * r + im * s
    o_ref[...] += _dot(bre_ref[...], re2) + _dot(bim_ref[...], im2)

    @pl.when(f == pl.num_programs(2) - 1)
    def _():
        o_ref[...] = g_ref[...] * (o_ref[...] + z_ref[...] * bias_ref[...])


def _hy_conv_call(z, z_col, z_row0, g, g_col, g_row0, bias, a_tab, b_tab, prs, order, l, n_seq):
    c = GROUP_WIDTH
    ct = 256
    ncb = c // ct
    kb = min(l, 256)
    nf = l // kb
    p, r, s = prs
    zspec = pl.BlockSpec((l, ct), lambda b, cb, f: (z_row0 // l + b, z_col * ncb + cb))
    gspec = pl.BlockSpec((l, ct), lambda b, cb, f: (g_row0 // l + b, g_col * ncb + cb))
    kspec = pl.BlockSpec((None, kb, ct), lambda b, cb, f: (order, f, cb))
    return pl.pallas_call(
        _hy_conv_kernel,
        out_shape=jax.ShapeDtypeStruct((n_seq * l, c), F32),
        grid=(n_seq, ncb, nf),
        in_specs=[
            zspec, gspec,
            pl.BlockSpec((None, 1, ct), lambda b, cb, f: (order, 0, cb)),
            pl.BlockSpec((kb, l), lambda b, cb, f: (f, 0)),
            pl.BlockSpec((kb, l), lambda b, cb, f: (f + nf, 0)),
            pl.BlockSpec((l, kb), lambda b, cb, f: (0, f)),
            pl.BlockSpec((l, kb), lambda b, cb, f: (0, f + nf)),
            kspec, kspec, kspec,
        ],
        out_specs=pl.BlockSpec((l, ct), lambda b, cb, f: (b, cb)),
        scratch_shapes=[pltpu.VMEM((l, ct), BF16)],
        compiler_params=_params("arbitrary", "arbitrary", "arbitrary"),
        name="hyena_conv",
    )(z, g, bias.reshape(HY_ORDER, 1, c), a_tab, a_tab, b_tab, b_tab, p, r, s)


def _chunk_tables(cfg, q):
    fwd, bwd, first, last, seq = [], [], [], [], []
    sid = 0
    for n_seq, l, row0 in ((cfg["n_dec"], cfg["ls"], 0), (cfg["n_ctx"], cfg["lp"], cfg["ts"])):
        nc = l // q
        for b in range(n_seq):
            base = (row0 + b * l) // q
            for c in range(nc):
                fwd.append(base + c)
                bwd.append(base + nc - 1 - c)
                first.append(int(c == 0))
                last.append(int(c == nc - 1))
                seq.append(sid)
            sid += 1
    return tuple(jnp.asarray(np.asarray(a, np.int32)) for a in (fwd, bwd, first, last, seq))


def _scan_masks(q, d):
    r = lax.broadcasted_iota(jnp.int32, (q, q), 0)
    c = lax.broadcasted_iota(jnp.int32, (q, q), 1)
    return ((r >= c), (r > c)) if d == 0 else ((r <= c), (r < c))


def _cumsum_pair(a_col, a_row, q, d):
    lo, up = _tri(q, True), _tri(q, False)
    if d == 0:
        return _dot_hi(lo, a_col), _dot_hi(a_row, up)
    return _dot_hi(up, a_col), _dot_hi(a_row, lo)


def _ssd_kernel(
    fwd_t, bwd_t, first_t, last_t, seq_t,
    xf_ref, bcf_ref, smf_ref, xb_ref, bcb_ref, smb_ref, pr_ref, pc_ref, dsk_ref, h0_ref,
    yf_ref, yb_ref, hout_ref, h_scr,
):
    s = pl.program_id(0)
    q, p, n = SSD_CHUNK, SSD_HEAD_DIM, SSD_STATE

    @pl.when(first_t[s] == 1)
    def _():
        h_scr[...] = h0_ref[...]

    bias_r, alog_r = pr_ref[0:1], pr_ref[1:2]
    bias_c, alog_c = pc_ref[:, 0:1], pc_ref[:, 1:2]
    probs = []
    for d in range(N_DIR):
        x_ref, bc_ref, sm_ref = (xf_ref, bcf_ref, smf_ref) if d == 0 else (xb_ref, bcb_ref, smb_ref)
        x = _silu(x_ref[...])
        bc = _silu(bc_ref[...])
        sm = sm_ref[...]
        dt_c = _softplus(sm + bias_r)
        a_c = dt_c * (-jnp.exp(alog_r))
        a_r = _softplus(sm.T + bias_c) * (-jnp.exp(alog_c))
        acs_c, acs_r = _cumsum_pair(a_c, a_r[SM_SDT:SM_SDT + N_DIR * SSD_HEADS], q, d)
        incl, _ = _scan_masks(q, d)
        end = q - 1 if d == 0 else 0
        for g in range(SSD_GROUPS):
            bm = bc[:, g * n:(g + 1) * n]
            cm = bc[:, (SSD_GROUPS + g) * n:(SSD_GROUPS + g + 1) * n]
            cb = _dot_nt(cm, bm)
            for e in range(SSD_HPG):
                hd = g * SSD_HPG + e
                idx = d * SSD_HEADS + hd
                col = acs_c[:, idx:idx + 1]
                xh = x[:, hd * p:(hd + 1) * p]
                probs.append(dict(
                    d=d, hd=hd, bm=bm, cm=cm, cb=cb, col=col, row=acs_r[idx:idx + 1, :], incl=incl, xh=xh,
                    xs=xh * dt_c[:, idx:idx + 1], tot=col[end:end + 1], skip=dsk_ref[d:d + 1, hd * p:(hd + 1) * p],
                ))
    hst = [h_scr[pb["d"], pb["hd"]] for pb in probs]
    lmat = [jnp.exp(jnp.where(pb["incl"], pb["col"] - pb["row"], -1e30)) for pb in probs]
    y_diag = [_dot(pb["cb"] * lm, pb["xs"]) for pb, lm in zip(probs, lmat)]
    y_off = [_dot_nt(pb["cm"], h_i) for pb, h_i in zip(probs, hst)]
    states = [_dot_tn(pb["xs"] * jnp.exp(pb["tot"] - pb["col"]), pb["bm"]) for pb in probs]
    ys = [yd + yo * jnp.exp(pb["col"]) + pb["skip"] * pb["xh"] for pb, yd, yo in zip(probs, y_diag, y_off)]
    for pb, h_i, st_i in zip(probs, hst, states):
        h_scr[pb["d"], pb["hd"]] = h_i * jnp.exp(pb["tot"]) + st_i
    yf_ref[...] = jnp.concatenate(ys[:SSD_HEADS], axis=-1)
    yb_ref[...] = jnp.concatenate(ys[SSD_HEADS:], axis=-1)

    @pl.when(last_t[s] == 1)
    def _():
        hout_ref[...] = h_scr[...]


def _ssd_call(proj, small, tables, h0, dt_bias, a_log, d_skip, cfg):
    q, w = SSD_CHUNK, GROUP_WIDTH
    t = proj.shape[0]
    n_steps = tables[0].shape[0]
    nlane = N_DIR * SSD_HEADS
    pr = jnp.zeros((SUBLANES, LANES), F32)
    pr = pr.at[0, SM_SDT:SM_SDT + nlane].set(dt_bias.reshape(-1)).at[1, SM_SDT:SM_SDT + nlane].set(a_log.reshape(-1))
    pc = pr.T
    dsk = jnp.repeat(d_skip, SSD_HEAD_DIM, axis=-1)
    st_shape = (N_DIR, SSD_HEADS, SSD_HEAD_DIM, SSD_STATE)
    fmap = lambda col: (lambda s, fw, bw, fi, la, sq: (fw[s], col))
    bmap = lambda col: (lambda s, fw, bw, fi, la, sq: (bw[s], col))
    const = lambda shape: pl.BlockSpec(shape, lambda s, *_: tuple(0 for _ in shape))
    st_spec = pl.BlockSpec((None,) + st_shape, lambda s, fw, bw, fi, la, sq: (sq[s], 0, 0, 0, 0))
    grid_spec = pltpu.PrefetchScalarGridSpec(
        num_scalar_prefetch=5,
        grid=(n_steps,),
        in_specs=[
            pl.BlockSpec((q, w), fmap(COL_S_X)), pl.BlockSpec((q, w), fmap(COL_S_BC)), pl.BlockSpec((q, LANES), fmap(0)),
            pl.BlockSpec((q, w), bmap(COL_S_X)), pl.BlockSpec((q, w), bmap(COL_S_BC)), pl.BlockSpec((q, LANES), bmap(0)),
            const((SUBLANES, LANES)), const((LANES, SUBLANES)), const((N_DIR, w)), st_spec,
        ],
        out_specs=[pl.BlockSpec((q, w), fmap(0)), pl.BlockSpec((q, w), bmap(0)), st_spec],
        scratch_shapes=[pltpu.VMEM(st_shape, F32)],
    )
    n_seq = cfg["n_dec"] + cfg["n_ctx"]
    return pl.pallas_call(
        _ssd_kernel,
        out_shape=[
            jax.ShapeDtypeStruct((t, w), F32), jax.ShapeDtypeStruct((t, w), F32),
            jax.ShapeDtypeStruct((n_seq,) + st_shape, F32),
        ],
        grid_spec=grid_spec,
        compiler_params=_params("arbitrary"),
        name="ssd_scan",
    )(*tables, proj, proj, small, proj, proj, small, pr, pc, dsk, h0)


TRI_BLOCK = 2 * SUBLANES


def _unit_tri_inverses(mats, lowers):
    n = mats[0].shape[0]
    blk = TRI_BLOCK
    sub = blk // SUBLANES
    lane = lax.broadcasted_iota(jnp.int32, (blk, n), 1)
    row = lax.broadcasted_iota(jnp.int32, (blk, n), 0)
    group = (lax.broadcasted_iota(jnp.int32, (SUBLANES, LANES), 1) // blk) * blk
    tiles = lambda v: [v[t * SUBLANES:(t + 1) * SUBLANES, :] for t in range(sub)]
    eye = jnp.where(lane % blk == row, 1.0, 0.0)
    d_t, inv_t = [], []
    for a in mats:
        packed = jnp.zeros((blk, n), F32)
        for bi in range(n // blk):
            packed = jnp.where(lane // blk == bi, a[bi * blk:(bi + 1) * blk, :], packed)
        if n < LANES:
            packed = jnp.concatenate([packed, jnp.zeros((blk, LANES - n), F32)], axis=1)
        d_t.append(tiles(packed))
        inv_t.append(tiles(eye))
    for step in range(blk):
        for i, lower in enumerate(lowers):
            jj = step if lower else blk - 1 - step
            st, jr = divmod(jj, SUBLANES)
            done_row = inv_t[i][st][jr:jr + 1, :]
            for s in (range(st, sub) if lower else range(st + 1)):
                col = jnp.take_along_axis(d_t[i][s], group + jj, axis=1)[:, :n]
                inv_t[i][s] = inv_t[i][s] - col * done_row
    ts = []
    for i in range(len(mats)):
        inv = jnp.concatenate(inv_t[i], axis=0)
        ts.append(jnp.concatenate([jnp.where(lane // blk == bi, inv, 0.0) for bi in range(n // blk)], axis=0))
    r = lax.broadcasted_iota(jnp.int32, (n, n), 0)
    c = lax.broadcasted_iota(jnp.int32, (n, n), 1)
    size = blk
    while size < n:
        level = (r // (2 * size) == c // (2 * size)) & (r // size != c // size)
        half = [_dot(t, jnp.where(level, a, 0.0)) for t, a in zip(ts, mats)]
        ts = [t - _dot(h, t) for t, h in zip(ts, half)]
        size *= 2
    return ts


def _l2norm_heads(x, heads, dim, scale):
    outs = []
    for h in range(heads):
        xh = x[:, h * dim:(h + 1) * dim]
        outs.append(xh * (lax.rsqrt(jnp.sum(xh * xh, axis=-1, keepdims=True) + 1e-6) * scale))
    return outs


def _gdn_kernel(
    fwd_t, bwd_t, first_t, last_t, seq_t,
    qf_ref, kf_ref, vf_ref, smf_ref, qb_ref, kb_ref, vb_ref, smb_ref, pr_ref, pc_ref, s0_ref,
    of_ref, ob_ref, sout_ref, s_scr,
):
    s = pl.program_id(0)
    q, dim = GDN_CHUNK, GDN_HEAD_DIM

    @pl.when(first_t[s] == 1)
    def _():
        s_scr[...] = s0_ref[...]

    bias_r, alog_r = pr_ref[0:1], pr_ref[1:2]
    bias_c, alog_c = pc_ref[:, 0:1], pc_ref[:, 1:2]
    probs = []
    for d in range(N_DIR):
        q_ref, k_ref, v_ref, sm_ref = (qf_ref, kf_ref, vf_ref, smf_ref) if d == 0 else (qb_ref, kb_ref, vb_ref, smb_ref)
        qs = _l2norm_heads(_silu(q_ref[...]), GDN_HEADS, dim, dim ** -0.5)
        ks = _l2norm_heads(_silu(k_ref[...]), GDN_HEADS, dim, 1.0)
        v = _silu(v_ref[...])
        sm = sm_ref[...]
        beta_c = jax.nn.sigmoid(sm)
        g_c = -jnp.exp(alog_r) * _softplus(sm + bias_r)
        g_r = -jnp.exp(alog_c) * _softplus(sm.T + bias_c)
        gc_c, gc_r = _cumsum_pair(g_c, g_r[SM_GA:SM_GA + N_DIR * GDN_HEADS], q, d)
        incl, strict = _scan_masks(q, d)
        end = q - 1 if d == 0 else 0
        for h in range(GDN_HEADS):
            idx = d * GDN_HEADS + h
            col = gc_c[:, SM_GA + idx:SM_GA + idx + 1]
            probs.append(dict(
                d=d, h=h, q=qs[h], k=ks[h], v=v[:, h * dim:(h + 1) * dim], col=col, row=gc_r[idx:idx + 1, :],
                beta=beta_c[:, SM_GBETA + idx:SM_GBETA + idx + 1], tot=col[end:end + 1], incl=incl, strict=strict,
            ))
    kk = [_dot_nt(p["k"], p["k"]) for p in probs]
    qk = [_dot_nt(p["q"], p["k"]) for p in probs]
    dmask = [jnp.exp(jnp.where(p["incl"], p["col"] - p["row"], -1e30)) for p in probs]
    a = [jnp.where(p["strict"], p["beta"] * kk_i * dm, 0.0) for p, kk_i, dm in zip(probs, kk, dmask)]
    ecol = [jnp.exp(p["col"]) for p in probs]
    rhs = [jnp.concatenate([p["v"] * p["beta"], p["k"] * (p["beta"] * e)], axis=-1) for p, e in zip(probs, ecol)]
    inv = _unit_tri_inverses(a, [p["d"] == 0 for p in probs])
    sol = [_dot(t, r) for t, r in zip(inv, rhs)]
    st = [s_scr[p["d"], p["h"]] for p in probs]
    w_s = [_dot(x[:, dim:], s_i) for x, s_i in zip(sol, st)]
    q_s = [_dot(p["q"] * e, s_i) for p, e, s_i in zip(probs, ecol, st)]
    vv = [x[:, :dim] - ws for x, ws in zip(sol, w_s)]
    outs = [qs_i + _dot(qk_i * dm, vv_i) for qs_i, qk_i, dm, vv_i in zip(q_s, qk, dmask, vv)]
    upd = [_dot_tn(p["k"] * jnp.exp(p["tot"] - p["col"]), vv_i) for p, vv_i in zip(probs, vv)]
    for p, s_i, u_i in zip(probs, st, upd):
        s_scr[p["d"], p["h"]] = s_i * jnp.exp(p["tot"]) + u_i
    of_ref[...] = jnp.concatenate(outs[:GDN_HEADS], axis=-1)
    ob_ref[...] = jnp.concatenate(outs[GDN_HEADS:], axis=-1)

    @pl.when(last_t[s] == 1)
    def _():
        sout_ref[...] = s_scr[...]


def _gdn_call(proj, small, tables, s0, a_log, dt_bias, cfg):
    q, w = GDN_CHUNK, GROUP_WIDTH
    t = proj.shape[0]
    n_steps = tables[0].shape[0]
    nlane = N_DIR * GDN_HEADS
    pr = jnp.zeros((SUBLANES, LANES), F32)
    pr = pr.at[0, SM_GA:SM_GA + nlane].set(dt_bias.reshape(-1)).at[1, SM_GA:SM_GA + nlane].set(a_log.reshape(-1))
    pc = pr.T
    st_shape = (N_DIR, GDN_HEADS, GDN_HEAD_DIM, GDN_HEAD_DIM)
    fmap = lambda col: (lambda s, fw, bw, fi, la, sq: (fw[s], col))
    bmap = lambda col: (lambda s, fw, bw, fi, la, sq: (bw[s], col))
    const = lambda shape: pl.BlockSpec(shape, lambda s, *_: tuple(0 for _ in shape))
    st_spec = pl.BlockSpec((None,) + st_shape, lambda s, fw, bw, fi, la, sq: (sq[s], 0, 0, 0, 0))
    blk = lambda m, col: pl.BlockSpec((q, w), m(col))
    grid_spec = pltpu.PrefetchScalarGridSpec(
        num_scalar_prefetch=5,
        grid=(n_steps,),
        in_specs=[
            blk(fmap, COL_G_Q), blk(fmap, COL_G_K), blk(fmap, COL_G_V), pl.BlockSpec((q, LANES), fmap(0)),
            blk(bmap, COL_G_Q), blk(bmap, COL_G_K), blk(bmap, COL_G_V), pl.BlockSpec((q, LANES), bmap(0)),
            const((SUBLANES, LANES)), const((LANES, SUBLANES)), st_spec,
        ],
        out_specs=[pl.BlockSpec((q, w), fmap(0)), pl.BlockSpec((q, w), bmap(0)), st_spec],
        scratch_shapes=[pltpu.VMEM(st_shape, F32)],
    )
    n_seq = cfg["n_dec"] + cfg["n_ctx"]
    return pl.pallas_call(
        _gdn_kernel,
        out_shape=[
            jax.ShapeDtypeStruct((t, w), F32), jax.ShapeDtypeStruct((t, w), F32),
            jax.ShapeDtypeStruct((n_seq,) + st_shape, F32),
        ],
        grid_spec=grid_spec,
        compiler_params=_params("arbitrary"),
        name="gdn_scan",
    )(*tables, proj, proj, proj, small, proj, proj, proj, small, pr, pc, s0)


def _rope_tables(cfg):
    l = cfg["ls"]
    pos = jnp.arange(l)
    r = (pos // GRID_W).astype(F32)
    col = (pos % GRID_W).astype(F32)
    nf = RET_HEAD_DIM // 4
    inv = jnp.power(ROPE_BASE, -jnp.arange(nf, dtype=F32) / nf)
    ang = jnp.concatenate([r[:, None] * inv, col[:, None] * inv], -1)
    cos, sin = jnp.cos(ang), jnp.sin(ang)
    cos2 = jnp.concatenate([cos, cos], -1)
    sin2 = jnp.concatenate([-sin, sin], -1)
    ident = jnp.ones((RET_CHUNK, RET_HEAD_DIM), F32)
    return jnp.concatenate([cos2, ident], 0), jnp.concatenate([sin2, 0.0 * ident], 0)


def _rope_block_table(cfg):
    q = RET_CHUNK
    nc_s, nc_p = cfg["ls"] // q, cfg["lp"] // q
    fwd = [c for _ in range(cfg["n_dec"]) for c in range(nc_s)] + [nc_s] * (cfg["n_ctx"] * nc_p)
    bwd = [nc_s - 1 - c for _ in range(cfg["n_dec"]) for c in range(nc_s)] + [nc_s] * (cfg["n_ctx"] * nc_p)
    return jnp.asarray(np.asarray(fwd, np.int32)), jnp.asarray(np.asarray(bwd, np.int32))


def _ret_kernel(
    fwd_t, bwd_t, first_t, last_t, seq_t, rf_t, rb_t,
    qf_ref, kf_ref, vf_ref, cf_ref, sf_ref, qb_ref, kb_ref, vb_ref, cb_ref, sb_ref, dec_ref, s0_ref,
    of_ref, ob_ref, sout_ref, s_scr,
):
    s = pl.program_id(0)
    q, dim = RET_CHUNK, RET_HEAD_DIM

    @pl.when(first_t[s] == 1)
    def _():
        s_scr[...] = s0_ref[...]

    lg_all = -jnp.exp(dec_ref[...])
    ri = lax.broadcasted_iota(jnp.int32, (q, q), 0)
    ci = lax.broadcasted_iota(jnp.int32, (q, q), 1)
    rpos = lax.broadcasted_iota(jnp.int32, (q, 1), 0)
    probs = []
    for d in range(N_DIR):
        q_ref, k_ref, v_ref, c_ref, sn_ref = (
            (qf_ref, kf_ref, vf_ref, cf_ref, sf_ref) if d == 0 else (qb_ref, kb_ref, vb_ref, cb_ref, sb_ref)
        )
        qa, ka, va = q_ref[...], k_ref[...], v_ref[...]
        cos, sin = c_ref[...], sn_ref[...]
        rel = ((ri - ci) if d == 0 else (ci - ri)).astype(F32)
        pos = (rpos if d == 0 else (q - 1 - rpos)).astype(F32)
        for h in range(RET_HEADS):
            idx = d * RET_HEADS + h
            lg = lg_all[idx:idx + 1, :]
            qh = qa[:, h * dim:(h + 1) * dim]
            kh = ka[:, h * dim:(h + 1) * dim] * dim ** -0.5
            probs.append(dict(
                d=d, h=h, lg=lg, lg1=lg[:, 0:1], rel=rel, pos=pos, v=va[:, h * dim:(h + 1) * dim],
                q=qh * cos + pltpu.roll(qh, dim // 2, axis=1) * sin,
                k=kh * cos + pltpu.roll(kh, dim // 2, axis=1) * sin,
            ))
    st = [s_scr[pb["d"], pb["h"]] for pb in probs]
    qk = [_dot_nt(pb["q"], pb["k"]) for pb in probs]
    dmask = [jnp.exp(jnp.where(pb["rel"] >= 0, pb["rel"] * pb["lg"], -1e30)) for pb in probs]
    o_state = [_dot(pb["q"] * jnp.exp((pb["pos"] + 1.0) * pb["lg1"]), s_i) for pb, s_i in zip(probs, st)]
    o_attn = [_dot(qk_i * dm, pb["v"]) for pb, qk_i, dm in zip(probs, qk, dmask)]
    upd = [_dot_tn(pb["k"] * jnp.exp((q - 1.0 - pb["pos"]) * pb["lg1"]), pb["v"]) for pb in probs]
    for pb, s_i, u_i in zip(probs, st, upd):
        s_scr[pb["d"], pb["h"]] = s_i * jnp.exp(q * pb["lg"]) + u_i
    outs = [a + b for a, b in zip(o_state, o_attn)]
    of_ref[...] = jnp.concatenate(outs[:RET_HEADS], axis=-1)
    ob_ref[...] = jnp.concatenate(outs[RET_HEADS:], axis=-1)

    @pl.when(last_t[s] == 1)
    def _():
        sout_ref[...] = s_scr[...]


def _ret_call(proj, tables, rope_tabs, rope_blocks, s0, ret_decay, cfg):
    q, w, dim = RET_CHUNK, GROUP_WIDTH, RET_HEAD_DIM
    t = proj.shape[0]
    n_steps = tables[0].shape[0]
    dec = jnp.broadcast_to(ret_decay.reshape(-1, 1), (N_DIR * RET_HEADS, LANES))
    st_shape = (N_DIR, RET_HEADS, dim, dim)
    fmap = lambda col: (lambda s, fw, bw, fi, la, sq, rf, rb: (fw[s], col))
    bmap = lambda col: (lambda s, fw, bw, fi, la, sq, rf, rb: (bw[s], col))
    rfmap = lambda s, fw, bw, fi, la, sq, rf, rb: (rf[s], 0)
    rbmap = lambda s, fw, bw, fi, la, sq, rf, rb: (rb[s], 0)
    st_spec = pl.BlockSpec((None,) + st_shape, lambda s, fw, bw, fi, la, sq, rf, rb: (sq[s], 0, 0, 0, 0))
    blk = lambda m, col: pl.BlockSpec((q, w), m(col))
    rope = lambda m: pl.BlockSpec((q, dim), m)
    grid_spec = pltpu.PrefetchScalarGridSpec(
        num_scalar_prefetch=7,
        grid=(n_steps,),
        in_specs=[
            blk(fmap, COL_R_Q), blk(fmap, COL_R_K), blk(fmap, COL_R_V), rope(rfmap), rope(rfmap),
            blk(bmap, COL_R_Q), blk(bmap, COL_R_K), blk(bmap, COL_R_V), rope(rbmap), rope(rbmap),
            pl.BlockSpec((N_DIR * RET_HEADS, LANES), lambda s, *_: (0, 0)), st_spec,
        ],
        out_specs=[pl.BlockSpec((q, w), fmap(0)), pl.BlockSpec((q, w), bmap(0)), st_spec],
        scratch_shapes=[pltpu.VMEM(st_shape, F32)],
    )
    n_seq = cfg["n_dec"] + cfg["n_ctx"]
    cos2, sin2 = rope_tabs
    return pl.pallas_call(
        _ret_kernel,
        out_shape=[
            jax.ShapeDtypeStruct((t, w), F32), jax.ShapeDtypeStruct((t, w), F32),
            jax.ShapeDtypeStruct((n_seq,) + st_shape, F32),
        ],
        grid_spec=grid_spec,
        compiler_params=_params("arbitrary"),
        name="retention_scan",
    )(*tables, *rope_blocks, proj, proj, proj, cos2, sin2, proj, proj, proj, cos2, sin2, dec, s0)


def _layer_norm_rows(y, g, b):
    mu = jnp.mean(y, axis=-1, keepdims=True)
    yc = y - mu
    var = jnp.mean(yc * yc, axis=-1, keepdims=True)
    return yc * lax.rsqrt(var + LN_EPS) * g + b


def _outproj_kernel(
    hys_ref, hyp_ref, sf_ref, sb_ref, sz_ref, gf_ref, gb_ref, gz_ref, rf_ref, rb_ref, rg_ref,
    snw_ref, gnw_ref, w_ref, xs_ref, xp_ref, g1_ref, sh2_ref, sc2_ref, lng_ref, lnb_ref,
    x1_ref, u2_ref, *, n_sample_tiles, alpha, two_group_x,
):
    i = pl.program_id(0)
    is_sample = i < n_sample_tiles
    w = GROUP_WIDTH
    y_hy = jnp.where(is_sample, hys_ref[...], hyp_ref[...])
    y_ssd = (sf_ref[...] + sb_ref[...]) * _silu(sz_ref[...])
    y_ssd = y_ssd * lax.rsqrt(jnp.mean(y_ssd * y_ssd, axis=-1, keepdims=True) + RMS_EPS) * snw_ref[...]
    og = gf_ref[...] + gb_ref[...]
    orr = rf_ref[...] + rb_ref[...]
    gz = _silu(gz_ref[...])
    rg = _silu(rg_ref[...])
    gnw = gnw_ref[...]
    acc = _dot(y_hy, w_ref[0:w, :]) + _dot(y_ssd, w_ref[w:2 * w, :])
    for h in range(GDN_HEADS):
        sl = slice(h * GDN_HEAD_DIM, (h + 1) * GDN_HEAD_DIM)
        o = og[:, sl]
        o = o * lax.rsqrt(jnp.mean(o * o, axis=-1, keepdims=True) + RMS_EPS) * gnw * gz[:, sl]
        acc += _dot(o, w_ref[2 * w + h * GDN_HEAD_DIM:2 * w + (h + 1) * GDN_HEAD_DIM, :])
    for h in range(RET_HEADS):
        sl = slice(h * RET_HEAD_DIM, (h + 1) * RET_HEAD_DIM)
        o = orr[:, sl]
        mu = jnp.mean(o, axis=-1, keepdims=True)
        oc = o - mu
        o = oc * lax.rsqrt(jnp.mean(oc * oc, axis=-1, keepdims=True) + LN_EPS) * rg[:, sl]
        acc += _dot(o, w_ref[3 * w + h * RET_HEAD_DIM:3 * w + (h + 1) * RET_HEAD_DIM, :])
    x = jnp.where(is_sample, xs_ref[...], xp_ref[...]) if two_group_x else xs_ref[...]
    x1 = _layer_norm_rows(alpha * x + g1_ref[...] * acc, lng_ref[...], lnb_ref[...])
    x1_ref[...] = x1
    u2_ref[...] = (x1 * (1.0 + sc2_ref[...]) + sh2_ref[...]).astype(BF16)


def _outproj_call(hy_s, hy_p, ssd, gdn, ret, proj, ssd_nw, gdn_nw, w_out, x_in, mod4, layer, ln_g, ln_b, cfg):
    t, d, w = cfg["t"], cfg["d"], GROUP_WIDTH
    tm = 256
    ns = cfg["ts"] // tm
    two_group_x = isinstance(x_in, tuple)
    row = lambda col: pl.BlockSpec((tm, w), lambda i: (i, col))
    vec = lambda n: pl.BlockSpec((1, n), lambda i: (0, 0))
    mspec = lambda k: _mod_spec(layer, k, d, tm, cfg["ls"], cfg["n_dec"])
    if two_group_x:
        x_specs = _two_group_specs(tm, d, ns)
        x_args = list(x_in)
    else:
        x_specs = [pl.BlockSpec((tm, d), lambda i: (i, 0))] * 2
        x_args = [x_in, x_in]
    kern = functools.partial(_outproj_kernel, n_sample_tiles=ns, alpha=cfg["alpha"], two_group_x=two_group_x)
    return pl.pallas_call(
        kern,
        out_shape=[jax.ShapeDtypeStruct((t, d), F32), jax.ShapeDtypeStruct((t, d), BF16)],
        grid=(t // tm,),
        in_specs=_two_group_specs(tm, w, ns)
        + [row(0), row(0), row(COL_S_Z), row(0), row(0), row(COL_G_Z), row(0), row(0), row(COL_R_G)]
        + [vec(w), vec(GDN_HEAD_DIM), pl.BlockSpec((d, d), lambda i: (0, 0))]
        + x_specs
        + [mspec(2), mspec(3), mspec(4), vec(d), vec(d)],
        out_specs=[pl.BlockSpec((tm, d), lambda i: (i, 0)), pl.BlockSpec((tm, d), lambda i: (i, 0))],
        compiler_params=_params("arbitrary"),
        name="out_proj_ln",
    )(
        hy_s, hy_p, ssd[0], ssd[1], proj, gdn[0], gdn[1], proj, ret[0], ret[1], proj,
        ssd_nw[None, :], gdn_nw[None, :], w_out, *x_args, mod4, mod4, mod4, ln_g[None, :], ln_b[None, :],
    )


def _ffn_up_kernel(u_ref, wg_ref, wv_ref, cwg_ref, cbg_ref, cwv_ref, cbv_ref, o_ref, *, n_sample_tiles, ls, lp):
    i = pl.program_id(0)
    seq_len = jnp.where(i < n_sample_tiles, ls, lp)
    u = u_ref[...]
    gate = _dwconv_rows(jnp.dot(u, wg_ref[...], preferred_element_type=F32), cwg_ref[...], cbg_ref[...], seq_len)
    val = _dwconv_rows(jnp.dot(u, wv_ref[...], preferred_element_type=F32), cwv_ref[...], cbv_ref[...], seq_len)
    o_ref[...] = (_silu(gate) * val).astype(BF16)


def _ffn_up_call(u2, w_up, cw, cb, cfg):
    t, d = u2.shape
    dff = w_up.shape[1] // 2
    tm, tn = cfg["ls"], 256
    nj = dff // tn
    kern = functools.partial(_ffn_up_kernel, n_sample_tiles=cfg["ts"] // tm, ls=cfg["ls"], lp=cfg["lp"])
    gcol = lambda i, j: (0, j)
    vcol = lambda i, j: (0, j + nj)
    return pl.pallas_call(
        kern,
        out_shape=jax.ShapeDtypeStruct((t, dff), BF16),
        grid=(t // tm, nj),
        in_specs=[
            pl.BlockSpec((tm, d), lambda i, j: (i, 0)),
            pl.BlockSpec((d, tn), gcol), pl.BlockSpec((d, tn), vcol),
            pl.BlockSpec((CONV_W, tn), gcol), pl.BlockSpec((1, tn), gcol),
            pl.BlockSpec((CONV_W, tn), vcol), pl.BlockSpec((1, tn), vcol),
        ],
        out_specs=pl.BlockSpec((tm, tn), lambda i, j: (i, j)),
        compiler_params=_params("arbitrary", "arbitrary"),
        name="ffn_up_conv_glu",
    )(u2, w_up, w_up, cw, cb, cw, cb)


def _ffn_down_kernel(a_ref, w_ref, x1_ref, g2_ref, shn_ref, scn_ref, lng_ref, lnb_ref, x2_ref, un_ref, acc_scr, *, alpha):
    k = pl.program_id(1)

    @pl.when(k == 0)
    def _():
        acc_scr[...] = jnp.zeros_like(acc_scr)

    acc_scr[...] += jnp.dot(a_ref[...], w_ref[...], preferred_element_type=F32)

    @pl.when(k == pl.num_programs(1) - 1)
    def _():
        x2 = _layer_norm_rows(alpha * x1_ref[...] + g2_ref[...] * acc_scr[...], lng_ref[...], lnb_ref[...])
        x2_ref[...] = x2
        un_ref[...] = (x2 * (1.0 + scn_ref[...]) + shn_ref[...]).astype(BF16)


def _ffn_down_call(act, w_down, x1, mod4, layer, next_layer, ln_g, ln_b, cfg):
    t, dff = act.shape
    d = cfg["d"]
    tm, tk = 512, 512
    if dff % tk:
        tk = dff
    vec = lambda n: pl.BlockSpec((1, n), lambda i, k: (0, 0))
    mspec = lambda lay, kk: _mod_spec(lay, kk, d, tm, cfg["ls"], cfg["n_dec"])
    return pl.pallas_call(
        functools.partial(_ffn_down_kernel, alpha=cfg["alpha"]),
        out_shape=[jax.ShapeDtypeStruct((t, d), F32), jax.ShapeDtypeStruct((t, d), BF16)],
        grid=(t // tm, dff // tk),
        in_specs=[
            pl.BlockSpec((tm, tk), lambda i, k: (i, k)),
            pl.BlockSpec((tk, d), lambda i, k: (k, 0)),
            pl.BlockSpec((tm, d), lambda i, k: (i, 0)),
            mspec(layer, 5), mspec(next_layer, 0), mspec(next_layer, 1), vec(d), vec(d),
        ],
        out_specs=[pl.BlockSpec((tm, d), lambda i, k: (i, 0)), pl.BlockSpec((tm, d), lambda i, k: (i, 0))],
        scratch_shapes=[pltpu.VMEM((tm, d), F32)],
        compiler_params=_params("arbitrary", "arbitrary"),
        name="ffn_down_ln",
    )(act, w_down, x1, mod4, mod4, mod4, ln_g[None, :], ln_b[None, :])


def _reorder_w_in(w_in):
    w = GROUP_WIDTH
    o = N_CONV
    s_z = w_in[..., o:o + w]
    o += w
    s_dt = w_in[..., o:o + N_DIR * SSD_HEADS]
    o += N_DIR * SSD_HEADS
    g_z = w_in[..., o:o + w]
    o += w
    g_beta = w_in[..., o:o + N_DIR * GDN_HEADS]
    o += N_DIR * GDN_HEADS
    g_a = w_in[..., o:o + N_DIR * GDN_HEADS]
    o += N_DIR * GDN_HEADS
    rest = w_in[..., o:]
    main = jnp.concatenate([w_in[..., :N_CONV], s_z, g_z, rest], axis=-1).astype(BF16)
    small = jnp.concatenate([s_dt, g_beta, g_a], axis=-1)
    small = jnp.pad(small, ((0, 0), (0, 0), (0, LANES - small.shape[-1]))).astype(BF16)
    return main, small


def kernel(x_prompt, x_sample, state_ssd, state_gdn, state_ret, c, c_ctx, w_mod, b_mod, w_in, conv_w, conv_b,
           hy_w1, hy_b1, hy_w2, hy_b2, hy_w3, hy_freq, hy_bias, ssd_A_log, ssd_dt_bias, ssd_D, ssd_norm_w,
           gdn_A_log, gdn_dt_bias, gdn_norm_w, ret_decay, w_out, ln1_g, ln1_b, w_up, ffn_conv_w, ffn_conv_b,
           w_down, ln2_g, ln2_b):
    n_ctx, lp, d = x_prompt.shape
    n_dec, ls, _ = x_sample.shape
    depth = w_in.shape[0]
    ts, tp = n_dec * ls, n_ctx * lp
    assert ls & (ls - 1) == 0 and lp & (lp - 1) == 0 and tp % ls == 0 and ls % lp == 0
    assert d == 4 * GROUP_WIDTH and w_in.shape[2] == N_MAIN + 2 * N_DIR * (SSD_HEADS // 2 + GDN_HEADS)
    cfg = dict(d=d, ls=ls, lp=lp, n_dec=n_dec, n_ctx=n_ctx, ts=ts, tp=tp, t=ts + tp, alpha=(2 * depth) ** 0.25)

    xs = x_sample.reshape(ts, d)
    xp = x_prompt.reshape(tp, d)

    mod_rows = -(-(n_dec + 1) // SUBLANES) * SUBLANES
    cond = jnp.concatenate([c, c_ctx[None, :], jnp.zeros((mod_rows - n_dec - 1, d), F32)], axis=0)
    mod = _mod_call(cond, w_mod, b_mod)
    mod4 = mod.reshape(depth, mod_rows, 1, 6 * d)

    w_main, w_small = _reorder_w_in(w_in)
    w_out_b, w_up_b, w_down_b = w_out.astype(BF16), w_up.astype(BF16), w_down.astype(BF16)

    dft = {l: _dft_tables(l) for l in (ls, lp)}
    rope_tabs = _rope_tables(cfg)
    rope_blocks = _rope_block_table(cfg)
    tabs = {q: _chunk_tables(cfg, q) for q in sorted({SSD_CHUNK, GDN_CHUNK, RET_CHUNK})}

    def init_state(st, l):
        zeros = jnp.zeros((n_ctx,) + st.shape[2:], F32)
        return jnp.concatenate([st[:, l], zeros], axis=0)

    u = _modulate_call(xs, xp, mod4, 0, cfg)
    x_res = (xs, xp)
    new_ssd, new_gdn, new_ret = [], [], []
    for l in range(depth):
        proj, small = _inproj_call(u, w_main[l], w_small[l], conv_w[l], conv_b[l][None, :], cfg)

        hy = {}
        for name, seq_len, n_seq, row0 in (("s", ls, n_dec, 0), ("p", lp, n_ctx, ts)):
            a_tab, b_tab = dft[seq_len]
            prs = _hy_filter_call(seq_len, a_tab, hy_w1[l], hy_b1[l], hy_w2[l], hy_b2[l], hy_w3[l], hy_freq[l])
            z1 = _hy_conv_call(proj, COL_HY_V, row0, proj, COL_HY_X1, row0, hy_bias[l], a_tab, b_tab, prs, 0, seq_len, n_seq)
            hy[name] = _hy_conv_call(z1, 0, 0, proj, COL_HY_X2, row0, hy_bias[l], a_tab, b_tab, prs, 1, seq_len, n_seq)

        ssd = _ssd_call(proj, small, tabs[SSD_CHUNK], init_state(state_ssd, l), ssd_dt_bias[l], ssd_A_log[l], ssd_D[l], cfg)
        gdn = _gdn_call(proj, small, tabs[GDN_CHUNK], init_state(state_gdn, l), gdn_A_log[l], gdn_dt_bias[l], cfg)
        ret = _ret_call(proj, tabs[RET_CHUNK], rope_tabs, rope_blocks, init_state(state_ret, l), ret_decay[l], cfg)
        new_ssd.append(ssd[2][n_dec:])
        new_gdn.append(gdn[2][n_dec:])
        new_ret.append(ret[2][n_dec:])

        x1, u2 = _outproj_call(hy["s"], hy["p"], ssd, gdn, ret, proj, ssd_norm_w[l], gdn_norm_w[l], w_out_b[l],
                               x_res, mod4, l, ln1_g[l], ln1_b[l], cfg)
        act = _ffn_up_call(u2, w_up_b[l], ffn_conv_w[l], ffn_conv_b[l][None, :], cfg)
        x_res, u = _ffn_down_call(act, w_down_b[l], x1, mod4, l, min(l + 1, depth - 1), ln2_g[l], ln2_b[l], cfg)

    y_sample = x_res[:ts].reshape(n_dec, ls, d)
    y_prompt = x_res[ts:].reshape(n_ctx, lp, d)
    return (y_prompt, y_sample, jnp.stack(new_ssd, 1), jnp.stack(new_gdn, 1), jnp.stack(new_ret, 1))
```

```python
import functools
import math

import numpy as np
import jax
import jax.numpy as jnp
from jax import lax
from jax.experimental import pallas as pl
from jax.experimental.pallas import tpu as pltpu

F32 = jnp.float32
BF16 = jnp.bfloat16
HIGHEST = lax.Precision.HIGHEST

N_DIR = 2
CONV_W = 3
GROUP_WIDTH = 512
HY_ORDER = 2
HY_BANDS = 8
HY_EMB = 1 + 2 * HY_BANDS
HY_FFN = 64
HY_DECAY_TARGET = 1e-2
HY_FAST_PCT = 0.3
HY_SLOW_PCT = 1.5
SSD_HEADS = 8
SSD_HEAD_DIM = 64
SSD_GROUPS = 2
SSD_HPG = SSD_HEADS // SSD_GROUPS
SSD_STATE = 128
SSD_CHUNK = 128
GDN_HEADS = 4
GDN_HEAD_DIM = 128
GDN_CHUNK = 64
RET_HEADS = 4
RET_HEAD_DIM = 128
RET_CHUNK = 128
GRID_W = 64
ROPE_BASE = 10000.0
LN_EPS = 1e-5
RMS_EPS = 1e-6
N_CONV = 8 * GROUP_WIDTH

LANES = 128
SUBLANES = 8
VMEM_LIMIT_BYTES = 60 * 1024 * 1024

COL_HY_V, COL_HY_X1, COL_HY_X2, COL_S_X, COL_S_BC, COL_G_Q, COL_G_K, COL_G_V = range(8)
COL_S_Z, COL_G_Z, COL_R_Q, COL_R_K, COL_R_V, COL_R_G = range(8, 14)
N_MAIN = 14 * GROUP_WIDTH
SM_SDT = 0
SM_GBETA = 16
SM_GA = 24


def _silu(x):
    return x * jax.nn.sigmoid(x)


def _softplus(x):
    return jnp.maximum(x, 0.0) + jnp.log1p(jnp.exp(-jnp.abs(x)))


def _dot(a, b):
    return jnp.dot(a.astype(BF16), b.astype(BF16), preferred_element_type=F32)


def _dot_nt(a, b):
    return lax.dot_general(a.astype(BF16), b.astype(BF16), (((1,), (1,)), ((), ())), preferred_element_type=F32)


def _dot_tn(a, b):
    return lax.dot_general(a.astype(BF16), b.astype(BF16), (((0,), (0,)), ((), ())), preferred_element_type=F32)


def _dot_hi(a, b):
    return jnp.dot(a, b, preferred_element_type=F32, precision=HIGHEST)


def _params(*semantics):
    return pltpu.CompilerParams(dimension_semantics=semantics, vmem_limit_bytes=VMEM_LIMIT_BYTES)


def _tri(n, lower):
    r = lax.broadcasted_iota(jnp.int32, (n, n), 0)
    c = lax.broadcasted_iota(jnp.int32, (n, n), 1)
    return jnp.where((r >= c) if lower else (r <= c), 1.0, 0.0).astype(F32)


def _mod_kernel(c_ref, w_ref, b_ref, o_ref):
    a = _silu(c_ref[...])
    o_ref[...] = _dot(a, w_ref[...]) + b_ref[...]


def _mod_call(cond, w_mod, b_mod):
    depth, d, n = w_mod.shape
    rows = cond.shape[0]
    tn = 1024
    return pl.pallas_call(
        _mod_kernel,
        out_shape=jax.ShapeDtypeStruct((depth, rows, n), F32),
        grid=(depth, n // tn),
        in_specs=[
            pl.BlockSpec((rows, d), lambda l, j: (0, 0)),
            pl.BlockSpec((None, d, tn), lambda l, j: (l, 0, j)),
            pl.BlockSpec((None, 1, tn), lambda l, j: (l, 0, j)),
        ],
        out_specs=pl.BlockSpec((None, rows, tn), lambda l, j: (l, 0, j)),
        compiler_params=_params("arbitrary", "arbitrary"),
        name="adaln_mod",
    )(cond, w_mod, b_mod.reshape(depth, 1, n))


def _mod_spec(layer, k, d, rows_per_tile, l_sample, n_dec):
    tiles_per_seq = l_sample // rows_per_tile

    def index_map(i, *_):
        return (layer, jnp.minimum(i // tiles_per_seq, n_dec), 0, k)

    return pl.BlockSpec((None, None, 1, d), index_map)


def _modulate_kernel(xs_ref, xp_ref, sh_ref, sc_ref, o_ref, *, n_sample_tiles):
    i = pl.program_id(0)
    x = jnp.where(i < n_sample_tiles, xs_ref[...], xp_ref[...])
    o_ref[...] = (x * (1.0 + sc_ref[...]) + sh_ref[...]).astype(BF16)


def _two_group_specs(tm, d, n_sample_tiles):
    return [
        pl.BlockSpec((tm, d), lambda i, *_: (jnp.minimum(i, n_sample_tiles - 1), 0)),
        pl.BlockSpec((tm, d), lambda i, *_: (jnp.maximum(i - n_sample_tiles, 0), 0)),
    ]


def _modulate_call(xs, xp, mod4, layer, cfg):
    tm, d = 256, cfg["d"]
    t = cfg["t"]
    ns = cfg["ts"] // tm
    return pl.pallas_call(
        functools.partial(_modulate_kernel, n_sample_tiles=ns),
        out_shape=jax.ShapeDtypeStruct((t, d), BF16),
        grid=(t // tm,),
        in_specs=_two_group_specs(tm, d, ns)
        + [_mod_spec(layer, 0, d, tm, cfg["ls"], cfg["n_dec"]), _mod_spec(layer, 1, d, tm, cfg["ls"], cfg["n_dec"])],
        out_specs=pl.BlockSpec((tm, d), lambda i: (i, 0)),
        compiler_params=_params("arbitrary"),
        name="modulate_in",
    )(xs, xp, mod4, mod4)


CONV_ROW_CHUNK = 512


def _matmul_conv_chunks(u_ref, w, cw, cb, seq_len, min_seq_len, finish):
    rows = u_ref.shape[0]
    rc = min(CONV_ROW_CHUNK, min_seq_len)
    assert rows % rc == 0 and min_seq_len % rc == 0
    n_chunks = rows // rc
    n = w.shape[1]
    first = lax.broadcasted_iota(jnp.int32, (SUBLANES, 1), 0) == 0
    last = lax.broadcasted_iota(jnp.int32, (SUBLANES, 1), 0) == SUBLANES - 1
    zero_row = jnp.zeros((1, n), F32)
    hs = {}
    for r in range(n_chunks + 1):
        if r < n_chunks:
            hs[r] = jnp.dot(u_ref[r * rc:(r + 1) * rc, :], w, preferred_element_type=F32)
        if r == 0:
            continue
        c = r - 1
        h = hs[c]
        row0 = c * rc
        prev_row = hs[c - 1][rc - 1:rc] if c > 0 else zero_row
        next_row = hs[c + 1][0:1] if c + 1 < n_chunks else zero_row
        prev_row = jnp.where((row0 & (seq_len - 1)) != 0, prev_row, 0.0)
        next_row = jnp.where(((row0 + rc) & (seq_len - 1)) != 0, next_row, 0.0)
        hp = pltpu.roll(h, 1, axis=0)
        hn = pltpu.roll(h, rc - 1, axis=0)
        hp = jnp.concatenate([jnp.where(first, prev_row, hp[:SUBLANES]), hp[SUBLANES:]], axis=0)
        hn = jnp.concatenate([hn[:rc - SUBLANES], jnp.where(last, next_row, hn[rc - SUBLANES:])], axis=0)
        finish(slice(row0, row0 + rc), hp * cw[0:1] + h * cw[1:2] + hn * cw[2:3] + cb)
        hs.pop(c - 1, None)


def _inproj_kernel(u_ref, w_ref, ws_ref, cw_ref, cb_ref, o_ref, os_ref, *, n_conv_tiles, n_sample_tiles, ls, lp):
    i = pl.program_id(0)
    j = pl.program_id(1)
    seq_len = jnp.where(i < n_sample_tiles, ls, lp)

    @pl.when(j < n_conv_tiles)
    def _():
        def finish(rows, conv):
            o_ref[rows, :] = conv

        _matmul_conv_chunks(u_ref, w_ref[...], cw_ref[...], cb_ref[...], seq_len, min(ls, lp), finish)

    @pl.when(j >= n_conv_tiles)
    def _():
        o_ref[...] = jnp.dot(u_ref[...], w_ref[...], preferred_element_type=F32)

    @pl.when(j == 0)
    def _():
        os_ref[...] = jnp.dot(u_ref[...], ws_ref[...], preferred_element_type=F32)


def _inproj_call(u, w_main, w_small, cw, cb, layer, cfg):
    t, d = u.shape
    tm, tn = cfg["ls"], 512
    n_conv_tiles = N_CONV // tn
    kern = functools.partial(
        _inproj_kernel, n_conv_tiles=n_conv_tiles, n_sample_tiles=cfg["ts"] // tm, ls=cfg["ls"], lp=cfg["lp"]
    )
    conv_col = lambda i, j: (layer, 0, jnp.minimum(j, n_conv_tiles - 1))
    return pl.pallas_call(
        kern,
        out_shape=[jax.ShapeDtypeStruct((t, N_MAIN), F32), jax.ShapeDtypeStruct((t, LANES), F32)],
        grid=(t // tm, N_MAIN // tn),
        in_specs=[
            pl.BlockSpec((tm, d), lambda i, j: (i, 0)),
            pl.BlockSpec((None, d, tn), lambda i, j: (layer, 0, j)),
            pl.BlockSpec((None, d, LANES), lambda i, j: (layer, 0, 0)),
            pl.BlockSpec((None, CONV_W, tn), conv_col),
            pl.BlockSpec((None, 1, tn), conv_col),
        ],
        out_specs=[pl.BlockSpec((tm, tn), lambda i, j: (i, j)), pl.BlockSpec((tm, LANES), lambda i, j: (i, 0))],
        compiler_params=_params("arbitrary", "arbitrary"),
        name="in_proj_conv",
    )(u, w_main, w_small, cw, cb)


@functools.lru_cache(maxsize=None)
def _dft_tables(l):
    n = 2 * l
    k = np.arange(l, dtype=np.int64)
    ang = (2.0 * math.pi / n) * ((k[:, None] * k[None, :]) % n)
    cos, sin = np.cos(ang), np.sin(ang)
    alt = np.where(k % 2 == 0, 1.0, -1.0)
    a_im = np.where(k[:, None] == 0, alt[None, :], -sin)
    a = np.concatenate([cos, a_im], axis=0)
    ck = np.where(k == 0, 1.0, 2.0) / n
    b_re = cos * ck[None, :]
    b_im = np.where(k[None, :] == 0, alt[:, None] / n, -sin * (2.0 / n))
    b = np.concatenate([b_re, b_im], axis=1)
    return a.astype(np.float32), b.astype(np.float32)


@functools.lru_cache(maxsize=None)
def _hy_feats(l):
    pos = np.arange(l, dtype=np.float32)
    t = pos / np.float32(l - 1)
    bands = np.linspace(1e-4, HY_BANDS - 1, HY_BANDS, dtype=np.float32)
    ang = np.float32(2.0 * math.pi / l) * pos[:, None] * bands
    feats = np.concatenate([t[:, None], np.cos(ang), -np.sin(ang)], -1).astype(np.float32)
    feats = np.pad(feats, ((0, 0), (0, LANES - HY_EMB)))
    return feats, t[:, None]


@functools.lru_cache(maxsize=None)
def _hy_deltas(c):
    lo, hi = math.log(HY_DECAY_TARGET) / HY_SLOW_PCT, math.log(HY_DECAY_TARGET) / HY_FAST_PCT
    return np.abs(np.linspace(lo, hi, c, dtype=np.float32))[None, :]


def _hy_filter_kernel(
    feats_ref, t_ref, dl_ref, w1_ref, b1_ref, w2_ref, b2_ref, w3_ref, fq_ref, are_ref, aim_ref,
    p_ref, r_ref, s_ref, sum_scr, dif_scr, nyq_scr,
):
    f = pl.program_id(0)
    l = feats_ref.shape[0]
    c = dl_ref.shape[1]
    kb = are_ref.shape[0]

    @pl.when(f == 0)
    def _():
        fq = fq_ref[...]
        h = jnp.sin(fq[0:1] * (_dot_hi(feats_ref[...], w1_ref[...]) + b1_ref[...]))
        h = jnp.sin(fq[1:2] * (_dot_hi(h, w2_ref[...]) + b2_ref[...]))
        decay = jnp.exp(-t_ref[...] * dl_ref[...])
        row = lax.broadcasted_iota(jnp.int32, (l, 1), 0)
        sign = jnp.where((row & 1) == 0, 1.0, -1.0)
        for o in range(HY_ORDER):
            fwd = _dot_hi(h, w3_ref[:, (2 * o) * c:(2 * o + 1) * c]) * decay
            bwd = _dot_hi(h, w3_ref[:, (2 * o + 1) * c:(2 * o + 2) * c]) * decay
            bwd = jnp.where(row == 0, 0.0, bwd)
            norm = jnp.sum(jnp.abs(fwd), axis=0, keepdims=True) + jnp.sum(jnp.abs(bwd), axis=0, keepdims=True)
            fwd = fwd / norm
            bwd = bwd / norm
            ssum = fwd + bwd
            sum_scr[o] = ssum.astype(BF16)
            dif_scr[o] = (fwd - bwd).astype(BF16)
            nyq_scr[o] = jnp.broadcast_to(jnp.sum(ssum * sign, axis=0, keepdims=True), (SUBLANES, c))

    grow = f * kb + lax.broadcasted_iota(jnp.int32, (kb, 1), 0)
    for o in range(HY_ORDER):
        kre = jnp.dot(are_ref[...], sum_scr[o], preferred_element_type=F32)
        kim = jnp.dot(aim_ref[...], dif_scr[o], preferred_element_type=F32)
        p_ref[o] = kre
        r_ref[o] = jnp.where(grow == 0, 0.0, kim)
        s_ref[o] = jnp.where(grow == 0, nyq_scr[o][0:1], kre)


def _hy_filter_call(l, a_tab, w1, b1, w2, b2, w3, freq):
    c = GROUP_WIDTH
    kb = min(l, 256)
    nf = l // kb
    feats, t = _hy_feats(l)
    w1p = jnp.pad(w1, ((0, LANES - HY_EMB), (0, 0)))
    full = lambda shape: pl.BlockSpec(shape, lambda f: tuple(0 for _ in shape))
    out = jax.ShapeDtypeStruct((HY_ORDER, l, c), F32)
    out_spec = pl.BlockSpec((HY_ORDER, kb, c), lambda f: (0, f, 0))
    return pl.pallas_call(
        _hy_filter_kernel,
        out_shape=[out, out, out],
        grid=(nf,),
        in_specs=[
            full((l, LANES)), full((l, 1)), full((1, c)), full((LANES, HY_FFN)), full((1, HY_FFN)),
            full((HY_FFN, HY_FFN)), full((1, HY_FFN)), full((HY_FFN, 2 * HY_ORDER * c)), full((2, HY_FFN)),
            pl.BlockSpec((kb, l), lambda f: (f, 0)),
            pl.BlockSpec((kb, l), lambda f: (f + nf, 0)),
        ],
        out_specs=[out_spec, out_spec, out_spec],
        scratch_shapes=[
            pltpu.VMEM((HY_ORDER, l, c), BF16), pltpu.VMEM((HY_ORDER, l, c), BF16),
            pltpu.VMEM((HY_ORDER, SUBLANES, c), F32),
        ],
        compiler_params=_params("arbitrary"),
        name="hyena_filter",
    )(feats, t, _hy_deltas(c), w1p, b1[None, :], w2, b2[None, :], w3, freq, a_tab, a_tab)


def _hy_conv_kernel(z_ref, g_ref, bias_ref, are_ref, aim_ref, bre_ref, bim_ref, p_ref, r_ref, s_ref, o_ref, zb_scr):
    f = pl.program_id(2)

    @pl.when(f == 0)
    def _():
        zb_scr[...] = z_ref[...].astype(BF16)
        o_ref[...] = jnp.zeros_like(o_ref)

    zb = zb_scr[...]
    re = jnp.dot(are_ref[...], zb, preferred_element_type=F32)
    im = jnp.dot(aim_ref[...], zb, preferred_element_type=F32)
    p, r, s = p_ref[...], r_ref[...], s_ref[...]
    re2 = re * p - im * r
    im2 = re * r + im * s
    o_ref[...] += _dot(bre_ref[...], re2) + _dot(bim_ref[...], im2)

    @pl.when(f == pl.num_programs(2) - 1)
    def _():
        o_ref[...] = g_ref[...] * (o_ref[...] + z_ref[...] * bias_ref[...])


def _hy_conv_call(z, z_col, z_row0, g, g_col, g_row0, bias, a_tab, b_tab, prs, order, l, n_seq):
    c = GROUP_WIDTH
    ct = c
    ncb = c // ct
    kb = min(l, 512)
    nf = l // kb
    p, r, s = prs
    zspec = pl.BlockSpec((l, ct), lambda b, cb, f: (z_row0 // l + b, z_col * ncb + cb))
    gspec = pl.BlockSpec((l, ct), lambda b, cb, f: (g_row0 // l + b, g_col * ncb + cb))
    kspec = pl.BlockSpec((None, kb, ct), lambda b, cb, f: (order, f, cb))
    return pl.pallas_call(
        _hy_conv_kernel,
        out_shape=jax.ShapeDtypeStruct((n_seq * l, c), F32),
        grid=(n_seq, ncb, nf),
        in_specs=[
            zspec, gspec,
            pl.BlockSpec((None, 1, ct), lambda b, cb, f: (order, 0, cb)),
            pl.BlockSpec((kb, l), lambda b, cb, f: (f, 0)),
            pl.BlockSpec((kb, l), lambda b, cb, f: (f + nf, 0)),
            pl.BlockSpec((l, kb), lambda b, cb, f: (0, f)),
            pl.BlockSpec((l, kb), lambda b, cb, f: (0, f + nf)),
            kspec, kspec, kspec,
        ],
        out_specs=pl.BlockSpec((l, ct), lambda b, cb, f: (b, cb)),
        scratch_shapes=[pltpu.VMEM((l, ct), BF16)],
        compiler_params=_params("arbitrary", "arbitrary", "arbitrary"),
        name="hyena_conv",
    )(z, g, bias.reshape(HY_ORDER, 1, c), a_tab, a_tab, b_tab, b_tab, p, r, s)


def _chunk_tables(cfg, q):
    fwd, bwd, first, last, seq = [], [], [], [], []
    sid = 0
    for n_seq, l, row0 in ((cfg["n_dec"], cfg["ls"], 0), (cfg["n_ctx"], cfg["lp"], cfg["ts"])):
        nc = l // q
        for b in range(n_seq):
            base = (row0 + b * l) // q
            for c in range(nc):
                fwd.append(base + c)
                bwd.append(base + nc - 1 - c)
                first.append(int(c == 0))
                last.append(int(c == nc - 1))
                seq.append(sid)
            sid += 1
    return tuple(jnp.asarray(np.asarray(a, np.int32)) for a in (fwd, bwd, first, last, seq))


def _scan_masks(q, d):
    r = lax.broadcasted_iota(jnp.int32, (q, q), 0)
    c = lax.broadcasted_iota(jnp.int32, (q, q), 1)
    return ((r >= c), (r > c)) if d == 0 else ((r <= c), (r < c))


def _cumsum_pair(a_col, a_row, q, d):
    lo, up = _tri(q, True), _tri(q, False)
    if d == 0:
        return _dot_hi(lo, a_col), _dot_hi(a_row, up)
    return _dot_hi(up, a_col), _dot_hi(a_row, lo)


def _ssd_kernel(
    fwd_t, bwd_t, first_t, last_t, seq_t,
    xf_ref, bcf_ref, smf_ref, xb_ref, bcb_ref, smb_ref, pr_ref, pc_ref, dsk_ref, h0_ref,
    yf_ref, yb_ref, hout_ref, h_scr,
):
    s = pl.program_id(0)
    q, p, n = SSD_CHUNK, SSD_HEAD_DIM, SSD_STATE

    @pl.when(first_t[s] == 1)
    def _():
        h_scr[...] = h0_ref[...]

    bias_r, alog_r = pr_ref[0:1], pr_ref[1:2]
    bias_c, alog_c = pc_ref[:, 0:1], pc_ref[:, 1:2]
    probs = []
    for d in range(N_DIR):
        x_ref, bc_ref, sm_ref = (xf_ref, bcf_ref, smf_ref) if d == 0 else (xb_ref, bcb_ref, smb_ref)
        x = _silu(x_ref[...])
        bc = _silu(bc_ref[...])
        sm = sm_ref[...]
        dt_c = _softplus(sm + bias_r)
        a_c = dt_c * (-jnp.exp(alog_r))
        a_r = _softplus(sm.T + bias_c) * (-jnp.exp(alog_c))
        acs_c, acs_r = _cumsum_pair(a_c, a_r[SM_SDT:SM_SDT + N_DIR * SSD_HEADS], q, d)
        incl, _ = _scan_masks(q, d)
        end = q - 1 if d == 0 else 0
        for g in range(SSD_GROUPS):
            bm = bc[:, g * n:(g + 1) * n]
            cm = bc[:, (SSD_GROUPS + g) * n:(SSD_GROUPS + g + 1) * n]
            cb = _dot_nt(cm, bm)
            for e in range(SSD_HPG):
                hd = g * SSD_HPG + e
                idx = d * SSD_HEADS + hd
                col = acs_c[:, idx:idx + 1]
                xh = x[:, hd * p:(hd + 1) * p]
                probs.append(dict(
                    d=d, hd=hd, bm=bm, cm=cm, cb=cb, col=col, row=acs_r[idx:idx + 1, :], incl=incl, xh=xh,
                    xs=xh * dt_c[:, idx:idx + 1], tot=col[end:end + 1], skip=dsk_ref[d:d + 1, hd * p:(hd + 1) * p],
                ))
    hst = [h_scr[pb["d"], pb["hd"]] for pb in probs]
    lmat = [jnp.exp(jnp.where(pb["incl"], pb["col"] - pb["row"], -1e30)) for pb in probs]
    y_diag = [_dot(pb["cb"] * lm, pb["xs"]) for pb, lm in zip(probs, lmat)]
    y_off = [_dot_nt(pb["cm"], h_i) for pb, h_i in zip(probs, hst)]
    states = [_dot_tn(pb["xs"] * jnp.exp(pb["tot"] - pb["col"]), pb["bm"]) for pb in probs]
    ys = [yd + yo * jnp.exp(pb["col"]) + pb["skip"] * pb["xh"] for pb, yd, yo in zip(probs, y_diag, y_off)]
    for pb, h_i, st_i in zip(probs, hst, states):
        h_scr[pb["d"], pb["hd"]] = h_i * jnp.exp(pb["tot"]) + st_i
    yf_ref[...] = jnp.concatenate(ys[:SSD_HEADS], axis=-1)
    yb_ref[...] = jnp.concatenate(ys[SSD_HEADS:], axis=-1)

    @pl.when(last_t[s] == 1)
    def _():
        hout_ref[...] = h_scr[...]


def _ssd_call(proj, small, tables, h0, dt_bias, a_log, d_skip, cfg):
    q, w = SSD_CHUNK, GROUP_WIDTH
    t = proj.shape[0]
    n_steps = tables[0].shape[0]
    nlane = N_DIR * SSD_HEADS
    pr = jnp.zeros((SUBLANES, LANES), F32)
    pr = pr.at[0, SM_SDT:SM_SDT + nlane].set(dt_bias.reshape(-1)).at[1, SM_SDT:SM_SDT + nlane].set(a_log.reshape(-1))
    pc = pr.T
    dsk = jnp.repeat(d_skip, SSD_HEAD_DIM, axis=-1)
    st_shape = (N_DIR, SSD_HEADS, SSD_HEAD_DIM, SSD_STATE)
    fmap = lambda col: (lambda s, fw, bw, fi, la, sq: (fw[s], col))
    bmap = lambda col: (lambda s, fw, bw, fi, la, sq: (bw[s], col))
    const = lambda shape: pl.BlockSpec(shape, lambda s, *_: tuple(0 for _ in shape))
    st_spec = pl.BlockSpec((None,) + st_shape, lambda s, fw, bw, fi, la, sq: (sq[s], 0, 0, 0, 0))
    grid_spec = pltpu.PrefetchScalarGridSpec(
        num_scalar_prefetch=5,
        grid=(n_steps,),
        in_specs=[
            pl.BlockSpec((q, w), fmap(COL_S_X)), pl.BlockSpec((q, w), fmap(COL_S_BC)), pl.BlockSpec((q, LANES), fmap(0)),
            pl.BlockSpec((q, w), bmap(COL_S_X)), pl.BlockSpec((q, w), bmap(COL_S_BC)), pl.BlockSpec((q, LANES), bmap(0)),
            const((SUBLANES, LANES)), const((LANES, SUBLANES)), const((N_DIR, w)), st_spec,
        ],
        out_specs=[pl.BlockSpec((q, w), fmap(0)), pl.BlockSpec((q, w), bmap(0)), st_spec],
        scratch_shapes=[pltpu.VMEM(st_shape, F32)],
    )
    n_seq = cfg["n_dec"] + cfg["n_ctx"]
    return pl.pallas_call(
        _ssd_kernel,
        out_shape=[
            jax.ShapeDtypeStruct((t, w), F32), jax.ShapeDtypeStruct((t, w), F32),
            jax.ShapeDtypeStruct((n_seq,) + st_shape, F32),
        ],
        grid_spec=grid_spec,
        compiler_params=_params("arbitrary"),
        name="ssd_scan",
    )(*tables, proj, proj, small, proj, proj, small, pr, pc, dsk, h0)


TRI_BLOCK = 2 * SUBLANES


def _unit_tri_inverses(mats, lowers):
    n = mats[0].shape[0]
    blk = TRI_BLOCK
    sub = blk // SUBLANES
    lane = lax.broadcasted_iota(jnp.int32, (blk, n), 1)
    row = lax.broadcasted_iota(jnp.int32, (blk, n), 0)
    group = (lax.broadcasted_iota(jnp.int32, (SUBLANES, LANES), 1) // blk) * blk
    tiles = lambda v: [v[t * SUBLANES:(t + 1) * SUBLANES, :] for t in range(sub)]
    eye = jnp.where(lane % blk == row, 1.0, 0.0)
    d_t, inv_t = [], []
    for a in mats:
        packed = jnp.zeros((blk, n), F32)
        for bi in range(n // blk):
            packed = jnp.where(lane // blk == bi, a[bi * blk:(bi + 1) * blk, :], packed)
        if n < LANES:
            packed = jnp.concatenate([packed, jnp.zeros((blk, LANES - n), F32)], axis=1)
        d_t.append(tiles(packed))
        inv_t.append(tiles(eye))
    for step in range(blk):
        for i, lower in enumerate(lowers):
            jj = step if lower else blk - 1 - step
            st, jr = divmod(jj, SUBLANES)
            done_row = inv_t[i][st][jr:jr + 1, :]
            for s in (range(st, sub) if lower else range(st + 1)):
                col = jnp.take_along_axis(d_t[i][s], group + jj, axis=1)[:, :n]
                inv_t[i][s] = inv_t[i][s] - col * done_row
    ts = []
    for i in range(len(mats)):
        inv = jnp.concatenate(inv_t[i], axis=0)
        ts.append(jnp.concatenate([jnp.where(lane // blk == bi, inv, 0.0) for bi in range(n // blk)], axis=0))
    r = lax.broadcasted_iota(jnp.int32, (n, n), 0)
    c = lax.broadcasted_iota(jnp.int32, (n, n), 1)
    size = blk
    while size < n:
        level = (r // (2 * size) == c // (2 * size)) & (r // size != c // size)
        half = [_dot(t, jnp.where(level, a, 0.0)) for t, a in zip(ts, mats)]
        ts = [t - _dot(h, t) for t, h in zip(ts, half)]
        size *= 2
    return ts


def _l2norm_heads(x, heads, dim, scale):
    outs = []
    for h in range(heads):
        xh = x[:, h * dim:(h + 1) * dim]
        outs.append(xh * (lax.rsqrt(jnp.sum(xh * xh, axis=-1, keepdims=True) + 1e-6) * scale))
    return outs


def _gdn_kernel(
    fwd_t, bwd_t, first_t, last_t, seq_t,
    qf_ref, kf_ref, vf_ref, smf_ref, qb_ref, kb_ref, vb_ref, smb_ref, pr_ref, pc_ref, s0_ref,
    of_ref, ob_ref, sout_ref, s_scr,
):
    s = pl.program_id(0)
    q, dim = GDN_CHUNK, GDN_HEAD_DIM

    @pl.when(first_t[s] == 1)
    def _():
        s_scr[...] = s0_ref[...]

    bias_r, alog_r = pr_ref[0:1], pr_ref[1:2]
    bias_c, alog_c = pc_ref[:, 0:1], pc_ref[:, 1:2]
    probs = []
    for d in range(N_DIR):
        q_ref, k_ref, v_ref, sm_ref = (qf_ref, kf_ref, vf_ref, smf_ref) if d == 0 else (qb_ref, kb_ref, vb_ref, smb_ref)
        qs = _l2norm_heads(_silu(q_ref[...]), GDN_HEADS, dim, dim ** -0.5)
        ks = _l2norm_heads(_silu(k_ref[...]), GDN_HEADS, dim, 1.0)
        v = _silu(v_ref[...])
        sm = sm_ref[...]
        beta_c = jax.nn.sigmoid(sm)
        g_c = -jnp.exp(alog_r) * _softplus(sm + bias_r)
        g_r = -jnp.exp(alog_c) * _softplus(sm.T + bias_c)
        gc_c, gc_r = _cumsum_pair(g_c, g_r[SM_GA:SM_GA + N_DIR * GDN_HEADS], q, d)
        incl, strict = _scan_masks(q, d)
        end = q - 1 if d == 0 else 0
        for h in range(GDN_HEADS):
            idx = d * GDN_HEADS + h
            col = gc_c[:, SM_GA + idx:SM_GA + idx + 1]
            probs.append(dict(
                d=d, h=h, q=qs[h], k=ks[h], v=v[:, h * dim:(h + 1) * dim], col=col, row=gc_r[idx:idx + 1, :],
                beta=beta_c[:, SM_GBETA + idx:SM_GBETA + idx + 1], tot=col[end:end + 1], incl=incl, strict=strict,
            ))
    kk = [_dot_nt(p["k"], p["k"]) for p in probs]
    qk = [_dot_nt(p["q"], p["k"]) for p in probs]
    dmask = [jnp.exp(jnp.where(p["incl"], p["col"] - p["row"], -1e30)) for p in probs]
    a = [jnp.where(p["strict"], p["beta"] * kk_i * dm, 0.0) for p, kk_i, dm in zip(probs, kk, dmask)]
    ecol = [jnp.exp(p["col"]) for p in probs]
    rhs = [jnp.concatenate([p["v"] * p["beta"], p["k"] * (p["beta"] * e)], axis=-1) for p, e in zip(probs, ecol)]
    inv = _unit_tri_inverses(a, [p["d"] == 0 for p in probs])
    sol = [_dot(t, r) for t, r in zip(inv, rhs)]
    st = [s_scr[p["d"], p["h"]] for p in probs]
    w_s = [_dot(x[:, dim:], s_i) for x, s_i in zip(sol, st)]
    q_s = [_dot(p["q"] * e, s_i) for p, e, s_i in zip(probs, ecol, st)]
    vv = [x[:, :dim] - ws for x, ws in zip(sol, w_s)]
    outs = [qs_i + _dot(qk_i * dm, vv_i) for qs_i, qk_i, dm, vv_i in zip(q_s, qk, dmask, vv)]
    upd = [_dot_tn(p["k"] * jnp.exp(p["tot"] - p["col"]), vv_i) for p, vv_i in zip(probs, vv)]
    for p, s_i, u_i in zip(probs, st, upd):
        s_scr[p["d"], p["h"]] = s_i * jnp.exp(p["tot"]) + u_i
    of_ref[...] = jnp.concatenate(outs[:GDN_HEADS], axis=-1)
    ob_ref[...] = jnp.concatenate(outs[GDN_HEADS:], axis=-1)

    @pl.when(last_t[s] == 1)
    def _():
        sout_ref[...] = s_scr[...]


def _gdn_call(proj, small, tables, s0, a_log, dt_bias, cfg):
    q, w = GDN_CHUNK, GROUP_WIDTH
    t = proj.shape[0]
    n_steps = tables[0].shape[0]
    nlane = N_DIR * GDN_HEADS
    pr = jnp.zeros((SUBLANES, LANES), F32)
    pr = pr.at[0, SM_GA:SM_GA + nlane].set(dt_bias.reshape(-1)).at[1, SM_GA:SM_GA + nlane].set(a_log.reshape(-1))
    pc = pr.T
    st_shape = (N_DIR, GDN_HEADS, GDN_HEAD_DIM, GDN_HEAD_DIM)
    fmap = lambda col: (lambda s, fw, bw, fi, la, sq: (fw[s], col))
    bmap = lambda col: (lambda s, fw, bw, fi, la, sq: (bw[s], col))
    const = lambda shape: pl.BlockSpec(shape, lambda s, *_: tuple(0 for _ in shape))
    st_spec = pl.BlockSpec((None,) + st_shape, lambda s, fw, bw, fi, la, sq: (sq[s], 0, 0, 0, 0))
    blk = lambda m, col: pl.BlockSpec((q, w), m(col))
    grid_spec = pltpu.PrefetchScalarGridSpec(
        num_scalar_prefetch=5,
        grid=(n_steps,),
        in_specs=[
            blk(fmap, COL_G_Q), blk(fmap, COL_G_K), blk(fmap, COL_G_V), pl.BlockSpec((q, LANES), fmap(0)),
            blk(bmap, COL_G_Q), blk(bmap, COL_G_K), blk(bmap, COL_G_V), pl.BlockSpec((q, LANES), bmap(0)),
            const((SUBLANES, LANES)), const((LANES, SUBLANES)), st_spec,
        ],
        out_specs=[pl.BlockSpec((q, w), fmap(0)), pl.BlockSpec((q, w), bmap(0)), st_spec],
        scratch_shapes=[pltpu.VMEM(st_shape, F32)],
    )
    n_seq = cfg["n_dec"] + cfg["n_ctx"]
    return pl.pallas_call(
        _gdn_kernel,
        out_shape=[
            jax.ShapeDtypeStruct((t, w), F32), jax.ShapeDtypeStruct((t, w), F32),
            jax.ShapeDtypeStruct((n_seq,) + st_shape, F32),
        ],
        grid_spec=grid_spec,
        compiler_params=_params("arbitrary"),
        name="gdn_scan",
    )(*tables, proj, proj, proj, small, proj, proj, proj, small, pr, pc, s0)


def _rope_tables(cfg):
    l = cfg["ls"]
    pos = np.arange(l)
    r = (pos // GRID_W).astype(np.float32)
    col = (pos % GRID_W).astype(np.float32)
    nf = RET_HEAD_DIM // 4
    inv = np.power(np.float32(ROPE_BASE), -np.arange(nf, dtype=np.float32) / np.float32(nf)).astype(np.float32)
    ang = np.concatenate([r[:, None] * inv, col[:, None] * inv], -1)
    cos, sin = np.cos(ang), np.sin(ang)
    cos2 = np.concatenate([cos, cos], -1)
    sin2 = np.concatenate([-sin, sin], -1)
    ident = np.ones((RET_CHUNK, RET_HEAD_DIM), np.float32)
    return np.concatenate([cos2, ident], 0).astype(np.float32), np.concatenate([sin2, 0.0 * ident], 0).astype(np.float32)


def _rope_block_table(cfg):
    q = RET_CHUNK
    nc_s, nc_p = cfg["ls"] // q, cfg["lp"] // q
    fwd = [c for _ in range(cfg["n_dec"]) for c in range(nc_s)] + [nc_s] * (cfg["n_ctx"] * nc_p)
    bwd = [nc_s - 1 - c for _ in range(cfg["n_dec"]) for c in range(nc_s)] + [nc_s] * (cfg["n_ctx"] * nc_p)
    return jnp.asarray(np.asarray(fwd, np.int32)), jnp.asarray(np.asarray(bwd, np.int32))


def _ret_kernel(
    fwd_t, bwd_t, first_t, last_t, seq_t, rf_t, rb_t,
    qf_ref, kf_ref, vf_ref, cf_ref, sf_ref, qb_ref, kb_ref, vb_ref, cb_ref, sb_ref, dec_ref, s0_ref,
    of_ref, ob_ref, sout_ref, s_scr,
):
    s = pl.program_id(0)
    q, dim = RET_CHUNK, RET_HEAD_DIM

    @pl.when(first_t[s] == 1)
    def _():
        s_scr[...] = s0_ref[...]

    lg_all = -jnp.exp(dec_ref[...])
    ri = lax.broadcasted_iota(jnp.int32, (q, q), 0)
    ci = lax.broadcasted_iota(jnp.int32, (q, q), 1)
    rpos = lax.broadcasted_iota(jnp.int32, (q, 1), 0)
    probs = []
    for d in range(N_DIR):
        q_ref, k_ref, v_ref, c_ref, sn_ref = (
            (qf_ref, kf_ref, vf_ref, cf_ref, sf_ref) if d == 0 else (qb_ref, kb_ref, vb_ref, cb_ref, sb_ref)
        )
        qa, ka, va = q_ref[...], k_ref[...], v_ref[...]
        cos, sin = c_ref[...], sn_ref[...]
        rel = ((ri - ci) if d == 0 else (ci - ri)).astype(F32)
        pos = (rpos if d == 0 else (q - 1 - rpos)).astype(F32)
        for h in range(RET_HEADS):
            idx = d * RET_HEADS + h
            lg = lg_all[idx:idx + 1, :]
            qh = qa[:, h * dim:(h + 1) * dim]
            kh = ka[:, h * dim:(h + 1) * dim] * dim ** -0.5
            probs.append(dict(
                d=d, h=h, lg=lg, lg1=lg[:, 0:1], rel=rel, pos=pos, v=va[:, h * dim:(h + 1) * dim],
                q=qh * cos + pltpu.roll(qh, dim // 2, axis=1) * sin,
                k=kh * cos + pltpu.roll(kh, dim // 2, axis=1) * sin,
            ))
    st = [s_scr[pb["d"], pb["h"]] for pb in probs]
    qk = [_dot_nt(pb["q"], pb["k"]) for pb in probs]
    dmask = [jnp.exp(jnp.where(pb["rel"] >= 0, pb["rel"] * pb["lg"], -1e30)) for pb in probs]
    o_state = [_dot(pb["q"] * jnp.exp((pb["pos"] + 1.0) * pb["lg1"]), s_i) for pb, s_i in zip(probs, st)]
    o_attn = [_dot(qk_i * dm, pb["v"]) for pb, qk_i, dm in zip(probs, qk, dmask)]
    upd = [_dot_tn(pb["k"] * jnp.exp((q - 1.0 - pb["pos"]) * pb["lg1"]), pb["v"]) for pb in probs]
    for pb, s_i, u_i in zip(probs, st, upd):
        s_scr[pb["d"], pb["h"]] = s_i * jnp.exp(q * pb["lg"]) + u_i
    outs = [a + b for a, b in zip(o_state, o_attn)]
    of_ref[...] = jnp.concatenate(outs[:RET_HEADS], axis=-1)
    ob_ref[...] = jnp.concatenate(outs[RET_HEADS:], axis=-1)

    @pl.when(last_t[s] == 1)
    def _():
        sout_ref[...] = s_scr[...]


def _ret_call(proj, tables, rope_tabs, rope_blocks, s0, ret_decay, cfg):
    q, w, dim = RET_CHUNK, GROUP_WIDTH, RET_HEAD_DIM
    t = proj.shape[0]
    n_steps = tables[0].shape[0]
    dec = jnp.broadcast_to(ret_decay.reshape(-1, 1), (N_DIR * RET_HEADS, LANES))
    st_shape = (N_DIR, RET_HEADS, dim, dim)
    fmap = lambda col: (lambda s, fw, bw, fi, la, sq, rf, rb: (fw[s], col))
    bmap = lambda col: (lambda s, fw, bw, fi, la, sq, rf, rb: (bw[s], col))
    rfmap = lambda s, fw, bw, fi, la, sq, rf, rb: (rf[s], 0)
    rbmap = lambda s, fw, bw, fi, la, sq, rf, rb: (rb[s], 0)
    st_spec = pl.BlockSpec((None,) + st_shape, lambda s, fw, bw, fi, la, sq, rf, rb: (sq[s], 0, 0, 0, 0))
    blk = lambda m, col: pl.BlockSpec((q, w), m(col))
    rope = lambda m: pl.BlockSpec((q, dim), m)
    grid_spec = pltpu.PrefetchScalarGridSpec(
        num_scalar_prefetch=7,
        grid=(n_steps,),
        in_specs=[
            blk(fmap, COL_R_Q), blk(fmap, COL_R_K), blk(fmap, COL_R_V), rope(rfmap), rope(rfmap),
            blk(bmap, COL_R_Q), blk(bmap, COL_R_K), blk(bmap, COL_R_V), rope(rbmap), rope(rbmap),
            pl.BlockSpec((N_DIR * RET_HEADS, LANES), lambda s, *_: (0, 0)), st_spec,
        ],
        out_specs=[pl.BlockSpec((q, w), fmap(0)), pl.BlockSpec((q, w), bmap(0)), st_spec],
        scratch_shapes=[pltpu.VMEM(st_shape, F32)],
    )
    n_seq = cfg["n_dec"] + cfg["n_ctx"]
    cos2, sin2 = rope_tabs
    return pl.pallas_call(
        _ret_kernel,
        out_shape=[
            jax.ShapeDtypeStruct((t, w), F32), jax.ShapeDtypeStruct((t, w), F32),
            jax.ShapeDtypeStruct((n_seq,) + st_shape, F32),
        ],
        grid_spec=grid_spec,
        compiler_params=_params("arbitrary"),
        name="retention_scan",
    )(*tables, *rope_blocks, proj, proj, proj, cos2, sin2, proj, proj, proj, cos2, sin2, dec, s0)


def _layer_norm_rows(y, g, b):
    mu = jnp.mean(y, axis=-1, keepdims=True)
    yc = y - mu
    var = jnp.mean(yc * yc, axis=-1, keepdims=True)
    return yc * lax.rsqrt(var + LN_EPS) * g + b


def _outproj_kernel(
    hys_ref, hyp_ref, sf_ref, sb_ref, sz_ref, gf_ref, gb_ref, gz_ref, rf_ref, rb_ref, rg_ref,
    snw_ref, gnw_ref, w_ref, xs_ref, xp_ref, g1_ref, sh2_ref, sc2_ref, lng_ref, lnb_ref,
    x1_ref, u2_ref, *, n_sample_tiles, alpha, two_group_x,
):
    i = pl.program_id(0)
    is_sample = i < n_sample_tiles
    w = GROUP_WIDTH
    y_hy = jnp.where(is_sample, hys_ref[...], hyp_ref[...])
    y_ssd = (sf_ref[...] + sb_ref[...]) * _silu(sz_ref[...])
    y_ssd = y_ssd * lax.rsqrt(jnp.mean(y_ssd * y_ssd, axis=-1, keepdims=True) + RMS_EPS) * snw_ref[...]
    og = gf_ref[...] + gb_ref[...]
    orr = rf_ref[...] + rb_ref[...]
    gz = _silu(gz_ref[...])
    rg = _silu(rg_ref[...])
    gnw = gnw_ref[...]
    acc = _dot(y_hy, w_ref[0:w, :]) + _dot(y_ssd, w_ref[w:2 * w, :])
    for h in range(GDN_HEADS):
        sl = slice(h * GDN_HEAD_DIM, (h + 1) * GDN_HEAD_DIM)
        o = og[:, sl]
        o = o * lax.rsqrt(jnp.mean(o * o, axis=-1, keepdims=True) + RMS_EPS) * gnw * gz[:, sl]
        acc += _dot(o, w_ref[2 * w + h * GDN_HEAD_DIM:2 * w + (h + 1) * GDN_HEAD_DIM, :])
    for h in range(RET_HEADS):
        sl = slice(h * RET_HEAD_DIM, (h + 1) * RET_HEAD_DIM)
        o = orr[:, sl]
        mu = jnp.mean(o, axis=-1, keepdims=True)
        oc = o - mu
        o = oc * lax.rsqrt(jnp.mean(oc * oc, axis=-1, keepdims=True) + LN_EPS) * rg[:, sl]
        acc += _dot(o, w_ref[3 * w + h * RET_HEAD_DIM:3 * w + (h + 1) * RET_HEAD_DIM, :])
    x = jnp.where(is_sample, xs_ref[...], xp_ref[...]) if two_group_x else xs_ref[...]
    x1 = _layer_norm_rows(alpha * x + g1_ref[...] * acc, lng_ref[...], lnb_ref[...])
    x1_ref[...] = x1
    u2_ref[...] = (x1 * (1.0 + sc2_ref[...]) + sh2_ref[...]).astype(BF16)


def _outproj_call(hy_s, hy_p, ssd, gdn, ret, proj, ssd_nw, gdn_nw, w_out, x_in, mod4, layer, ln_g, ln_b, cfg):
    t, d, w = cfg["t"], cfg["d"], GROUP_WIDTH
    tm = 256
    ns = cfg["ts"] // tm
    two_group_x = isinstance(x_in, tuple)
    row = lambda col: pl.BlockSpec((tm, w), lambda i: (i, col))
    vec = lambda n: pl.BlockSpec((None, 1, n), lambda i: (layer, 0, 0))
    mspec = lambda k: _mod_spec(layer, k, d, tm, cfg["ls"], cfg["n_dec"])
    if two_group_x:
        x_specs = _two_group_specs(tm, d, ns)
        x_args = list(x_in)
    else:
        x_specs = [pl.BlockSpec((tm, d), lambda i: (i, 0)), pl.BlockSpec((SUBLANES, d), lambda i: (0, 0))]
        x_args = [x_in, x_in]
    kern = functools.partial(_outproj_kernel, n_sample_tiles=ns, alpha=cfg["alpha"], two_group_x=two_group_x)
    return pl.pallas_call(
        kern,
        out_shape=[jax.ShapeDtypeStruct((t, d), F32), jax.ShapeDtypeStruct((t, d), BF16)],
        grid=(t // tm,),
        in_specs=_two_group_specs(tm, w, ns)
        + [row(0), row(0), row(COL_S_Z), row(0), row(0), row(COL_G_Z), row(0), row(0), row(COL_R_G)]
        + [vec(w), vec(GDN_HEAD_DIM), pl.BlockSpec((None, d, d), lambda i: (layer, 0, 0))]
        + x_specs
        + [mspec(2), mspec(3), mspec(4), vec(d), vec(d)],
        out_specs=[pl.BlockSpec((tm, d), lambda i: (i, 0)), pl.BlockSpec((tm, d), lambda i: (i, 0))],
        compiler_params=_params("arbitrary"),
        name="out_proj_ln",
    )(
        hy_s, hy_p, ssd[0], ssd[1], proj, gdn[0], gdn[1], proj, ret[0], ret[1], proj,
        ssd_nw, gdn_nw, w_out, *x_args, mod4, mod4, mod4, ln_g, ln_b,
    )


def _ffn_up_kernel(u_ref, wg_ref, wv_ref, cwg_ref, cbg_ref, cwv_ref, cbv_ref, o_ref, w_scr, *, n_sample_tiles, ls, lp):
    i = pl.program_id(1)
    tn = wg_ref.shape[1]
    seq_len = jnp.where(i < n_sample_tiles, ls, lp)

    @pl.when(i == 0)
    def _():
        w_scr[:, :tn] = wg_ref[...].astype(BF16)
        w_scr[:, tn:] = wv_ref[...].astype(BF16)

    cw = jnp.concatenate([cwg_ref[...], cwv_ref[...]], axis=-1)
    cb = jnp.concatenate([cbg_ref[...], cbv_ref[...]], axis=-1)

    def finish(rows, conv):
        o_ref[rows, :] = (_silu(conv[:, :tn]) * conv[:, tn:]).astype(BF16)

    _matmul_conv_chunks(u_ref, w_scr[...], cw, cb, seq_len, min(ls, lp), finish)


def _ffn_up_call(u2, w_up, cw, cb, layer, cfg):
    t, d = u2.shape
    dff = w_up.shape[2] // 2
    tm, tn = cfg["ls"], 256
    nj = dff // tn
    kern = functools.partial(_ffn_up_kernel, n_sample_tiles=cfg["ts"] // tm, ls=cfg["ls"], lp=cfg["lp"])
    gcol = lambda j, i: (layer, 0, j)
    vcol = lambda j, i: (layer, 0, j + nj)
    return pl.pallas_call(
        kern,
        out_shape=jax.ShapeDtypeStruct((t, dff), BF16),
        grid=(nj, t // tm),
        in_specs=[
            pl.BlockSpec((tm, d), lambda j, i: (i, 0)),
            pl.BlockSpec((None, d, tn), gcol), pl.BlockSpec((None, d, tn), vcol),
            pl.BlockSpec((None, CONV_W, tn), gcol), pl.BlockSpec((None, 1, tn), gcol),
            pl.BlockSpec((None, CONV_W, tn), vcol), pl.BlockSpec((None, 1, tn), vcol),
        ],
        out_specs=pl.BlockSpec((tm, tn), lambda j, i: (i, j)),
        scratch_shapes=[pltpu.VMEM((d, 2 * tn), BF16)],
        compiler_params=_params("arbitrary", "arbitrary"),
        name="ffn_up_conv_glu",
    )(u2, w_up, w_up, cw, cb, cw, cb)


def _ffn_down_kernel(a_ref, w_ref, x1_ref, g2_ref, shn_ref, scn_ref, lng_ref, lnb_ref, oa_ref, ob_ref, acc_scr,
                     *, alpha, n_sample_tiles, last_layer):
    i = pl.program_id(0)
    k = pl.program_id(1)

    @pl.when(k == 0)
    def _():
        acc_scr[...] = jnp.zeros_like(acc_scr)

    acc_scr[...] += jnp.dot(a_ref[...], w_ref[...], preferred_element_type=F32)

    @pl.when(k == pl.num_programs(1) - 1)
    def _():
        x2 = _layer_norm_rows(alpha * x1_ref[...] + g2_ref[...] * acc_scr[...], lng_ref[...], lnb_ref[...])
        if last_layer:
            @pl.when(i < n_sample_tiles)
            def _():
                oa_ref[...] = x2

            @pl.when(i >= n_sample_tiles)
            def _():
                ob_ref[...] = x2
        else:
            oa_ref[...] = x2
            ob_ref[...] = (x2 * (1.0 + scn_ref[...]) + shn_ref[...]).astype(BF16)


def _ffn_down_call(act, w_down, x1, mod4, layer, ln_g, ln_b, cfg):
    t, dff = act.shape
    d = cfg["d"]
    depth = w_down.shape[0]
    last_layer = layer == depth - 1
    next_layer = min(layer + 1, depth - 1)
    tm = 512
    tk = dff // 4 if (dff // 4) % LANES == 0 else dff
    ns = cfg["ts"] // tm
    vec = lambda: pl.BlockSpec((None, 1, d), lambda i, k: (layer, 0, 0))
    mspec = lambda lay, kk: _mod_spec(lay, kk, d, tm, cfg["ls"], cfg["n_dec"])
    if last_layer:
        out_shape = [jax.ShapeDtypeStruct((cfg["ts"], d), F32), jax.ShapeDtypeStruct((cfg["tp"], d), F32)]
        out_specs = [
            pl.BlockSpec((tm, d), lambda i, k: (jnp.minimum(i, ns - 1), 0)),
            pl.BlockSpec((tm, d), lambda i, k: (jnp.maximum(i - ns, 0), 0)),
        ]
    else:
        out_shape = [jax.ShapeDtypeStruct((t, d), F32), jax.ShapeDtypeStruct((t, d), BF16)]
        out_specs = [pl.BlockSpec((tm, d), lambda i, k: (i, 0)), pl.BlockSpec((tm, d), lambda i, k: (i, 0))]
    return pl.pallas_call(
        functools.partial(_ffn_down_kernel, alpha=cfg["alpha"], n_sample_tiles=ns, last_layer=last_layer),
        out_shape=out_shape,
        grid=(t // tm, dff // tk),
        in_specs=[
            pl.BlockSpec((tm, tk), lambda i, k: (i, k)),
            pl.BlockSpec((None, tk, d), lambda i, k: (layer, k, 0)),
            pl.BlockSpec((tm, d), lambda i, k: (i, 0)),
            mspec(layer, 5), mspec(next_layer, 0), mspec(next_layer, 1), vec(), vec(),
        ],
        out_specs=out_specs,
        scratch_shapes=[pltpu.VMEM((tm, d), F32)],
        compiler_params=_params("arbitrary", "arbitrary"),
        name="ffn_down_ln",
    )(act, w_down, x1, mod4, mod4, mod4, ln_g, ln_b)


def _reorder_w_in(w_in):
    w = GROUP_WIDTH
    o = N_CONV
    s_z = w_in[..., o:o + w]
    o += w
    s_dt = w_in[..., o:o + N_DIR * SSD_HEADS]
    o += N_DIR * SSD_HEADS
    g_z = w_in[..., o:o + w]
    o += w
    g_beta = w_in[..., o:o + N_DIR * GDN_HEADS]
    o += N_DIR * GDN_HEADS
    g_a = w_in[..., o:o + N_DIR * GDN_HEADS]
    o += N_DIR * GDN_HEADS
    rest = w_in[..., o:]
    main = jnp.concatenate([w_in[..., :N_CONV], s_z, g_z, rest], axis=-1).astype(BF16)
    small = jnp.concatenate([s_dt, g_beta, g_a], axis=-1)
    small = jnp.pad(small, ((0, 0), (0, 0), (0, LANES - small.shape[-1]))).astype(BF16)
    return main, small


def kernel(x_prompt, x_sample, state_ssd, state_gdn, state_ret, c, c_ctx, w_mod, b_mod, w_in, conv_w, conv_b,
           hy_w1, hy_b1, hy_w2, hy_b2, hy_w3, hy_freq, hy_bias, ssd_A_log, ssd_dt_bias, ssd_D, ssd_norm_w,
           gdn_A_log, gdn_dt_bias, gdn_norm_w, ret_decay, w_out, ln1_g, ln1_b, w_up, ffn_conv_w, ffn_conv_b,
           w_down, ln2_g, ln2_b):
    n_ctx, lp, d = x_prompt.shape
    n_dec, ls, _ = x_sample.shape
    depth = w_in.shape[0]
    ts, tp = n_dec * ls, n_ctx * lp
    assert ls & (ls - 1) == 0 and lp & (lp - 1) == 0 and tp % ls == 0 and ls % lp == 0
    assert d == 4 * GROUP_WIDTH and w_in.shape[2] == N_MAIN + 2 * N_DIR * (SSD_HEADS // 2 + GDN_HEADS)
    cfg = dict(d=d, ls=ls, lp=lp, n_dec=n_dec, n_ctx=n_ctx, ts=ts, tp=tp, t=ts + tp, alpha=(2 * depth) ** 0.25)

    xs = x_sample.reshape(ts, d)
    xp = x_prompt.reshape(tp, d)

    mod_rows = -(-(n_dec + 1) // SUBLANES) * SUBLANES
    cond = jnp.concatenate([c, c_ctx[None, :], jnp.zeros((mod_rows - n_dec - 1, d), F32)], axis=0)
    mod = _mod_call(cond, w_mod, b_mod)
    mod4 = mod.reshape(depth, mod_rows, 1, 6 * d)

    w_main, w_small = _reorder_w_in(w_in)
    w_out_b, w_down_b = w_out.astype(BF16), w_down.astype(BF16)
    stack_rows = lambda v: v[:, None, :]

    dft = {l: tuple(jnp.asarray(tab).astype(BF16) for tab in _dft_tables(l)) for l in (ls, lp)}
    rope_tabs = _rope_tables(cfg)
    rope_blocks = _rope_block_table(cfg)
    tabs = {q: _chunk_tables(cfg, q) for q in sorted({SSD_CHUNK, GDN_CHUNK, RET_CHUNK})}

    def init_state(st, l):
        zeros = jnp.zeros((n_ctx,) + st.shape[2:], F32)
        return jnp.concatenate([st[:, l], zeros], axis=0)

    u = _modulate_call(xs, xp, mod4, 0, cfg)
    x_res = (xs, xp)
    new_ssd, new_gdn, new_ret = [], [], []
    for l in range(depth):
        proj, small = _inproj_call(u, w_main, w_small, conv_w, stack_rows(conv_b), l, cfg)

        hy = {}
        for name, seq_len, n_seq, row0 in (("s", ls, n_dec, 0), ("p", lp, n_ctx, ts)):
            a_tab, b_tab = dft[seq_len]
            prs = _hy_filter_call(seq_len, a_tab, hy_w1[l], hy_b1[l], hy_w2[l], hy_b2[l], hy_w3[l], hy_freq[l])
            z1 = _hy_conv_call(proj, COL_HY_V, row0, proj, COL_HY_X1, row0, hy_bias[l], a_tab, b_tab, prs, 0, seq_len, n_seq)
            hy[name] = _hy_conv_call(z1, 0, 0, proj, COL_HY_X2, row0, hy_bias[l], a_tab, b_tab, prs, 1, seq_len, n_seq)

        ssd = _ssd_call(proj, small, tabs[SSD_CHUNK], init_state(state_ssd, l), ssd_dt_bias[l], ssd_A_log[l], ssd_D[l], cfg)
        gdn = _gdn_call(proj, small, tabs[GDN_CHUNK], init_state(state_gdn, l), gdn_A_log[l], gdn_dt_bias[l], cfg)
        ret = _ret_call(proj, tabs[RET_CHUNK], rope_tabs, rope_blocks, init_state(state_ret, l), ret_decay[l], cfg)
        new_ssd.append(ssd[2][n_dec:])
        new_gdn.append(gdn[2][n_dec:])
        new_ret.append(ret[2][n_dec:])

        x1, u2 = _outproj_call(hy["s"], hy["p"], ssd, gdn, ret, proj, stack_rows(ssd_norm_w), stack_rows(gdn_norm_w),
                               w_out_b, x_res, mod4, l, stack_rows(ln1_g), stack_rows(ln1_b), cfg)
        act = _ffn_up_call(u2, w_up, ffn_conv_w, stack_rows(ffn_conv_b), l, cfg)
        x_res, u = _ffn_down_call(act, w_down_b, x1, mod4, l, stack_rows(ln2_g), stack_rows(ln2_b), cfg)

    y_sample = x_res.reshape(n_dec, ls, d)
    y_prompt = u.reshape(n_ctx, lp, d)
    return (y_prompt, y_sample, jnp.stack(new_ssd, 1), jnp.stack(new_gdn, 1), jnp.stack(new_ret, 1))
```

```python
import functools
import math

import numpy as np
import jax
import jax.numpy as jnp
from jax import lax
from jax.experimental import pallas as pl
from jax.experimental.pallas import tpu as pltpu

F32 = jnp.float32
BF16 = jnp.bfloat16
HIGHEST = lax.Precision.HIGHEST

N_DIR = 2
CONV_W = 3
GROUP_WIDTH = 512
HY_ORDER = 2
HY_BANDS = 8
HY_EMB = 1 + 2 * HY_BANDS
HY_FFN = 64
HY_DECAY_TARGET = 1e-2
HY_FAST_PCT = 0.3
HY_SLOW_PCT = 1.5
SSD_HEADS = 8
SSD_HEAD_DIM = 64
SSD_GROUPS = 2
SSD_HPG = SSD_HEADS // SSD_GROUPS
SSD_STATE = 128
SSD_CHUNK = 128
GDN_HEADS = 4
GDN_HEAD_DIM = 128
GDN_CHUNK = 64
RET_HEADS = 4
RET_HEAD_DIM = 128
RET_CHUNK = 128
GRID_W = 64
ROPE_BASE = 10000.0
LN_EPS = 1e-5
RMS_EPS = 1e-6
N_CONV = 8 * GROUP_WIDTH

LANES = 128
SUBLANES = 8
VMEM_LIMIT_BYTES = 60 * 1024 * 1024

COL_HY_V, COL_HY_X1, COL_HY_X2, COL_S_X, COL_S_BC, COL_G_Q, COL_G_K, COL_G_V = range(8)
COL_S_Z, COL_G_Z, COL_R_Q, COL_R_K, COL_R_V, COL_R_G = range(8, 14)
N_MAIN = 14 * GROUP_WIDTH
SM_SDT = 0
SM_GBETA = 16
SM_GA = 24


def _silu(x):
    return x * jax.nn.sigmoid(x)


def _softplus(x):
    return jnp.maximum(x, 0.0) + jnp.log1p(jnp.exp(-jnp.abs(x)))


def _dot(a, b):
    return jnp.dot(a.astype(BF16), b.astype(BF16), preferred_element_type=F32)


def _dot_nt(a, b):
    return lax.dot_general(a.astype(BF16), b.astype(BF16), (((1,), (1,)), ((), ())), preferred_element_type=F32)


def _dot_tn(a, b):
    return lax.dot_general(a.astype(BF16), b.astype(BF16), (((0,), (0,)), ((), ())), preferred_element_type=F32)


def _dot_hi(a, b):
    return jnp.dot(a, b, preferred_element_type=F32, precision=HIGHEST)


def _params(*semantics):
    return pltpu.CompilerParams(dimension_semantics=semantics, vmem_limit_bytes=VMEM_LIMIT_BYTES)


def _tri(n, lower):
    r = lax.broadcasted_iota(jnp.int32, (n, n), 0)
    c = lax.broadcasted_iota(jnp.int32, (n, n), 1)
    return jnp.where((r >= c) if lower else (r <= c), 1.0, 0.0).astype(F32)


def _mod_kernel(c_ref, w_ref, b_ref, o_ref):
    a = _silu(c_ref[...])
    o_ref[...] = _dot(a, w_ref[...]) + b_ref[...]


def _mod_call(cond, w_mod, b_mod):
    depth, d, n = w_mod.shape
    rows = cond.shape[0]
    tn = 1024
    return pl.pallas_call(
        _mod_kernel,
        out_shape=jax.ShapeDtypeStruct((depth, rows, n), F32),
        grid=(depth, n // tn),
        in_specs=[
            pl.BlockSpec((rows, d), lambda l, j: (0, 0)),
            pl.BlockSpec((None, d, tn), lambda l, j: (l, 0, j)),
            pl.BlockSpec((None, 1, tn), lambda l, j: (l, 0, j)),
        ],
        out_specs=pl.BlockSpec((None, rows, tn), lambda l, j: (l, 0, j)),
        compiler_params=_params("arbitrary", "arbitrary"),
        name="adaln_mod",
    )(cond, w_mod, b_mod.reshape(depth, 1, n))


def _mod_spec(layer, k, d, rows_per_tile, l_sample, n_dec):
    tiles_per_seq = l_sample // rows_per_tile

    def index_map(i, *_):
        return (layer, jnp.minimum(i // tiles_per_seq, n_dec), 0, k)

    return pl.BlockSpec((None, None, 1, d), index_map)


def _modulate_kernel(xs_ref, xp_ref, sh_ref, sc_ref, o_ref, *, n_sample_tiles):
    i = pl.program_id(0)
    x = jnp.where(i < n_sample_tiles, xs_ref[...], xp_ref[...])
    o_ref[...] = (x * (1.0 + sc_ref[...]) + sh_ref[...]).astype(BF16)


def _two_group_specs(tm, d, n_sample_tiles):
    return [
        pl.BlockSpec((tm, d), lambda i, *_: (jnp.minimum(i, n_sample_tiles - 1), 0)),
        pl.BlockSpec((tm, d), lambda i, *_: (jnp.maximum(i - n_sample_tiles, 0), 0)),
    ]


def _modulate_call(xs, xp, mod4, layer, cfg):
    tm, d = 256, cfg["d"]
    t = cfg["t"]
    ns = cfg["ts"] // tm
    return pl.pallas_call(
        functools.partial(_modulate_kernel, n_sample_tiles=ns),
        out_shape=jax.ShapeDtypeStruct((t, d), BF16),
        grid=(t // tm,),
        in_specs=_two_group_specs(tm, d, ns)
        + [_mod_spec(layer, 0, d, tm, cfg["ls"], cfg["n_dec"]), _mod_spec(layer, 1, d, tm, cfg["ls"], cfg["n_dec"])],
        out_specs=pl.BlockSpec((tm, d), lambda i: (i, 0)),
        compiler_params=_params("arbitrary"),
        name="modulate_in",
    )(xs, xp, mod4, mod4)


CONV_ROW_CHUNK = 256


def _matmul_conv_chunks(u_ref, w, cw, cb, h_scr, seq_len, min_seq_len, finish):
    rows = u_ref.shape[0]
    rc = min(CONV_ROW_CHUNK, rows)
    assert rows % rc == 0 and rc % min_seq_len == 0
    n_chunks = rows // rc
    n = w.shape[1]
    halo = SUBLANES
    first = lax.broadcasted_iota(jnp.int32, (SUBLANES, 1), 0) == 0
    last = lax.broadcasted_iota(jnp.int32, (SUBLANES, 1), 0) == SUBLANES - 1
    tile = lambda v, t: v[t * SUBLANES:(t + 1) * SUBLANES]
    h_scr[0:halo, :] = jnp.zeros((halo, n), F32)
    h_scr[halo + rows:2 * halo + rows, :] = jnp.zeros((halo, n), F32)
    for r in range(n_chunks + 1):
        if r < n_chunks:
            h_scr[halo + r * rc:halo + (r + 1) * rc, :] = jnp.dot(u_ref[r * rc:(r + 1) * rc, :], w, preferred_element_type=F32)
        if r == 0:
            continue
        row0 = (r - 1) * rc
        h = h_scr[halo + row0:halo + row0 + rc, :]
        hp = h_scr[halo + row0 - 1:halo + row0 - 1 + rc, :]
        hn = h_scr[halo + row0 + 1:halo + row0 + 1 + rc, :]
        fix_p, fix_n = {}, {}
        for b in range(0, rc, min_seq_len):
            seq_start = ((row0 + b) & (seq_len - 1)) == 0
            seq_end = ((row0 + b + min_seq_len) & (seq_len - 1)) == 0
            tp, tn_ = b // SUBLANES, (b + min_seq_len) // SUBLANES - 1
            fix_p[tp] = jnp.where(jnp.logical_and(first, seq_start), 0.0, tile(hp, tp))
            fix_n[tn_] = jnp.where(jnp.logical_and(last, seq_end), 0.0, tile(hn, tn_))
        n_tiles = rc // SUBLANES
        hp = jnp.concatenate([fix_p.get(t, tile(hp, t)) for t in range(n_tiles)], axis=0)
        hn = jnp.concatenate([fix_n.get(t, tile(hn, t)) for t in range(n_tiles)], axis=0)
        finish(slice(row0, row0 + rc), hp * cw[0:1] + h * cw[1:2] + hn * cw[2:3] + cb)


def _inproj_kernel(u_ref, w_ref, ws_ref, cw_ref, cb_ref, o_ref, os_ref, h_scr, *, n_conv_tiles, n_sample_tiles, ls, lp):
    i = pl.program_id(0)
    j = pl.program_id(1)
    seq_len = jnp.where(i < n_sample_tiles, ls, lp)

    @pl.when(j < n_conv_tiles)
    def _():
        def finish(rows, conv):
            o_ref[rows, :] = conv

        _matmul_conv_chunks(u_ref, w_ref[...], cw_ref[...], cb_ref[...], h_scr, seq_len, min(ls, lp), finish)

    @pl.when(j >= n_conv_tiles)
    def _():
        o_ref[...] = jnp.dot(u_ref[...], w_ref[...], preferred_element_type=F32)

    @pl.when(j == 0)
    def _():
        os_ref[...] = jnp.dot(u_ref[...], ws_ref[...], preferred_element_type=F32)


def _inproj_call(u, w_main, w_small, cw, cb, layer, cfg):
    t, d = u.shape
    tm, tn = cfg["ls"], 512
    n_conv_tiles = N_CONV // tn
    kern = functools.partial(
        _inproj_kernel, n_conv_tiles=n_conv_tiles, n_sample_tiles=cfg["ts"] // tm, ls=cfg["ls"], lp=cfg["lp"]
    )
    conv_col = lambda i, j: (layer, 0, jnp.minimum(j, n_conv_tiles - 1))
    return pl.pallas_call(
        kern,
        out_shape=[jax.ShapeDtypeStruct((t, N_MAIN), F32), jax.ShapeDtypeStruct((t, LANES), F32)],
        grid=(t // tm, N_MAIN // tn),
        in_specs=[
            pl.BlockSpec((tm, d), lambda i, j: (i, 0)),
            pl.BlockSpec((None, d, tn), lambda i, j: (layer, 0, j)),
            pl.BlockSpec((None, d, LANES), lambda i, j: (layer, 0, 0)),
            pl.BlockSpec((None, CONV_W, tn), conv_col),
            pl.BlockSpec((None, 1, tn), conv_col),
        ],
        out_specs=[pl.BlockSpec((tm, tn), lambda i, j: (i, j)), pl.BlockSpec((tm, LANES), lambda i, j: (i, 0))],
        scratch_shapes=[pltpu.VMEM((tm + 2 * SUBLANES, tn), F32)],
        compiler_params=_params("arbitrary", "arbitrary"),
        name="in_proj_conv",
    )(u, w_main, w_small, cw, cb)


@functools.lru_cache(maxsize=None)
def _dft_tables(l):
    n = 2 * l
    k = np.arange(l, dtype=np.int64)
    ang = (2.0 * math.pi / n) * ((k[:, None] * k[None, :]) % n)
    cos, sin = np.cos(ang), np.sin(ang)
    alt = np.where(k % 2 == 0, 1.0, -1.0)
    a_im = np.where(k[:, None] == 0, alt[None, :], -sin)
    a = np.concatenate([cos, a_im], axis=0)
    ck = np.where(k == 0, 1.0, 2.0) / n
    b_re = cos * ck[None, :]
    b_im = np.where(k[None, :] == 0, alt[:, None] / n, -sin * (2.0 / n))
    b = np.concatenate([b_re, b_im], axis=1)
    return a.astype(np.float32), b.astype(np.float32)


@functools.lru_cache(maxsize=None)
def _hy_feats(l):
    pos = np.arange(l, dtype=np.float32)
    t = pos / np.float32(l - 1)
    bands = np.linspace(1e-4, HY_BANDS - 1, HY_BANDS, dtype=np.float32)
    ang = np.float32(2.0 * math.pi / l) * pos[:, None] * bands
    feats = np.concatenate([t[:, None], np.cos(ang), -np.sin(ang)], -1).astype(np.float32)
    feats = np.pad(feats, ((0, 0), (0, LANES - HY_EMB)))
    return feats, t[:, None]


@functools.lru_cache(maxsize=None)
def _hy_deltas(c):
    lo, hi = math.log(HY_DECAY_TARGET) / HY_SLOW_PCT, math.log(HY_DECAY_TARGET) / HY_FAST_PCT
    return np.abs(np.linspace(lo, hi, c, dtype=np.float32))[None, :]


def _hy_filter_kernel(
    feats_ref, t_ref, dl_ref, w1_ref, b1_ref, w2_ref, b2_ref, w3_ref, fq_ref, are_ref, aim_ref,
    p_ref, r_ref, s_ref, sum_scr, dif_scr, nyq_scr,
):
    f = pl.program_id(0)
    l = feats_ref.shape[0]
    c = dl_ref.shape[1]
    kb = are_ref.shape[0]

    @pl.when(f == 0)
    def _():
        fq = fq_ref[...]
        h = jnp.sin(fq[0:1] * (_dot_hi(feats_ref[...], w1_ref[...]) + b1_ref[...]))
        h = jnp.sin(fq[1:2] * (_dot_hi(h, w2_ref[...]) + b2_ref[...]))
        decay = jnp.exp(-t_ref[...] * dl_ref[...])
        row = lax.broadcasted_iota(jnp.int32, (l, 1), 0)
        sign = jnp.where((row & 1) == 0, 1.0, -1.0)
        for o in range(HY_ORDER):
            fwd = _dot_hi(h, w3_ref[:, (2 * o) * c:(2 * o + 1) * c]) * decay
            bwd = _dot_hi(h, w3_ref[:, (2 * o + 1) * c:(2 * o + 2) * c]) * decay
            bwd = jnp.where(row == 0, 0.0, bwd)
            norm = jnp.sum(jnp.abs(fwd), axis=0, keepdims=True) + jnp.sum(jnp.abs(bwd), axis=0, keepdims=True)
            fwd = fwd / norm
            bwd = bwd / norm
            ssum = fwd + bwd
            sum_scr[o] = ssum.astype(BF16)
            dif_scr[o] = (fwd - bwd).astype(BF16)
            nyq_scr[o] = jnp.broadcast_to(jnp.sum(ssum * sign, axis=0, keepdims=True), (SUBLANES, c))

    grow = f * kb + lax.broadcasted_iota(jnp.int32, (kb, 1), 0)
    for o in range(HY_ORDER):
        kre = jnp.dot(are_ref[...], sum_scr[o], preferred_element_type=F32)
        kim = jnp.dot(aim_ref[...], dif_scr[o], preferred_element_type=F32)
        p_ref[o] = kre
        r_ref[o] = jnp.where(grow == 0, 0.0, kim)
        s_ref[o] = jnp.where(grow == 0, nyq_scr[o][0:1], kre)


def _hy_filter_call(l, a_tab, w1, b1, w2, b2, w3, freq):
    c = GROUP_WIDTH
    kb = min(l, 256)
    nf = l // kb
    feats, t = _hy_feats(l)
    w1p = jnp.pad(w1, ((0, LANES - HY_EMB), (0, 0)))
    full = lambda shape: pl.BlockSpec(shape, lambda f: tuple(0 for _ in shape))
    out = jax.ShapeDtypeStruct((HY_ORDER, l, c), F32)
    out_spec = pl.BlockSpec((HY_ORDER, kb, c), lambda f: (0, f, 0))
    return pl.pallas_call(
        _hy_filter_kernel,
        out_shape=[out, out, out],
        grid=(nf,),
        in_specs=[
            full((l, LANES)), full((l, 1)), full((1, c)), full((LANES, HY_FFN)), full((1, HY_FFN)),
            full((HY_FFN, HY_FFN)), full((1, HY_FFN)), full((HY_FFN, 2 * HY_ORDER * c)), full((2, HY_FFN)),
            pl.BlockSpec((kb, l), lambda f: (f, 0)),
            pl.BlockSpec((kb, l), lambda f: (f + nf, 0)),
        ],
        out_specs=[out_spec, out_spec, out_spec],
        scratch_shapes=[
            pltpu.VMEM((HY_ORDER, l, c), BF16), pltpu.VMEM((HY_ORDER, l, c), BF16),
            pltpu.VMEM((HY_ORDER, SUBLANES, c), F32),
        ],
        compiler_params=_params("arbitrary"),
        name="hyena_filter",
    )(feats, t, _hy_deltas(c), w1p, b1[None, :], w2, b2[None, :], w3, freq, a_tab, a_tab)


def _hy_conv_kernel(z_ref, g_ref, bias_ref, are_ref, aim_ref, bre_ref, bim_ref, p_ref, r_ref, s_ref, o_ref, zb_scr):
    f = pl.program_id(2)

    @pl.when(f == 0)
    def _():
        zb_scr[...] = z_ref[...].astype(BF16)
        o_ref[...] = jnp.zeros_like(o_ref)

    zb = zb_scr[...]
    re = jnp.dot(are_ref[...], zb, preferred_element_type=F32)
    im = jnp.dot(aim_ref[...], zb, preferred_element_type=F32)
    p, r, s = p_ref[...], r_ref[...], s_ref[...]
    re2 = re * p - im * r
    im2 = re * r + im * s
    o_ref[...] += _dot(bre_ref[...], re2) + _dot(bim_ref[...], im2)

    @pl.when(f == pl.num_programs(2) - 1)
    def _():
        o_ref[...] = g_ref[...] * (o_ref[...] + z_ref[...] * bias_ref[...])


def _hy_conv_call(z, z_col, z_row0, g, g_col, g_row0, bias, a_tab, b_tab, prs, order, l, n_seq):
    c = GROUP_WIDTH
    ct = c
    ncb = c // ct
    kb = min(l, 512)
    nf = l // kb
    p, r, s = prs
    zspec = pl.BlockSpec((l, ct), lambda b, cb, f: (z_row0 // l + b, z_col * ncb + cb))
    gspec = pl.BlockSpec((l, ct), lambda b, cb, f: (g_row0 // l + b, g_col * ncb + cb))
    kspec = pl.BlockSpec((None, kb, ct), lambda b, cb, f: (order, f, cb))
    return pl.pallas_call(
        _hy_conv_kernel,
        out_shape=jax.ShapeDtypeStruct((n_seq * l, c), F32),
        grid=(n_seq, ncb, nf),
        in_specs=[
            zspec, gspec,
            pl.BlockSpec((None, 1, ct), lambda b, cb, f: (order, 0, cb)),
            pl.BlockSpec((kb, l), lambda b, cb, f: (f, 0)),
            pl.BlockSpec((kb, l), lambda b, cb, f: (f + nf, 0)),
            pl.BlockSpec((l, kb), lambda b, cb, f: (0, f)),
            pl.BlockSpec((l, kb), lambda b, cb, f: (0, f + nf)),
            kspec, kspec, kspec,
        ],
        out_specs=pl.BlockSpec((l, ct), lambda b, cb, f: (b, cb)),
        scratch_shapes=[pltpu.VMEM((l, ct), BF16)],
        compiler_params=_params("arbitrary", "arbitrary", "arbitrary"),
        name="hyena_conv",
    )(z, g, bias.reshape(HY_ORDER, 1, c), a_tab, a_tab, b_tab, b_tab, p, r, s)


CHUNKS_PER_STEP = 2
GDN_CHUNKS_PER_STEP = 4


def _scan_order(d, n=CHUNKS_PER_STEP):
    return list(range(n)) if d == 0 else list(range(n - 1, -1, -1))


def _chunk_tables(cfg, q):
    fwd, bwd, first, last, seq = [], [], [], [], []
    sid = 0
    for n_seq, l, row0 in ((cfg["n_dec"], cfg["ls"], 0), (cfg["n_ctx"], cfg["lp"], cfg["ts"])):
        nc = l // q
        for b in range(n_seq):
            base = (row0 + b * l) // q
            for c in range(nc):
                fwd.append(base + c)
                bwd.append(base + nc - 1 - c)
                first.append(int(c == 0))
                last.append(int(c == nc - 1))
                seq.append(sid)
            sid += 1
    return tuple(jnp.asarray(np.asarray(a, np.int32)) for a in (fwd, bwd, first, last, seq))


def _scan_masks(q, d):
    r = lax.broadcasted_iota(jnp.int32, (q, q), 0)
    c = lax.broadcasted_iota(jnp.int32, (q, q), 1)
    return ((r >= c), (r > c)) if d == 0 else ((r <= c), (r < c))


def _cumsum_pair(a_col, a_row, q, d):
    lo, up = _tri(q, True), _tri(q, False)
    if d == 0:
        return _dot_hi(lo, a_col), _dot_hi(a_row, up)
    return _dot_hi(up, a_col), _dot_hi(a_row, lo)


def _ssd_kernel(
    fwd_t, bwd_t, first_t, last_t, seq_t,
    xf_ref, bcf_ref, smf_ref, xb_ref, bcb_ref, smb_ref, pr_ref, pc_ref, dsk_ref, h0_ref,
    yf_ref, yb_ref, hout_ref, h_scr,
):
    s = pl.program_id(0)
    q, p, n = SSD_CHUNK, SSD_HEAD_DIM, SSD_STATE

    @pl.when(first_t[s] == 1)
    def _():
        h_scr[...] = h0_ref[...]

    bias_r, alog_r = pr_ref[0:1], pr_ref[1:2]
    bias_c, alog_c = pc_ref[:, 0:1], pc_ref[:, 1:2]
    assert q == LANES == n and 2 * p == LANES
    even_half = lax.broadcasted_iota(jnp.int32, (q, LANES), 1) < p
    probs = []
    for d in range(N_DIR):
        x_ref, bc_ref, sm_ref = (xf_ref, bcf_ref, smf_ref) if d == 0 else (xb_ref, bcb_ref, smb_ref)
        incl, _ = _scan_masks(q, d)
        end = q - 1 if d == 0 else 0
        for k in _scan_order(d):
            rows = slice(k * q, (k + 1) * q)
            x = _silu(x_ref[rows, :])
            x_t = x.T
            bc = _silu(bc_ref[rows, :])
            sm = sm_ref[rows, :]
            a_c = _softplus(sm + bias_r) * (-jnp.exp(alog_r))
            dt_r = _softplus(sm.T + bias_c)
            acs_c, acs_r = _cumsum_pair(a_c, (dt_r * (-jnp.exp(alog_c)))[SM_SDT:SM_SDT + N_DIR * SSD_HEADS], q, d)
            for g in range(SSD_GROUPS):
                bm = bc[:, g * n:(g + 1) * n]
                cm = bc[:, (SSD_GROUPS + g) * n:(SSD_GROUPS + g + 1) * n]
                cb = _dot_nt(cm, bm)
                for e in range(SSD_HPG):
                    hd = g * SSD_HPG + e
                    idx = d * SSD_HEADS + hd
                    pair = hd // 2
                    row = acs_r[idx:idx + 1, :]
                    dt_row = dt_r[SM_SDT + idx:SM_SDT + idx + 1, :]
                    tot = row[:, end:end + 1]
                    probs.append(dict(
                        d=d, k=k, hd=hd, bm=bm, cm=cm, cb=cb, incl=incl, row=row, dt_row=dt_row, tot=tot,
                        colb=jnp.broadcast_to(acs_c[:, idx:idx + 1], (q, q)),
                        x_pair=x[:, pair * LANES:(pair + 1) * LANES], x_t=x_t[hd * p:(hd + 1) * p, :],
                        skip=dsk_ref[d:d + 1, pair * LANES:(pair + 1) * LANES],
                    ))
    lmat = [jnp.exp(jnp.where(pb["incl"], pb["colb"] - pb["row"], -1e30)) * pb["dt_row"] for pb in probs]
    y_diag = [_dot(pb["cb"] * lm, pb["x_pair"]) for pb, lm in zip(probs, lmat)]
    states = [_dot(pb["x_t"] * (pb["dt_row"] * jnp.exp(pb["tot"] - pb["row"])), pb["bm"]) for pb in probs]
    c_dec = [pb["cm"] * jnp.exp(pb["colb"]) for pb in probs]
    state = {(d, hd): h_scr[d, hd] for d in range(N_DIR) for hd in range(SSD_HEADS)}
    ys = {}
    for step in range(CHUNKS_PER_STEP):
        group = [(i, pb) for i, pb in enumerate(probs) if _scan_order(pb["d"])[step] == pb["k"]]
        pair_state = {
            (d, pr_): jnp.concatenate([state[d, 2 * pr_], state[d, 2 * pr_ + 1]], axis=0)
            for d in range(N_DIR) for pr_ in range(SSD_HEADS // 2)
        }
        y_off = [_dot_nt(c_dec[i], pair_state[pb["d"], pb["hd"] // 2]) for i, pb in group]
        for (i, pb), yo in zip(group, y_off):
            ys[pb["d"], pb["k"], pb["hd"]] = y_diag[i] + yo
            state[pb["d"], pb["hd"]] = state[pb["d"], pb["hd"]] * jnp.exp(pb["tot"]) + states[i]
    for (d, hd), h_new in state.items():
        h_scr[d, hd] = h_new
    by_key = {(pb["d"], pb["k"], pb["hd"]): pb for pb in probs}
    for d, y_ref in ((0, yf_ref), (1, yb_ref)):
        for k in range(CHUNKS_PER_STEP):
            for pr_ in range(SSD_HEADS // 2):
                pb = by_key[d, k, 2 * pr_]
                y_pair = jnp.where(even_half, ys[d, k, 2 * pr_], ys[d, k, 2 * pr_ + 1]) + pb["skip"] * pb["x_pair"]
                y_ref[k * q:(k + 1) * q, pr_ * LANES:(pr_ + 1) * LANES] = y_pair

    @pl.when(last_t[s] == 1)
    def _():
        hout_ref[...] = h_scr[...]


def _ssd_call(proj, small, tables, h0, dt_bias, a_log, d_skip, cfg):
    q, w = CHUNKS_PER_STEP * SSD_CHUNK, GROUP_WIDTH
    t = proj.shape[0]
    n_steps = tables[0].shape[0]
    nlane = N_DIR * SSD_HEADS
    pr = jnp.zeros((SUBLANES, LANES), F32)
    pr = pr.at[0, SM_SDT:SM_SDT + nlane].set(dt_bias.reshape(-1)).at[1, SM_SDT:SM_SDT + nlane].set(a_log.reshape(-1))
    pc = pr.T
    dsk = jnp.repeat(d_skip, SSD_HEAD_DIM, axis=-1)
    st_shape = (N_DIR, SSD_HEADS, SSD_HEAD_DIM, SSD_STATE)
    fmap = lambda col: (lambda s, fw, bw, fi, la, sq: (fw[s], col))
    bmap = lambda col: (lambda s, fw, bw, fi, la, sq: (bw[s], col))
    const = lambda shape: pl.BlockSpec(shape, lambda s, *_: tuple(0 for _ in shape))
    st_spec = pl.BlockSpec((None,) + st_shape, lambda s, fw, bw, fi, la, sq: (sq[s], 0, 0, 0, 0))
    grid_spec = pltpu.PrefetchScalarGridSpec(
        num_scalar_prefetch=5,
        grid=(n_steps,),
        in_specs=[
            pl.BlockSpec((q, w), fmap(COL_S_X)), pl.BlockSpec((q, w), fmap(COL_S_BC)), pl.BlockSpec((q, LANES), fmap(0)),
            pl.BlockSpec((q, w), bmap(COL_S_X)), pl.BlockSpec((q, w), bmap(COL_S_BC)), pl.BlockSpec((q, LANES), bmap(0)),
            const((SUBLANES, LANES)), const((LANES, SUBLANES)), const((N_DIR, w)), st_spec,
        ],
        out_specs=[pl.BlockSpec((q, w), fmap(0)), pl.BlockSpec((q, w), bmap(0)), st_spec],
        scratch_shapes=[pltpu.VMEM(st_shape, F32)],
    )
    n_seq = cfg["n_dec"] + cfg["n_ctx"]
    return pl.pallas_call(
        _ssd_kernel,
        out_shape=[
            jax.ShapeDtypeStruct((t, w), F32), jax.ShapeDtypeStruct((t, w), F32),
            jax.ShapeDtypeStruct((n_seq,) + st_shape, F32),
        ],
        grid_spec=grid_spec,
        compiler_params=_params("arbitrary"),
        name="ssd_scan",
    )(*tables, proj, proj, small, proj, proj, small, pr, pc, dsk, h0)


TRI_BLOCK = 2 * SUBLANES


def _unit_tri_inverses(mats, lowers):
    n = mats[0].shape[0]
    blk = TRI_BLOCK
    sub = blk // SUBLANES
    lane = lax.broadcasted_iota(jnp.int32, (blk, n), 1)
    row = lax.broadcasted_iota(jnp.int32, (blk, n), 0)
    group = (lax.broadcasted_iota(jnp.int32, (SUBLANES, LANES), 1) // blk) * blk
    tiles = lambda v: [v[t * SUBLANES:(t + 1) * SUBLANES, :] for t in range(sub)]
    eye = jnp.where(lane % blk == row, 1.0, 0.0)
    d_t, inv_t = [], []
    for a in mats:
        packed = jnp.zeros((blk, n), F32)
        for bi in range(n // blk):
            packed = jnp.where(lane // blk == bi, a[bi * blk:(bi + 1) * blk, :], packed)
        if n < LANES:
            packed = jnp.concatenate([packed, jnp.zeros((blk, LANES - n), F32)], axis=1)
        d_t.append(tiles(packed))
        inv_t.append(tiles(eye))
    for step in range(blk):
        for i, lower in enumerate(lowers):
            jj = step if lower else blk - 1 - step
            st, jr = divmod(jj, SUBLANES)
            done_row = inv_t[i][st][jr:jr + 1, :]
            for s in (range(st, sub) if lower else range(st + 1)):
                col = jnp.take_along_axis(d_t[i][s], group + jj, axis=1)[:, :n]
                inv_t[i][s] = inv_t[i][s] - col * done_row
    ts = []
    for i in range(len(mats)):
        inv = jnp.concatenate(inv_t[i], axis=0)
        ts.append(jnp.concatenate([jnp.where(lane // blk == bi, inv, 0.0) for bi in range(n // blk)], axis=0))
    r = lax.broadcasted_iota(jnp.int32, (n, n), 0)
    c = lax.broadcasted_iota(jnp.int32, (n, n), 1)
    size = blk
    while size < n:
        level = (r // (2 * size) == c // (2 * size)) & (r // size != c // size)
        half = [_dot(t, jnp.where(level, a, 0.0)) for t, a in zip(ts, mats)]
        ts = [t - _dot(h, t) for t, h in zip(ts, half)]
        size *= 2
    return ts


def _l2norm_heads(x, heads, dim, scale):
    outs = []
    for h in range(heads):
        xh = x[:, h * dim:(h + 1) * dim]
        outs.append(xh * (lax.rsqrt(jnp.sum(xh * xh, axis=-1, keepdims=True) + 1e-6) * scale))
    return outs


def _gdn_kernel(
    fwd_t, bwd_t, first_t, last_t, seq_t,
    qf_ref, kf_ref, vf_ref, smf_ref, qb_ref, kb_ref, vb_ref, smb_ref, pr_ref, pc_ref, s0_ref,
    of_ref, ob_ref, sout_ref, s_scr,
):
    s = pl.program_id(0)
    q, dim = GDN_CHUNK, GDN_HEAD_DIM

    @pl.when(first_t[s] == 1)
    def _():
        s_scr[...] = s0_ref[...]

    bias_r, alog_r = pr_ref[0:1], pr_ref[1:2]
    bias_c, alog_c = pc_ref[:, 0:1], pc_ref[:, 1:2]
    n_sub = GDN_CHUNKS_PER_STEP
    probs = []
    for d in range(N_DIR):
        q_ref, k_ref, v_ref, sm_ref = (qf_ref, kf_ref, vf_ref, smf_ref) if d == 0 else (qb_ref, kb_ref, vb_ref, smb_ref)
        incl, strict = _scan_masks(q, d)
        end = q - 1 if d == 0 else 0
        for c in _scan_order(d, n_sub):
            rows = slice(c * q, (c + 1) * q)
            qs = _l2norm_heads(_silu(q_ref[rows, :]), GDN_HEADS, dim, dim ** -0.5)
            ks = _l2norm_heads(_silu(k_ref[rows, :]), GDN_HEADS, dim, 1.0)
            v = _silu(v_ref[rows, :])
            sm = sm_ref[rows, :]
            beta_c = jax.nn.sigmoid(sm)
            g_c = -jnp.exp(alog_r) * _softplus(sm + bias_r)
            g_r = -jnp.exp(alog_c) * _softplus(sm.T + bias_c)
            gc_c, gc_r = _cumsum_pair(g_c, g_r[SM_GA:SM_GA + N_DIR * GDN_HEADS], q, d)
            for h in range(GDN_HEADS):
                idx = d * GDN_HEADS + h
                row = gc_r[idx:idx + 1, :]
                probs.append(dict(
                    d=d, c=c, h=h, q=qs[h], k=ks[h], v=v[:, h * dim:(h + 1) * dim], row=row, tot=row[:, end:end + 1],
                    colb=jnp.broadcast_to(gc_c[:, SM_GA + idx:SM_GA + idx + 1], (q, dim)),
                    betab=jnp.broadcast_to(beta_c[:, SM_GBETA + idx:SM_GBETA + idx + 1], (q, dim)),
                    incl=incl, strict=strict,
                ))
    kk = [_dot_nt(p["k"], p["k"]) for p in probs]
    qk = [_dot_nt(p["q"], p["k"]) for p in probs]
    dmask = [jnp.exp(jnp.where(p["incl"], p["colb"][:, :q] - p["row"], -1e30)) for p in probs]
    a = [jnp.where(p["strict"], p["betab"][:, :q] * kk_i * dm, 0.0) for p, kk_i, dm in zip(probs, kk, dmask)]
    ecol = [jnp.exp(p["colb"]) for p in probs]
    rhs = [jnp.concatenate([p["v"] * p["betab"], p["k"] * (p["betab"] * e)], axis=-1) for p, e in zip(probs, ecol)]
    inv = _unit_tri_inverses(a, [p["d"] == 0 for p in probs])
    sol = [_dot(t, r) for t, r in zip(inv, rhs)]
    attn = [qk_i * dm for qk_i, dm in zip(qk, dmask)]
    q_dec = [p["q"] * e for p, e in zip(probs, ecol)]
    k_dec = [p["k"] * jnp.exp(p["tot"] - p["colb"]) for p in probs]
    state = {(d, h): s_scr[d, h] for d in range(N_DIR) for h in range(GDN_HEADS)}
    outs = {}
    for step in range(n_sub):
        group = [(i, p) for i, p in enumerate(probs) if _scan_order(p["d"], n_sub)[step] == p["c"]]
        w_s = [_dot(sol[i][:, dim:], state[p["d"], p["h"]]) for i, p in group]
        q_s = [_dot(q_dec[i], state[p["d"], p["h"]]) for i, p in group]
        vv = [sol[i][:, :dim] - ws for (i, p), ws in zip(group, w_s)]
        o_at = [_dot(attn[i], vv_i) for (i, p), vv_i in zip(group, vv)]
        upd = [_dot_tn(k_dec[i], vv_i) for (i, p), vv_i in zip(group, vv)]
        for (i, p), qs_i, oa, u_i in zip(group, q_s, o_at, upd):
            outs[p["d"], p["c"], p["h"]] = qs_i + oa
            state[p["d"], p["h"]] = state[p["d"], p["h"]] * jnp.exp(p["tot"]) + u_i
    for (d, h), s_new in state.items():
        s_scr[d, h] = s_new
    for d, o_ref in ((0, of_ref), (1, ob_ref)):
        for c in range(n_sub):
            o_ref[c * q:(c + 1) * q, :] = jnp.concatenate([outs[d, c, h] for h in range(GDN_HEADS)], axis=-1)

    @pl.when(last_t[s] == 1)
    def _():
        sout_ref[...] = s_scr[...]


def _gdn_call(proj, small, tables, s0, a_log, dt_bias, cfg):
    q, w = GDN_CHUNKS_PER_STEP * GDN_CHUNK, GROUP_WIDTH
    t = proj.shape[0]
    n_steps = tables[0].shape[0]
    nlane = N_DIR * GDN_HEADS
    pr = jnp.zeros((SUBLANES, LANES), F32)
    pr = pr.at[0, SM_GA:SM_GA + nlane].set(dt_bias.reshape(-1)).at[1, SM_GA:SM_GA + nlane].set(a_log.reshape(-1))
    pc = pr.T
    st_shape = (N_DIR, GDN_HEADS, GDN_HEAD_DIM, GDN_HEAD_DIM)
    fmap = lambda col: (lambda s, fw, bw, fi, la, sq: (fw[s], col))
    bmap = lambda col: (lambda s, fw, bw, fi, la, sq: (bw[s], col))
    const = lambda shape: pl.BlockSpec(shape, lambda s, *_: tuple(0 for _ in shape))
    st_spec = pl.BlockSpec((None,) + st_shape, lambda s, fw, bw, fi, la, sq: (sq[s], 0, 0, 0, 0))
    blk = lambda m, col: pl.BlockSpec((q, w), m(col))
    grid_spec = pltpu.PrefetchScalarGridSpec(
        num_scalar_prefetch=5,
        grid=(n_steps,),
        in_specs=[
            blk(fmap, COL_G_Q), blk(fmap, COL_G_K), blk(fmap, COL_G_V), pl.BlockSpec((q, LANES), fmap(0)),
            blk(bmap, COL_G_Q), blk(bmap, COL_G_K), blk(bmap, COL_G_V), pl.BlockSpec((q, LANES), bmap(0)),
            const((SUBLANES, LANES)), const((LANES, SUBLANES)), st_spec,
        ],
        out_specs=[pl.BlockSpec((q, w), fmap(0)), pl.BlockSpec((q, w), bmap(0)), st_spec],
        scratch_shapes=[pltpu.VMEM(st_shape, F32)],
    )
    n_seq = cfg["n_dec"] + cfg["n_ctx"]
    return pl.pallas_call(
        _gdn_kernel,
        out_shape=[
            jax.ShapeDtypeStruct((t, w), F32), jax.ShapeDtypeStruct((t, w), F32),
            jax.ShapeDtypeStruct((n_seq,) + st_shape, F32),
        ],
        grid_spec=grid_spec,
        compiler_params=_params("arbitrary"),
        name="gdn_scan",
    )(*tables, proj, proj, proj, small, proj, proj, proj, small, pr, pc, s0)


def _rope_tables(cfg):
    l = cfg["ls"]
    pos = np.arange(l)
    r = (pos // GRID_W).astype(np.float32)
    col = (pos % GRID_W).astype(np.float32)
    nf = RET_HEAD_DIM // 4
    inv = np.power(np.float32(ROPE_BASE), -np.arange(nf, dtype=np.float32) / np.float32(nf)).astype(np.float32)
    ang = np.concatenate([r[:, None] * inv, col[:, None] * inv], -1)
    cos, sin = np.cos(ang), np.sin(ang)
    cos2 = np.concatenate([cos, cos], -1)
    sin2 = np.concatenate([-sin, sin], -1)
    ident = np.ones((RET_CHUNK, RET_HEAD_DIM), np.float32)
    return np.concatenate([cos2, ident], 0).astype(np.float32), np.concatenate([sin2, 0.0 * ident], 0).astype(np.float32)


def _rope_block_table(cfg):
    q = RET_CHUNK
    nc_s, nc_p = cfg["ls"] // q, cfg["lp"] // q
    fwd = [c for _ in range(cfg["n_dec"]) for c in range(nc_s)] + [nc_s] * (cfg["n_ctx"] * nc_p)
    bwd = [nc_s - 1 - c for _ in range(cfg["n_dec"]) for c in range(nc_s)] + [nc_s] * (cfg["n_ctx"] * nc_p)
    return jnp.asarray(np.asarray(fwd, np.int32)), jnp.asarray(np.asarray(bwd, np.int32))


def _ret_kernel(
    fwd_t, bwd_t, first_t, last_t, seq_t, rf_t, rb_t,
    qf_ref, kf_ref, vf_ref, cf_ref, sf_ref, qb_ref, kb_ref, vb_ref, cb_ref, sb_ref, dec_ref, s0_ref,
    of_ref, ob_ref, sout_ref, s_scr,
):
    s = pl.program_id(0)
    q, dim = RET_CHUNK, RET_HEAD_DIM

    @pl.when(first_t[s] == 1)
    def _():
        s_scr[...] = s0_ref[...]

    lg_all = -jnp.exp(dec_ref[...])
    ri = lax.broadcasted_iota(jnp.int32, (q, q), 0)
    ci = lax.broadcasted_iota(jnp.int32, (q, q), 1)
    rpos = lax.broadcasted_iota(jnp.int32, (q, dim), 0)
    probs = []
    for d in range(N_DIR):
        q_ref, k_ref, v_ref, c_ref, sn_ref = (
            (qf_ref, kf_ref, vf_ref, cf_ref, sf_ref) if d == 0 else (qb_ref, kb_ref, vb_ref, cb_ref, sb_ref)
        )
        qa, ka, va = q_ref[...], k_ref[...], v_ref[...]
        cos, sin = c_ref[...], sn_ref[...]
        rel = ((ri - ci) if d == 0 else (ci - ri)).astype(F32)
        pos = (rpos if d == 0 else (q - 1 - rpos)).astype(F32)
        for h in range(RET_HEADS):
            idx = d * RET_HEADS + h
            lg = lg_all[idx:idx + 1, :]
            qh = qa[:, h * dim:(h + 1) * dim]
            kh = ka[:, h * dim:(h + 1) * dim] * dim ** -0.5
            probs.append(dict(
                d=d, h=h, lg=lg, lg1=lg, rel=rel, pos=pos, v=va[:, h * dim:(h + 1) * dim],
                q=qh * cos + pltpu.roll(qh, dim // 2, axis=1) * sin,
                k=kh * cos + pltpu.roll(kh, dim // 2, axis=1) * sin,
            ))
    st = [s_scr[pb["d"], pb["h"]] for pb in probs]
    qk = [_dot_nt(pb["q"], pb["k"]) for pb in probs]
    dmask = [jnp.exp(jnp.where(pb["rel"] >= 0, pb["rel"] * pb["lg"], -1e30)) for pb in probs]
    o_state = [_dot(pb["q"] * jnp.exp((pb["pos"] + 1.0) * pb["lg1"]), s_i) for pb, s_i in zip(probs, st)]
    o_attn = [_dot(qk_i * dm, pb["v"]) for pb, qk_i, dm in zip(probs, qk, dmask)]
    upd = [_dot_tn(pb["k"] * jnp.exp((q - 1.0 - pb["pos"]) * pb["lg1"]), pb["v"]) for pb in probs]
    for pb, s_i, u_i in zip(probs, st, upd):
        s_scr[pb["d"], pb["h"]] = s_i * jnp.exp(q * pb["lg"]) + u_i
    outs = [a + b for a, b in zip(o_state, o_attn)]
    of_ref[...] = jnp.concatenate(outs[:RET_HEADS], axis=-1)
    ob_ref[...] = jnp.concatenate(outs[RET_HEADS:], axis=-1)

    @pl.when(last_t[s] == 1)
    def _():
        sout_ref[...] = s_scr[...]


def _ret_call(proj, tables, rope_tabs, rope_blocks, s0, ret_decay, cfg):
    q, w, dim = RET_CHUNK, GROUP_WIDTH, RET_HEAD_DIM
    t = proj.shape[0]
    n_steps = tables[0].shape[0]
    dec = jnp.broadcast_to(ret_decay.reshape(-1, 1), (N_DIR * RET_HEADS, LANES))
    st_shape = (N_DIR, RET_HEADS, dim, dim)
    fmap = lambda col: (lambda s, fw, bw, fi, la, sq, rf, rb: (fw[s], col))
    bmap = lambda col: (lambda s, fw, bw, fi, la, sq, rf, rb: (bw[s], col))
    rfmap = lambda s, fw, bw, fi, la, sq, rf, rb: (rf[s], 0)
    rbmap = lambda s, fw, bw, fi, la, sq, rf, rb: (rb[s], 0)
    st_spec = pl.BlockSpec((None,) + st_shape, lambda s, fw, bw, fi, la, sq, rf, rb: (sq[s], 0, 0, 0, 0))
    blk = lambda m, col: pl.BlockSpec((q, w), m(col))
    rope = lambda m: pl.BlockSpec((q, dim), m)
    grid_spec = pltpu.PrefetchScalarGridSpec(
        num_scalar_prefetch=7,
        grid=(n_steps,),
        in_specs=[
            blk(fmap, COL_R_Q), blk(fmap, COL_R_K), blk(fmap, COL_R_V), rope(rfmap), rope(rfmap),
            blk(bmap, COL_R_Q), blk(bmap, COL_R_K), blk(bmap, COL_R_V), rope(rbmap), rope(rbmap),
            pl.BlockSpec((N_DIR * RET_HEADS, LANES), lambda s, *_: (0, 0)), st_spec,
        ],
        out_specs=[pl.BlockSpec((q, w), fmap(0)), pl.BlockSpec((q, w), bmap(0)), st_spec],
        scratch_shapes=[pltpu.VMEM(st_shape, F32)],
    )
    n_seq = cfg["n_dec"] + cfg["n_ctx"]
    cos2, sin2 = rope_tabs
    return pl.pallas_call(
        _ret_kernel,
        out_shape=[
            jax.ShapeDtypeStruct((t, w), F32), jax.ShapeDtypeStruct((t, w), F32),
            jax.ShapeDtypeStruct((n_seq,) + st_shape, F32),
        ],
        grid_spec=grid_spec,
        compiler_params=_params("arbitrary"),
        name="retention_scan",
    )(*tables, *rope_blocks, proj, proj, proj, cos2, sin2, proj, proj, proj, cos2, sin2, dec, s0)


def _layer_norm_rows(y, g, b):
    mu = jnp.mean(y, axis=-1, keepdims=True)
    yc = y - mu
    var = jnp.mean(yc * yc, axis=-1, keepdims=True)
    return yc * lax.rsqrt(var + LN_EPS) * g + b


def _outproj_kernel(
    hys_ref, hyp_ref, sf_ref, sb_ref, sz_ref, gf_ref, gb_ref, gz_ref, rf_ref, rb_ref, rg_ref,
    snw_ref, gnw_ref, w_ref, xs_ref, xp_ref, g1_ref, sh2_ref, sc2_ref, lng_ref, lnb_ref,
    x1_ref, u2_ref, *, n_sample_tiles, alpha, two_group_x,
):
    i = pl.program_id(0)
    is_sample = i < n_sample_tiles
    w = GROUP_WIDTH
    y_hy = jnp.where(is_sample, hys_ref[...], hyp_ref[...])
    y_ssd = (sf_ref[...] + sb_ref[...]) * _silu(sz_ref[...])
    y_ssd = y_ssd * lax.rsqrt(jnp.mean(y_ssd * y_ssd, axis=-1, keepdims=True) + RMS_EPS) * snw_ref[...]
    og = gf_ref[...] + gb_ref[...]
    orr = rf_ref[...] + rb_ref[...]
    gz = _silu(gz_ref[...])
    rg = _silu(rg_ref[...])
    gnw = gnw_ref[...]
    acc = _dot(y_hy, w_ref[0:w, :]) + _dot(y_ssd, w_ref[w:2 * w, :])
    for h in range(GDN_HEADS):
        sl = slice(h * GDN_HEAD_DIM, (h + 1) * GDN_HEAD_DIM)
        o = og[:, sl]
        o = o * lax.rsqrt(jnp.mean(o * o, axis=-1, keepdims=True) + RMS_EPS) * gnw * gz[:, sl]
        acc += _dot(o, w_ref[2 * w + h * GDN_HEAD_DIM:2 * w + (h + 1) * GDN_HEAD_DIM, :])
    for h in range(RET_HEADS):
        sl = slice(h * RET_HEAD_DIM, (h + 1) * RET_HEAD_DIM)
        o = orr[:, sl]
        mu = jnp.mean(o, axis=-1, keepdims=True)
        oc = o - mu
        o = oc * lax.rsqrt(jnp.mean(oc * oc, axis=-1, keepdims=True) + LN_EPS) * rg[:, sl]
        acc += _dot(o, w_ref[3 * w + h * RET_HEAD_DIM:3 * w + (h + 1) * RET_HEAD_DIM, :])
    x = jnp.where(is_sample, xs_ref[...], xp_ref[...]) if two_group_x else xs_ref[...]
    x1 = _layer_norm_rows(alpha * x + g1_ref[...] * acc, lng_ref[...], lnb_ref[...])
    x1_ref[...] = x1
    u2_ref[...] = (x1 * (1.0 + sc2_ref[...]) + sh2_ref[...]).astype(BF16)


def _outproj_call(hy_s, hy_p, ssd, gdn, ret, proj, ssd_nw, gdn_nw, w_out, x_in, mod4, layer, ln_g, ln_b, cfg):
    t, d, w = cfg["t"], cfg["d"], GROUP_WIDTH
    tm = 256
    ns = cfg["ts"] // tm
    two_group_x = isinstance(x_in, tuple)
    row = lambda col: pl.BlockSpec((tm, w), lambda i: (i, col))
    vec = lambda n: pl.BlockSpec((None, 1, n), lambda i: (layer, 0, 0))
    mspec = lambda k: _mod_spec(layer, k, d, tm, cfg["ls"], cfg["n_dec"])
    if two_group_x:
        x_specs = _two_group_specs(tm, d, ns)
        x_args = list(x_in)
    else:
        x_specs = [pl.BlockSpec((tm, d), lambda i: (i, 0)), pl.BlockSpec((SUBLANES, d), lambda i: (0, 0))]
        x_args = [x_in, x_in]
    kern = functools.partial(_outproj_kernel, n_sample_tiles=ns, alpha=cfg["alpha"], two_group_x=two_group_x)
    return pl.pallas_call(
        kern,
        out_shape=[jax.ShapeDtypeStruct((t, d), F32), jax.ShapeDtypeStruct((t, d), BF16)],
        grid=(t // tm,),
        in_specs=_two_group_specs(tm, w, ns)
        + [row(0), row(0), row(COL_S_Z), row(0), row(0), row(COL_G_Z), row(0), row(0), row(COL_R_G)]
        + [vec(w), vec(GDN_HEAD_DIM), pl.BlockSpec((None, d, d), lambda i: (layer, 0, 0))]
        + x_specs
        + [mspec(2), mspec(3), mspec(4), vec(d), vec(d)],
        out_specs=[pl.BlockSpec((tm, d), lambda i: (i, 0)), pl.BlockSpec((tm, d), lambda i: (i, 0))],
        compiler_params=_params("arbitrary"),
        name="out_proj_ln",
    )(
        hy_s, hy_p, ssd[0], ssd[1], proj, gdn[0], gdn[1], proj, ret[0], ret[1], proj,
        ssd_nw, gdn_nw, w_out, *x_args, mod4, mod4, mod4, ln_g, ln_b,
    )


def _ffn_up_kernel(u_ref, wg_ref, wv_ref, cwg_ref, cbg_ref, cwv_ref, cbv_ref, o_ref, w_scr, h_scr, *, n_sample_tiles, ls, lp):
    i = pl.program_id(1)
    tn = wg_ref.shape[1]
    seq_len = jnp.where(i < n_sample_tiles, ls, lp)

    @pl.when(i == 0)
    def _():
        w_scr[:, :tn] = wg_ref[...].astype(BF16)
        w_scr[:, tn:] = wv_ref[...].astype(BF16)

    cw = jnp.concatenate([cwg_ref[...], cwv_ref[...]], axis=-1)
    cb = jnp.concatenate([cbg_ref[...], cbv_ref[...]], axis=-1)

    def finish(rows, conv):
        o_ref[rows, :] = (_silu(conv[:, :tn]) * conv[:, tn:]).astype(BF16)

    _matmul_conv_chunks(u_ref, w_scr[...], cw, cb, h_scr, seq_len, min(ls, lp), finish)


def _ffn_up_call(u2, w_up, cw, cb, layer, cfg):
    t, d = u2.shape
    dff = w_up.shape[2] // 2
    tm, tn = cfg["ls"], 256
    nj = dff // tn
    kern = functools.partial(_ffn_up_kernel, n_sample_tiles=cfg["ts"] // tm, ls=cfg["ls"], lp=cfg["lp"])
    gcol = lambda j, i: (layer, 0, j)
    vcol = lambda j, i: (layer, 0, j + nj)
    return pl.pallas_call(
        kern,
        out_shape=jax.ShapeDtypeStruct((t, dff), BF16),
        grid=(nj, t // tm),
        in_specs=[
            pl.BlockSpec((tm, d), lambda j, i: (i, 0)),
            pl.BlockSpec((None, d, tn), gcol), pl.BlockSpec((None, d, tn), vcol),
            pl.BlockSpec((None, CONV_W, tn), gcol), pl.BlockSpec((None, 1, tn), gcol),
            pl.BlockSpec((None, CONV_W, tn), vcol), pl.BlockSpec((None, 1, tn), vcol),
        ],
        out_specs=pl.BlockSpec((tm, tn), lambda j, i: (i, j)),
        scratch_shapes=[pltpu.VMEM((d, 2 * tn), BF16), pltpu.VMEM((tm + 2 * SUBLANES, 2 * tn), F32)],
        compiler_params=_params("arbitrary", "arbitrary"),
        name="ffn_up_conv_glu",
    )(u2, w_up, w_up, cw, cb, cw, cb)


def _ffn_down_kernel(a_ref, w_ref, x1_ref, g2_ref, shn_ref, scn_ref, lng_ref, lnb_ref, oa_ref, ob_ref, acc_scr,
                     *, alpha, n_sample_tiles, last_layer):
    i = pl.program_id(0)
    k = pl.program_id(1)

    @pl.when(k == 0)
    def _():
        acc_scr[...] = jnp.zeros_like(acc_scr)

    acc_scr[...] += jnp.dot(a_ref[...], w_ref[...], preferred_element_type=F32)

    @pl.when(k == pl.num_programs(1) - 1)
    def _():
        x2 = _layer_norm_rows(alpha * x1_ref[...] + g2_ref[...] * acc_scr[...], lng_ref[...], lnb_ref[...])
        if last_layer:
            @pl.when(i < n_sample_tiles)
            def _():
                oa_ref[...] = x2

            @pl.when(i >= n_sample_tiles)
            def _():
                ob_ref[...] = x2
        else:
            oa_ref[...] = x2
            ob_ref[...] = (x2 * (1.0 + scn_ref[...]) + shn_ref[...]).astype(BF16)


def _ffn_down_call(act, w_down, x1, mod4, layer, ln_g, ln_b, cfg):
    t, dff = act.shape
    d = cfg["d"]
    depth = w_down.shape[0]
    last_layer = layer == depth - 1
    next_layer = min(layer + 1, depth - 1)
    tm = 512
    tk = dff // 4 if (dff // 4) % LANES == 0 else dff
    ns = cfg["ts"] // tm
    vec = lambda: pl.BlockSpec((None, 1, d), lambda i, k: (layer, 0, 0))
    mspec = lambda lay, kk: _mod_spec(lay, kk, d, tm, cfg["ls"], cfg["n_dec"])
    if last_layer:
        out_shape = [jax.ShapeDtypeStruct((cfg["ts"], d), F32), jax.ShapeDtypeStruct((cfg["tp"], d), F32)]
        out_specs = [
            pl.BlockSpec((tm, d), lambda i, k: (jnp.minimum(i, ns - 1), 0)),
            pl.BlockSpec((tm, d), lambda i, k: (jnp.maximum(i - ns, 0), 0)),
        ]
    else:
        out_shape = [jax.ShapeDtypeStruct((t, d), F32), jax.ShapeDtypeStruct((t, d), BF16)]
        out_specs = [pl.BlockSpec((tm, d), lambda i, k: (i, 0)), pl.BlockSpec((tm, d), lambda i, k: (i, 0))]
    return pl.pallas_call(
        functools.partial(_ffn_down_kernel, alpha=cfg["alpha"], n_sample_tiles=ns, last_layer=last_layer),
        out_shape=out_shape,
        grid=(t // tm, dff // tk),
        in_specs=[
            pl.BlockSpec((tm, tk), lambda i, k: (i, k)),
            pl.BlockSpec((None, tk, d), lambda i, k: (layer, k, 0)),
            pl.BlockSpec((tm, d), lambda i, k: (i, 0)),
            mspec(layer, 5), mspec(next_layer, 0), mspec(next_layer, 1), vec(), vec(),
        ],
        out_specs=out_specs,
        scratch_shapes=[pltpu.VMEM((tm, d), F32)],
        compiler_params=_params("arbitrary", "arbitrary"),
        name="ffn_down_ln",
    )(act, w_down, x1, mod4, mod4, mod4, ln_g, ln_b)


def _reorder_w_in(w_in):
    w = GROUP_WIDTH
    o = N_CONV
    s_z = w_in[..., o:o + w]
    o += w
    s_dt = w_in[..., o:o + N_DIR * SSD_HEADS]
    o += N_DIR * SSD_HEADS
    g_z = w_in[..., o:o + w]
    o += w
    g_beta = w_in[..., o:o + N_DIR * GDN_HEADS]
    o += N_DIR * GDN_HEADS
    g_a = w_in[..., o:o + N_DIR * GDN_HEADS]
    o += N_DIR * GDN_HEADS
    rest = w_in[..., o:]
    main = jnp.concatenate([w_in[..., :N_CONV], s_z, g_z, rest], axis=-1).astype(BF16)
    small = jnp.concatenate([s_dt, g_beta, g_a], axis=-1)
    small = jnp.pad(small, ((0, 0), (0, 0), (0, LANES - small.shape[-1]))).astype(BF16)
    return main, small


def kernel(x_prompt, x_sample, state_ssd, state_gdn, state_ret, c, c_ctx, w_mod, b_mod, w_in, conv_w, conv_b,
           hy_w1, hy_b1, hy_w2, hy_b2, hy_w3, hy_freq, hy_bias, ssd_A_log, ssd_dt_bias, ssd_D, ssd_norm_w,
           gdn_A_log, gdn_dt_bias, gdn_norm_w, ret_decay, w_out, ln1_g, ln1_b, w_up, ffn_conv_w, ffn_conv_b,
           w_down, ln2_g, ln2_b):
    n_ctx, lp, d = x_prompt.shape
    n_dec, ls, _ = x_sample.shape
    depth = w_in.shape[0]
    ts, tp = n_dec * ls, n_ctx * lp
    assert ls & (ls - 1) == 0 and lp & (lp - 1) == 0 and tp % ls == 0 and ls % lp == 0
    assert d == 4 * GROUP_WIDTH and w_in.shape[2] == N_MAIN + 2 * N_DIR * (SSD_HEADS // 2 + GDN_HEADS)
    cfg = dict(d=d, ls=ls, lp=lp, n_dec=n_dec, n_ctx=n_ctx, ts=ts, tp=tp, t=ts + tp, alpha=(2 * depth) ** 0.25)

    xs = x_sample.reshape(ts, d)
    xp = x_prompt.reshape(tp, d)

    mod_rows = -(-(n_dec + 1) // SUBLANES) * SUBLANES
    cond = jnp.concatenate([c, c_ctx[None, :], jnp.zeros((mod_rows - n_dec - 1, d), F32)], axis=0)
    mod = _mod_call(cond, w_mod, b_mod)
    mod4 = mod.reshape(depth, mod_rows, 1, 6 * d)

    w_main, w_small = _reorder_w_in(w_in)
    w_out_b, w_down_b = w_out.astype(BF16), w_down.astype(BF16)
    stack_rows = lambda v: v[:, None, :]

    dft = {l: tuple(jnp.asarray(tab).astype(BF16) for tab in _dft_tables(l)) for l in (ls, lp)}
    rope_tabs = _rope_tables(cfg)
    rope_blocks = _rope_block_table(cfg)
    ssd_rows, gdn_rows = CHUNKS_PER_STEP * SSD_CHUNK, GDN_CHUNKS_PER_STEP * GDN_CHUNK
    tabs = {q: _chunk_tables(cfg, q) for q in sorted({ssd_rows, gdn_rows, RET_CHUNK})}

    def init_state(st, l):
        zeros = jnp.zeros((n_ctx,) + st.shape[2:], F32)
        return jnp.concatenate([st[:, l], zeros], axis=0)

    u = _modulate_call(xs, xp, mod4, 0, cfg)
    x_res = (xs, xp)
    new_ssd, new_gdn, new_ret = [], [], []
    for l in range(depth):
        proj, small = _inproj_call(u, w_main, w_small, conv_w, stack_rows(conv_b), l, cfg)

        hy = {}
        for name, seq_len, n_seq, row0 in (("s", ls, n_dec, 0), ("p", lp, n_ctx, ts)):
            a_tab, b_tab = dft[seq_len]
            prs = _hy_filter_call(seq_len, a_tab, hy_w1[l], hy_b1[l], hy_w2[l], hy_b2[l], hy_w3[l], hy_freq[l])
            z1 = _hy_conv_call(proj, COL_HY_V, row0, proj, COL_HY_X1, row0, hy_bias[l], a_tab, b_tab, prs, 0, seq_len, n_seq)
            hy[name] = _hy_conv_call(z1, 0, 0, proj, COL_HY_X2, row0, hy_bias[l], a_tab, b_tab, prs, 1, seq_len, n_seq)

        ssd = _ssd_call(proj, small, tabs[ssd_rows], init_state(state_ssd, l), ssd_dt_bias[l], ssd_A_log[l], ssd_D[l], cfg)
        gdn = _gdn_call(proj, small, tabs[gdn_rows], init_state(state_gdn, l), gdn_A_log[l], gdn_dt_bias[l], cfg)
        ret = _ret_call(proj, tabs[RET_CHUNK], rope_tabs, rope_blocks, init_state(state_ret, l), ret_decay[l], cfg)
        new_ssd.append(ssd[2][n_dec:])
        new_gdn.append(gdn[2][n_dec:])
        new_ret.append(ret[2][n_dec:])

        x1, u2 = _outproj_call(hy["s"], hy["p"], ssd, gdn, ret, proj, stack_rows(ssd_norm_w), stack_rows(gdn_norm_w),
                               w_out_b, x_res, mod4, l, stack_rows(ln1_g), stack_rows(ln1_b), cfg)
        act = _ffn_up_call(u2, w_up, ffn_conv_w, stack_rows(ffn_conv_b), l, cfg)
        x_res, u = _ffn_down_call(act, w_down_b, x1, mod4, l, stack_rows(ln2_g), stack_rows(ln2_b), cfg)

    y_sample = x_res.reshape(n_dec, ls, d)
    y_prompt = u.reshape(n_ctx, lp, d)
    return (y_prompt, y_sample, jnp.stack(new_ssd, 1), jnp.stack(new_gdn, 1), jnp.stack(new_ret, 1))
```

```python
import functools
import math

import numpy as np
import jax
import jax.numpy as jnp
from jax import lax
from jax.experimental import pallas as pl
from jax.experimental.pallas import tpu as pltpu

F32 = jnp.float32
BF16 = jnp.bfloat16
HIGHEST = lax.Precision.HIGHEST

N_DIR = 2
CONV_W = 3
GROUP_WIDTH = 512
HY_ORDER = 2
HY_BANDS = 8
HY_EMB = 1 + 2 * HY_BANDS
HY_FFN = 64
HY_DECAY_TARGET = 1e-2
HY_FAST_PCT = 0.3
HY_SLOW_PCT = 1.5
SSD_HEADS = 8
SSD_HEAD_DIM = 64
SSD_GROUPS = 2
SSD_HPG = SSD_HEADS // SSD_GROUPS
SSD_STATE = 128
SSD_CHUNK = 128
GDN_HEADS = 4
GDN_HEAD_DIM = 128
GDN_CHUNK = 64
RET_HEADS = 4
RET_HEAD_DIM = 128
RET_CHUNK = 128
GRID_W = 64
ROPE_BASE = 10000.0
LN_EPS = 1e-5
RMS_EPS = 1e-6
N_CONV = 8 * GROUP_WIDTH

LANES = 128
SUBLANES = 8
VMEM_LIMIT_BYTES = 60 * 1024 * 1024

COL_HY_V, COL_HY_X1, COL_HY_X2, COL_S_X, COL_S_BC, COL_G_Q, COL_G_K, COL_G_V = range(8)
COL_S_Z, COL_G_Z, COL_R_Q, COL_R_K, COL_R_V, COL_R_G = range(8, 14)
N_MAIN = 14 * GROUP_WIDTH
SM_SDT = 0
SM_GBETA = 16
SM_GA = 24


def _silu(x):
    return x * jax.nn.sigmoid(x)


def _softplus(x):
    return jnp.maximum(x, 0.0) + jnp.log1p(jnp.exp(-jnp.abs(x)))


def _dot(a, b):
    return jnp.dot(a.astype(BF16), b.astype(BF16), preferred_element_type=F32)


def _dot_nt(a, b):
    return lax.dot_general(a.astype(BF16), b.astype(BF16), (((1,), (1,)), ((), ())), preferred_element_type=F32)


def _dot_tn(a, b):
    return lax.dot_general(a.astype(BF16), b.astype(BF16), (((0,), (0,)), ((), ())), preferred_element_type=F32)


def _dot_hi(a, b):
    return jnp.dot(a, b, preferred_element_type=F32, precision=HIGHEST)


def _params(*semantics):
    return pltpu.CompilerParams(dimension_semantics=semantics, vmem_limit_bytes=VMEM_LIMIT_BYTES)


def _tri(n, lower):
    r = lax.broadcasted_iota(jnp.int32, (n, n), 0)
    c = lax.broadcasted_iota(jnp.int32, (n, n), 1)
    return jnp.where((r >= c) if lower else (r <= c), 1.0, 0.0).astype(F32)


def _mod_kernel(c_ref, w_ref, b_ref, o_ref):
    a = _silu(c_ref[...])
    o_ref[...] = _dot(a, w_ref[...]) + b_ref[...]


def _mod_call(cond, w_mod, b_mod):
    depth, d, n = w_mod.shape
    rows = cond.shape[0]
    tn = 1024
    return pl.pallas_call(
        _mod_kernel,
        out_shape=jax.ShapeDtypeStruct((depth, rows, n), F32),
        grid=(depth, n // tn),
        in_specs=[
            pl.BlockSpec((rows, d), lambda l, j: (0, 0)),
            pl.BlockSpec((None, d, tn), lambda l, j: (l, 0, j)),
            pl.BlockSpec((None, 1, tn), lambda l, j: (l, 0, j)),
        ],
        out_specs=pl.BlockSpec((None, rows, tn), lambda l, j: (l, 0, j)),
        compiler_params=_params("arbitrary", "arbitrary"),
        name="adaln_mod",
    )(cond, w_mod, b_mod.reshape(depth, 1, n))


def _mod_spec(layer, k, d, rows_per_tile, l_sample, n_dec):
    tiles_per_seq = l_sample // rows_per_tile

    def index_map(i, *_):
        return (layer, jnp.minimum(i // tiles_per_seq, n_dec), 0, k)

    return pl.BlockSpec((None, None, 1, d), index_map)


def _modulate_kernel(xs_ref, xp_ref, sh_ref, sc_ref, o_ref, *, n_sample_tiles):
    i = pl.program_id(0)
    x = jnp.where(i < n_sample_tiles, xs_ref[...], xp_ref[...])
    o_ref[...] = (x * (1.0 + sc_ref[...]) + sh_ref[...]).astype(BF16)


def _two_group_specs(tm, d, n_sample_tiles, **spec_kwargs):
    return [
        pl.BlockSpec((tm, d), lambda i, *_: (jnp.minimum(i, n_sample_tiles - 1), 0), **spec_kwargs),
        pl.BlockSpec((tm, d), lambda i, *_: (jnp.maximum(i - n_sample_tiles, 0), 0), **spec_kwargs),
    ]


def _modulate_call(xs, xp, mod4, layer, cfg):
    tm, d = 256, cfg["d"]
    t = cfg["t"]
    ns = cfg["ts"] // tm
    return pl.pallas_call(
        functools.partial(_modulate_kernel, n_sample_tiles=ns),
        out_shape=jax.ShapeDtypeStruct((t, d), BF16),
        grid=(t // tm,),
        in_specs=_two_group_specs(tm, d, ns)
        + [_mod_spec(layer, 0, d, tm, cfg["ls"], cfg["n_dec"]), _mod_spec(layer, 1, d, tm, cfg["ls"], cfg["n_dec"])],
        out_specs=pl.BlockSpec((tm, d), lambda i: (i, 0)),
        compiler_params=_params("arbitrary"),
        name="modulate_in",
    )(xs, xp, mod4, mod4)


CONV_ROW_CHUNK = 256


def _matmul_conv_chunks(u_ref, w, cw, cb, h_scr, seq_len, min_seq_len, finish):
    rows = u_ref.shape[0]
    rc = min(CONV_ROW_CHUNK, rows)
    assert rows % rc == 0 and rc % min_seq_len == 0
    n_chunks = rows // rc
    n = w.shape[1]
    halo = SUBLANES
    first = lax.broadcasted_iota(jnp.int32, (SUBLANES, 1), 0) == 0
    last = lax.broadcasted_iota(jnp.int32, (SUBLANES, 1), 0) == SUBLANES - 1
    tile = lambda v, t: v[t * SUBLANES:(t + 1) * SUBLANES]
    h_scr[0:halo, :] = jnp.zeros((halo, n), F32)
    h_scr[halo + rows:2 * halo + rows, :] = jnp.zeros((halo, n), F32)
    for r in range(n_chunks + 1):
        if r < n_chunks:
            h_scr[halo + r * rc:halo + (r + 1) * rc, :] = jnp.dot(u_ref[r * rc:(r + 1) * rc, :], w, preferred_element_type=F32)
        if r == 0:
            continue
        row0 = (r - 1) * rc
        h = h_scr[halo + row0:halo + row0 + rc, :]
        hp = h_scr[halo + row0 - 1:halo + row0 - 1 + rc, :]
        hn = h_scr[halo + row0 + 1:halo + row0 + 1 + rc, :]
        fix_p, fix_n = {}, {}
        for b in range(0, rc, min_seq_len):
            seq_start = ((row0 + b) & (seq_len - 1)) == 0
            seq_end = ((row0 + b + min_seq_len) & (seq_len - 1)) == 0
            tp, tn_ = b // SUBLANES, (b + min_seq_len) // SUBLANES - 1
            fix_p[tp] = jnp.where(jnp.logical_and(first, seq_start), 0.0, tile(hp, tp))
            fix_n[tn_] = jnp.where(jnp.logical_and(last, seq_end), 0.0, tile(hn, tn_))
        n_tiles = rc // SUBLANES
        hp = jnp.concatenate([fix_p.get(t, tile(hp, t)) for t in range(n_tiles)], axis=0)
        hn = jnp.concatenate([fix_n.get(t, tile(hn, t)) for t in range(n_tiles)], axis=0)
        finish(slice(row0, row0 + rc), hp * cw[0:1] + h * cw[1:2] + hn * cw[2:3] + cb)


def _inproj_kernel(u_ref, w_ref, ws_ref, cw_ref, cb_ref, o_ref, os_ref, h_scr, *, n_conv_tiles, n_sample_tiles, ls, lp):
    i = pl.program_id(0)
    j = pl.program_id(1)
    seq_len = jnp.where(i < n_sample_tiles, ls, lp)

    @pl.when(j < n_conv_tiles)
    def _():
        def finish(rows, conv):
            o_ref[rows, :] = conv

        _matmul_conv_chunks(u_ref, w_ref[...], cw_ref[...], cb_ref[...], h_scr, seq_len, min(ls, lp), finish)

    @pl.when(j >= n_conv_tiles)
    def _():
        o_ref[...] = jnp.dot(u_ref[...], w_ref[...], preferred_element_type=F32)

    @pl.when(j == 0)
    def _():
        os_ref[...] = jnp.dot(u_ref[...], ws_ref[...], preferred_element_type=F32)


def _inproj_call(u, w_main, w_small, cw, cb, layer, cfg):
    t, d = u.shape
    tm, tn = cfg["ls"], 512
    n_conv_tiles = N_CONV // tn
    kern = functools.partial(
        _inproj_kernel, n_conv_tiles=n_conv_tiles, n_sample_tiles=cfg["ts"] // tm, ls=cfg["ls"], lp=cfg["lp"]
    )
    conv_col = lambda i, j: (layer, 0, jnp.minimum(j, n_conv_tiles - 1))
    return pl.pallas_call(
        kern,
        out_shape=[jax.ShapeDtypeStruct((t, N_MAIN), F32), jax.ShapeDtypeStruct((t, LANES), F32)],
        grid=(t // tm, N_MAIN // tn),
        in_specs=[
            pl.BlockSpec((tm, d), lambda i, j: (i, 0)),
            pl.BlockSpec((None, d, tn), lambda i, j: (layer, 0, j)),
            pl.BlockSpec((None, d, LANES), lambda i, j: (layer, 0, 0)),
            pl.BlockSpec((None, CONV_W, tn), conv_col),
            pl.BlockSpec((None, 1, tn), conv_col),
        ],
        out_specs=[pl.BlockSpec((tm, tn), lambda i, j: (i, j)), pl.BlockSpec((tm, LANES), lambda i, j: (i, 0))],
        scratch_shapes=[pltpu.VMEM((tm + 2 * SUBLANES, tn), F32)],
        compiler_params=_params("arbitrary", "arbitrary"),
        name="in_proj_conv",
    )(u, w_main, w_small, cw, cb)


@functools.lru_cache(maxsize=None)
def _dft_tables(l):
    n = 2 * l
    k = np.arange(l, dtype=np.int64)
    ang = (2.0 * math.pi / n) * ((k[:, None] * k[None, :]) % n)
    cos, sin = np.cos(ang), np.sin(ang)
    alt = np.where(k % 2 == 0, 1.0, -1.0)
    a_im = np.where(k[:, None] == 0, alt[None, :], -sin)
    a = np.concatenate([cos, a_im], axis=0)
    ck = np.where(k == 0, 1.0, 2.0) / n
    b_re = cos * ck[None, :]
    b_im = np.where(k[None, :] == 0, alt[:, None] / n, -sin * (2.0 / n))
    b = np.concatenate([b_re, b_im], axis=1)
    return a.astype(np.float32), b.astype(np.float32)


@functools.lru_cache(maxsize=None)
def _hy_feats(l):
    pos = np.arange(l, dtype=np.float32)
    t = pos / np.float32(l - 1)
    bands = np.linspace(1e-4, HY_BANDS - 1, HY_BANDS, dtype=np.float32)
    ang = np.float32(2.0 * math.pi / l) * pos[:, None] * bands
    feats = np.concatenate([t[:, None], np.cos(ang), -np.sin(ang)], -1).astype(np.float32)
    feats = np.pad(feats, ((0, 0), (0, LANES - HY_EMB)))
    return feats, t[:, None]


@functools.lru_cache(maxsize=None)
def _hy_deltas(c):
    lo, hi = math.log(HY_DECAY_TARGET) / HY_SLOW_PCT, math.log(HY_DECAY_TARGET) / HY_FAST_PCT
    return np.abs(np.linspace(lo, hi, c, dtype=np.float32))[None, :]


def _hy_filter_kernel(
    feats_ref, t_ref, dl_ref, w1_ref, b1_ref, w2_ref, b2_ref, w3_ref, fq_ref, are_ref, aim_ref,
    p_ref, r_ref, s_ref, sum_scr, dif_scr, nyq_scr,
):
    f = pl.program_id(0)
    l = feats_ref.shape[0]
    c = dl_ref.shape[1]
    kb = are_ref.shape[0]

    @pl.when(f == 0)
    def _():
        fq = fq_ref[...]
        h = jnp.sin(fq[0:1] * (_dot_hi(feats_ref[...], w1_ref[...]) + b1_ref[...]))
        h = jnp.sin(fq[1:2] * (_dot_hi(h, w2_ref[...]) + b2_ref[...]))
        decay = jnp.exp(-t_ref[...] * dl_ref[...])
        row = lax.broadcasted_iota(jnp.int32, (l, 1), 0)
        sign = jnp.where((row & 1) == 0, 1.0, -1.0)
        for o in range(HY_ORDER):
            fwd = _dot_hi(h, w3_ref[:, (2 * o) * c:(2 * o + 1) * c]) * decay
            bwd = _dot_hi(h, w3_ref[:, (2 * o + 1) * c:(2 * o + 2) * c]) * decay
            bwd = jnp.where(row == 0, 0.0, bwd)
            norm = jnp.sum(jnp.abs(fwd), axis=0, keepdims=True) + jnp.sum(jnp.abs(bwd), axis=0, keepdims=True)
            fwd = fwd / norm
            bwd = bwd / norm
            ssum = fwd + bwd
            sum_scr[o] = ssum.astype(BF16)
            dif_scr[o] = (fwd - bwd).astype(BF16)
            nyq_scr[o] = jnp.broadcast_to(jnp.sum(ssum * sign, axis=0, keepdims=True), (SUBLANES, c))

    grow = f * kb + lax.broadcasted_iota(jnp.int32, (kb, 1), 0)
    for o in range(HY_ORDER):
        kre = jnp.dot(are_ref[...], sum_scr[o], preferred_element_type=F32)
        kim = jnp.dot(aim_ref[...], dif_scr[o], preferred_element_type=F32)
        p_ref[o] = kre
        r_ref[o] = jnp.where(grow == 0, 0.0, kim)
        s_ref[o] = jnp.where(grow == 0, nyq_scr[o][0:1], kre)


def _hy_filter_call(l, a_tab, w1, b1, w2, b2, w3, freq):
    c = GROUP_WIDTH
    kb = min(l, 256)
    nf = l // kb
    feats, t = _hy_feats(l)
    w1p = jnp.pad(w1, ((0, LANES - HY_EMB), (0, 0)))
    full = lambda shape: pl.BlockSpec(shape, lambda f: tuple(0 for _ in shape))
    out = jax.ShapeDtypeStruct((HY_ORDER, l, c), F32)
    out_spec = pl.BlockSpec((HY_ORDER, kb, c), lambda f: (0, f, 0))
    return pl.pallas_call(
        _hy_filter_kernel,
        out_shape=[out, out, out],
        grid=(nf,),
        in_specs=[
            full((l, LANES)), full((l, 1)), full((1, c)), full((LANES, HY_FFN)), full((1, HY_FFN)),
            full((HY_FFN, HY_FFN)), full((1, HY_FFN)), full((HY_FFN, 2 * HY_ORDER * c)), full((2, HY_FFN)),
            pl.BlockSpec((kb, l), lambda f: (f, 0)),
            pl.BlockSpec((kb, l), lambda f: (f + nf, 0)),
        ],
        out_specs=[out_spec, out_spec, out_spec],
        scratch_shapes=[
            pltpu.VMEM((HY_ORDER, l, c), BF16), pltpu.VMEM((HY_ORDER, l, c), BF16),
            pltpu.VMEM((HY_ORDER, SUBLANES, c), F32),
        ],
        compiler_params=_params("arbitrary"),
        name="hyena_filter",
    )(feats, t, _hy_deltas(c), w1p, b1[None, :], w2, b2[None, :], w3, freq, a_tab, a_tab)


def _hy_conv_kernel(z_ref, g_ref, bias_ref, are_ref, aim_ref, bre_ref, bim_ref, p_ref, r_ref, s_ref, o_ref, zb_scr):
    f = pl.program_id(2)

    @pl.when(f == 0)
    def _():
        zb_scr[...] = z_ref[...].astype(BF16)
        o_ref[...] = jnp.zeros_like(o_ref)

    zb = zb_scr[...]
    re = jnp.dot(are_ref[...], zb, preferred_element_type=F32)
    im = jnp.dot(aim_ref[...], zb, preferred_element_type=F32)
    p, r, s = p_ref[...], r_ref[...], s_ref[...]
    re2 = re * p - im * r
    im2 = re * r + im * s
    o_ref[...] += _dot(bre_ref[...], re2) + _dot(bim_ref[...], im2)

    @pl.when(f == pl.num_programs(2) - 1)
    def _():
        o_ref[...] = g_ref[...] * (o_ref[...] + z_ref[...] * bias_ref[...])


def _hy_conv_call(z, z_col, z_row0, g, g_col, g_row0, bias, a_tab, b_tab, prs, order, l, n_seq):
    c = GROUP_WIDTH
    ct = c
    ncb = c // ct
    kb = min(l, 512)
    nf = l // kb
    p, r, s = prs
    zspec = pl.BlockSpec((l, ct), lambda b, cb, f: (z_row0 // l + b, z_col * ncb + cb))
    gspec = pl.BlockSpec((l, ct), lambda b, cb, f: (g_row0 // l + b, g_col * ncb + cb))
    kspec = pl.BlockSpec((None, kb, ct), lambda b, cb, f: (order, f, cb))
    return pl.pallas_call(
        _hy_conv_kernel,
        out_shape=jax.ShapeDtypeStruct((n_seq * l, c), F32),
        grid=(n_seq, ncb, nf),
        in_specs=[
            zspec, gspec,
            pl.BlockSpec((None, 1, ct), lambda b, cb, f: (order, 0, cb)),
            pl.BlockSpec((kb, l), lambda b, cb, f: (f, 0)),
            pl.BlockSpec((kb, l), lambda b, cb, f: (f + nf, 0)),
            pl.BlockSpec((l, kb), lambda b, cb, f: (0, f)),
            pl.BlockSpec((l, kb), lambda b, cb, f: (0, f + nf)),
            kspec, kspec, kspec,
        ],
        out_specs=pl.BlockSpec((l, ct), lambda b, cb, f: (b, cb)),
        scratch_shapes=[pltpu.VMEM((l, ct), BF16)],
        compiler_params=_params("arbitrary", "arbitrary", "arbitrary"),
        name="hyena_conv",
    )(z, g, bias.reshape(HY_ORDER, 1, c), a_tab, a_tab, b_tab, b_tab, p, r, s)


CHUNKS_PER_STEP = 2
GDN_CHUNKS_PER_STEP = 4


def _scan_order(d, n=CHUNKS_PER_STEP):
    return list(range(n)) if d == 0 else list(range(n - 1, -1, -1))


def _chunk_tables(cfg, q):
    fwd, bwd, first, last, seq = [], [], [], [], []
    sid = 0
    for n_seq, l, row0 in ((cfg["n_dec"], cfg["ls"], 0), (cfg["n_ctx"], cfg["lp"], cfg["ts"])):
        nc = l // q
        for b in range(n_seq):
            base = (row0 + b * l) // q
            for c in range(nc):
                fwd.append(base + c)
                bwd.append(base + nc - 1 - c)
                first.append(int(c == 0))
                last.append(int(c == nc - 1))
                seq.append(sid)
            sid += 1
    return tuple(jnp.asarray(np.asarray(a, np.int32)) for a in (fwd, bwd, first, last, seq))


def _scan_masks(q, d):
    r = lax.broadcasted_iota(jnp.int32, (q, q), 0)
    c = lax.broadcasted_iota(jnp.int32, (q, q), 1)
    return ((r >= c), (r > c)) if d == 0 else ((r <= c), (r < c))


def _cumsum_pair(a_col, a_row, q, d):
    lo, up = _tri(q, True), _tri(q, False)
    if d == 0:
        return _dot_hi(lo, a_col), _dot_hi(a_row, up)
    return _dot_hi(up, a_col), _dot_hi(a_row, lo)


def _ssd_kernel(
    fwd_t, bwd_t, first_t, last_t, seq_t,
    xf_ref, bcf_ref, smf_ref, xb_ref, bcb_ref, smb_ref, pr_ref, pc_ref, dsk_ref, h0_ref,
    yf_ref, yb_ref, hout_ref, h_scr,
):
    s = pl.program_id(0)
    q, p, n = SSD_CHUNK, SSD_HEAD_DIM, SSD_STATE

    @pl.when(first_t[s] == 1)
    def _():
        h_scr[...] = h0_ref[...]

    bias_r, alog_r = pr_ref[0:1], pr_ref[1:2]
    bias_c, alog_c = pc_ref[:, 0:1], pc_ref[:, 1:2]
    assert q == LANES == n and 2 * p == LANES
    even_half = lax.broadcasted_iota(jnp.int32, (q, LANES), 1) < p
    probs = []
    for d in range(N_DIR):
        x_ref, bc_ref, sm_ref = (xf_ref, bcf_ref, smf_ref) if d == 0 else (xb_ref, bcb_ref, smb_ref)
        incl, _ = _scan_masks(q, d)
        end = q - 1 if d == 0 else 0
        for k in _scan_order(d):
            rows = slice(k * q, (k + 1) * q)
            x = _silu(x_ref[rows, :])
            x_t = x.T
            bc = _silu(bc_ref[rows, :])
            sm = sm_ref[rows, :]
            a_c = _softplus(sm + bias_r) * (-jnp.exp(alog_r))
            dt_r = _softplus(sm.T + bias_c)
            acs_c, acs_r = _cumsum_pair(a_c, (dt_r * (-jnp.exp(alog_c)))[SM_SDT:SM_SDT + N_DIR * SSD_HEADS], q, d)
            for g in range(SSD_GROUPS):
                bm = bc[:, g * n:(g + 1) * n]
                cm = bc[:, (SSD_GROUPS + g) * n:(SSD_GROUPS + g + 1) * n]
                cb = _dot_nt(cm, bm)
                for e in range(SSD_HPG):
                    hd = g * SSD_HPG + e
                    idx = d * SSD_HEADS + hd
                    pair = hd // 2
                    row = acs_r[idx:idx + 1, :]
                    dt_row = dt_r[SM_SDT + idx:SM_SDT + idx + 1, :]
                    tot = row[:, end:end + 1]
                    probs.append(dict(
                        d=d, k=k, hd=hd, bm=bm, cm=cm, cb=cb, incl=incl, row=row, dt_row=dt_row, tot=tot,
                        colb=jnp.broadcast_to(acs_c[:, idx:idx + 1], (q, q)),
                        x_pair=x[:, pair * LANES:(pair + 1) * LANES], x_t=x_t[hd * p:(hd + 1) * p, :],
                        skip=dsk_ref[d:d + 1, pair * LANES:(pair + 1) * LANES],
                    ))
    lmat = [jnp.exp(jnp.where(pb["incl"], pb["colb"] - pb["row"], -1e30)) * pb["dt_row"] for pb in probs]
    y_diag = [_dot(pb["cb"] * lm, pb["x_pair"]) for pb, lm in zip(probs, lmat)]
    states = [_dot(pb["x_t"] * (pb["dt_row"] * jnp.exp(pb["tot"] - pb["row"])), pb["bm"]) for pb in probs]
    c_dec = [pb["cm"] * jnp.exp(pb["colb"]) for pb in probs]
    state = {(d, hd): h_scr[d, hd] for d in range(N_DIR) for hd in range(SSD_HEADS)}
    ys = {}
    for step in range(CHUNKS_PER_STEP):
        group = [(i, pb) for i, pb in enumerate(probs) if _scan_order(pb["d"])[step] == pb["k"]]
        pair_state = {
            (d, pr_): jnp.concatenate([state[d, 2 * pr_], state[d, 2 * pr_ + 1]], axis=0)
            for d in range(N_DIR) for pr_ in range(SSD_HEADS // 2)
        }
        y_off = [_dot_nt(c_dec[i], pair_state[pb["d"], pb["hd"] // 2]) for i, pb in group]
        for (i, pb), yo in zip(group, y_off):
            ys[pb["d"], pb["k"], pb["hd"]] = y_diag[i] + yo
            state[pb["d"], pb["hd"]] = state[pb["d"], pb["hd"]] * jnp.exp(pb["tot"]) + states[i]
    for (d, hd), h_new in state.items():
        h_scr[d, hd] = h_new
    by_key = {(pb["d"], pb["k"], pb["hd"]): pb for pb in probs}
    for d, y_ref in ((0, yf_ref), (1, yb_ref)):
        for k in range(CHUNKS_PER_STEP):
            for pr_ in range(SSD_HEADS // 2):
                pb = by_key[d, k, 2 * pr_]
                y_pair = jnp.where(even_half, ys[d, k, 2 * pr_], ys[d, k, 2 * pr_ + 1]) + pb["skip"] * pb["x_pair"]
                y_ref[k * q:(k + 1) * q, pr_ * LANES:(pr_ + 1) * LANES] = y_pair.astype(BF16)

    @pl.when(last_t[s] == 1)
    def _():
        hout_ref[...] = h_scr[...]


def _ssd_call(proj, small, tables, h0, dt_bias, a_log, d_skip, cfg):
    q, w = CHUNKS_PER_STEP * SSD_CHUNK, GROUP_WIDTH
    t = proj.shape[0]
    n_steps = tables[0].shape[0]
    nlane = N_DIR * SSD_HEADS
    pr = jnp.zeros((SUBLANES, LANES), F32)
    pr = pr.at[0, SM_SDT:SM_SDT + nlane].set(dt_bias.reshape(-1)).at[1, SM_SDT:SM_SDT + nlane].set(a_log.reshape(-1))
    pc = pr.T
    dsk = jnp.repeat(d_skip, SSD_HEAD_DIM, axis=-1)
    st_shape = (N_DIR, SSD_HEADS, SSD_HEAD_DIM, SSD_STATE)
    fmap = lambda col: (lambda s, fw, bw, fi, la, sq: (fw[s], col))
    bmap = lambda col: (lambda s, fw, bw, fi, la, sq: (bw[s], col))
    const = lambda shape: pl.BlockSpec(shape, lambda s, *_: tuple(0 for _ in shape))
    st_spec = pl.BlockSpec((None,) + st_shape, lambda s, fw, bw, fi, la, sq: (sq[s], 0, 0, 0, 0))
    grid_spec = pltpu.PrefetchScalarGridSpec(
        num_scalar_prefetch=5,
        grid=(n_steps,),
        in_specs=[
            pl.BlockSpec((q, w), fmap(COL_S_X)), pl.BlockSpec((q, w), fmap(COL_S_BC)), pl.BlockSpec((q, LANES), fmap(0)),
            pl.BlockSpec((q, w), bmap(COL_S_X)), pl.BlockSpec((q, w), bmap(COL_S_BC)), pl.BlockSpec((q, LANES), bmap(0)),
            const((SUBLANES, LANES)), const((LANES, SUBLANES)), const((N_DIR, w)), st_spec,
        ],
        out_specs=[pl.BlockSpec((q, w), fmap(0)), pl.BlockSpec((q, w), bmap(0)), st_spec],
        scratch_shapes=[pltpu.VMEM(st_shape, F32)],
    )
    n_seq = cfg["n_dec"] + cfg["n_ctx"]
    return pl.pallas_call(
        _ssd_kernel,
        out_shape=[
            jax.ShapeDtypeStruct((t, w), BF16), jax.ShapeDtypeStruct((t, w), BF16),
            jax.ShapeDtypeStruct((n_seq,) + st_shape, F32),
        ],
        grid_spec=grid_spec,
        compiler_params=_params("arbitrary"),
        name="ssd_scan",
    )(*tables, proj, proj, small, proj, proj, small, pr, pc, dsk, h0)


TRI_BLOCK = 2 * SUBLANES


def _unit_tri_inverses(mats, lowers):
    n = mats[0].shape[0]
    blk = TRI_BLOCK
    sub = blk // SUBLANES
    lane = lax.broadcasted_iota(jnp.int32, (blk, n), 1)
    row = lax.broadcasted_iota(jnp.int32, (blk, n), 0)
    group = (lax.broadcasted_iota(jnp.int32, (SUBLANES, LANES), 1) // blk) * blk
    tiles = lambda v: [v[t * SUBLANES:(t + 1) * SUBLANES, :] for t in range(sub)]
    eye = jnp.where(lane % blk == row, 1.0, 0.0)
    d_t, inv_t = [], []
    for a in mats:
        packed = jnp.zeros((blk, n), F32)
        for bi in range(n // blk):
            packed = jnp.where(lane // blk == bi, a[bi * blk:(bi + 1) * blk, :], packed)
        if n < LANES:
            packed = jnp.concatenate([packed, jnp.zeros((blk, LANES - n), F32)], axis=1)
        d_t.append(tiles(packed))
        inv_t.append(tiles(eye))
    for step in range(blk):
        for i, lower in enumerate(lowers):
            jj = step if lower else blk - 1 - step
            st, jr = divmod(jj, SUBLANES)
            done_row = inv_t[i][st][jr:jr + 1, :]
            for s in (range(st, sub) if lower else range(st + 1)):
                col = jnp.take_along_axis(d_t[i][s], group + jj, axis=1)[:, :n]
                inv_t[i][s] = inv_t[i][s] - col * done_row
    ts = []
    for i in range(len(mats)):
        inv = jnp.concatenate(inv_t[i], axis=0)
        ts.append(jnp.concatenate([jnp.where(lane // blk == bi, inv, 0.0) for bi in range(n // blk)], axis=0))
    r = lax.broadcasted_iota(jnp.int32, (n, n), 0)
    c = lax.broadcasted_iota(jnp.int32, (n, n), 1)
    size = blk
    while size < n:
        level = (r // (2 * size) == c // (2 * size)) & (r // size != c // size)
        half = [_dot(t, jnp.where(level, a, 0.0)) for t, a in zip(ts, mats)]
        ts = [t - _dot(h, t) for t, h in zip(ts, half)]
        size *= 2
    return ts


def _l2norm_heads(x, heads, dim, scale):
    outs = []
    for h in range(heads):
        xh = x[:, h * dim:(h + 1) * dim]
        outs.append(xh * (lax.rsqrt(jnp.sum(xh * xh, axis=-1, keepdims=True) + 1e-6) * scale))
    return outs


def _gdn_kernel(
    fwd_t, bwd_t, first_t, last_t, seq_t,
    qf_ref, kf_ref, vf_ref, smf_ref, qb_ref, kb_ref, vb_ref, smb_ref, pr_ref, pc_ref, s0_ref,
    of_ref, ob_ref, sout_ref, s_scr,
):
    s = pl.program_id(0)
    q, dim = GDN_CHUNK, GDN_HEAD_DIM

    @pl.when(first_t[s] == 1)
    def _():
        s_scr[...] = s0_ref[...]

    bias_r, alog_r = pr_ref[0:1], pr_ref[1:2]
    bias_c, alog_c = pc_ref[:, 0:1], pc_ref[:, 1:2]
    n_sub = GDN_CHUNKS_PER_STEP
    probs = []
    for d in range(N_DIR):
        q_ref, k_ref, v_ref, sm_ref = (qf_ref, kf_ref, vf_ref, smf_ref) if d == 0 else (qb_ref, kb_ref, vb_ref, smb_ref)
        incl, strict = _scan_masks(q, d)
        end = q - 1 if d == 0 else 0
        for c in _scan_order(d, n_sub):
            rows = slice(c * q, (c + 1) * q)
            qs = _l2norm_heads(_silu(q_ref[rows, :]), GDN_HEADS, dim, dim ** -0.5)
            ks = _l2norm_heads(_silu(k_ref[rows, :]), GDN_HEADS, dim, 1.0)
            v = _silu(v_ref[rows, :])
            sm = sm_ref[rows, :]
            beta_c = jax.nn.sigmoid(sm)
            g_c = -jnp.exp(alog_r) * _softplus(sm + bias_r)
            g_r = -jnp.exp(alog_c) * _softplus(sm.T + bias_c)
            gc_c, gc_r = _cumsum_pair(g_c, g_r[SM_GA:SM_GA + N_DIR * GDN_HEADS], q, d)
            for h in range(GDN_HEADS):
                idx = d * GDN_HEADS + h
                row = gc_r[idx:idx + 1, :]
                probs.append(dict(
                    d=d, c=c, h=h, q=qs[h], k=ks[h], v=v[:, h * dim:(h + 1) * dim], row=row, tot=row[:, end:end + 1],
                    colb=jnp.broadcast_to(gc_c[:, SM_GA + idx:SM_GA + idx + 1], (q, dim)),
                    betab=jnp.broadcast_to(beta_c[:, SM_GBETA + idx:SM_GBETA + idx + 1], (q, dim)),
                    incl=incl, strict=strict,
                ))
    k16 = [p["k"].astype(BF16) for p in probs]
    kk = [_dot_nt(k, k) for k in k16]
    qk = [_dot_nt(p["q"], k) for p, k in zip(probs, k16)]
    dmask = [jnp.exp(jnp.where(p["incl"], p["colb"][:, :q] - p["row"], -1e30)) for p in probs]
    a = [jnp.where(p["strict"], p["betab"][:, :q] * kk_i * dm, 0.0) for p, kk_i, dm in zip(probs, kk, dmask)]
    ecol = [jnp.exp(p["colb"]) for p in probs]
    rhs = [
        jnp.concatenate([p["v"] * p["betab"], p["k"] * (p["betab"] * e)], axis=-1).astype(BF16)
        for p, e in zip(probs, ecol)
    ]
    inv = _unit_tri_inverses(a, [p["d"] == 0 for p in probs])
    sol = [_dot(t, r) for t, r in zip(inv, rhs)]
    u_sol = [x[:, :dim] for x in sol]
    w_sol = [x[:, dim:].astype(BF16) for x in sol]
    attn = [(qk_i * dm).astype(BF16) for qk_i, dm in zip(qk, dmask)]
    q_dec = [(p["q"] * e).astype(BF16) for p, e in zip(probs, ecol)]
    k_dec = [(p["k"] * jnp.exp(p["tot"] - p["colb"])).astype(BF16) for p in probs]
    state = {(d, h): s_scr[d, h] for d in range(N_DIR) for h in range(GDN_HEADS)}
    outs = {}
    for step in range(n_sub):
        group = [(i, p) for i, p in enumerate(probs) if _scan_order(p["d"], n_sub)[step] == p["c"]]
        w_s = [_dot(w_sol[i], state[p["d"], p["h"]]) for i, p in group]
        q_s = [_dot(q_dec[i], state[p["d"], p["h"]]) for i, p in group]
        vv = [u_sol[i] - ws for (i, p), ws in zip(group, w_s)]
        o_at = [_dot(attn[i], vv_i) for (i, p), vv_i in zip(group, vv)]
        upd = [_dot_tn(k_dec[i], vv_i) for (i, p), vv_i in zip(group, vv)]
        for (i, p), qs_i, oa, u_i in zip(group, q_s, o_at, upd):
            outs[p["d"], p["c"], p["h"]] = qs_i + oa
            state[p["d"], p["h"]] = state[p["d"], p["h"]] * jnp.exp(p["tot"]) + u_i
    for (d, h), s_new in state.items():
        s_scr[d, h] = s_new
    for d, o_ref in ((0, of_ref), (1, ob_ref)):
        for c in range(n_sub):
            o_ref[c * q:(c + 1) * q, :] = jnp.concatenate([outs[d, c, h] for h in range(GDN_HEADS)], axis=-1).astype(BF16)

    @pl.when(last_t[s] == 1)
    def _():
        sout_ref[...] = s_scr[...]


def _gdn_call(proj, small, tables, s0, a_log, dt_bias, cfg):
    q, w = GDN_CHUNKS_PER_STEP * GDN_CHUNK, GROUP_WIDTH
    t = proj.shape[0]
    n_steps = tables[0].shape[0]
    nlane = N_DIR * GDN_HEADS
    pr = jnp.zeros((SUBLANES, LANES), F32)
    pr = pr.at[0, SM_GA:SM_GA + nlane].set(dt_bias.reshape(-1)).at[1, SM_GA:SM_GA + nlane].set(a_log.reshape(-1))
    pc = pr.T
    st_shape = (N_DIR, GDN_HEADS, GDN_HEAD_DIM, GDN_HEAD_DIM)
    fmap = lambda col: (lambda s, fw, bw, fi, la, sq: (fw[s], col))
    bmap = lambda col: (lambda s, fw, bw, fi, la, sq: (bw[s], col))
    const = lambda shape: pl.BlockSpec(shape, lambda s, *_: tuple(0 for _ in shape))
    st_spec = pl.BlockSpec((None,) + st_shape, lambda s, fw, bw, fi, la, sq: (sq[s], 0, 0, 0, 0))
    blk = lambda m, col: pl.BlockSpec((q, w), m(col))
    grid_spec = pltpu.PrefetchScalarGridSpec(
        num_scalar_prefetch=5,
        grid=(n_steps,),
        in_specs=[
            blk(fmap, COL_G_Q), blk(fmap, COL_G_K), blk(fmap, COL_G_V), pl.BlockSpec((q, LANES), fmap(0)),
            blk(bmap, COL_G_Q), blk(bmap, COL_G_K), blk(bmap, COL_G_V), pl.BlockSpec((q, LANES), bmap(0)),
            const((SUBLANES, LANES)), const((LANES, SUBLANES)), st_spec,
        ],
        out_specs=[pl.BlockSpec((q, w), fmap(0)), pl.BlockSpec((q, w), bmap(0)), st_spec],
        scratch_shapes=[pltpu.VMEM(st_shape, F32)],
    )
    n_seq = cfg["n_dec"] + cfg["n_ctx"]
    return pl.pallas_call(
        _gdn_kernel,
        out_shape=[
            jax.ShapeDtypeStruct((t, w), BF16), jax.ShapeDtypeStruct((t, w), BF16),
            jax.ShapeDtypeStruct((n_seq,) + st_shape, F32),
        ],
        grid_spec=grid_spec,
        compiler_params=_params("arbitrary"),
        name="gdn_scan",
    )(*tables, proj, proj, proj, small, proj, proj, proj, small, pr, pc, s0)


def _rope_tables(cfg):
    l = cfg["ls"]
    pos = np.arange(l)
    r = (pos // GRID_W).astype(np.float32)
    col = (pos % GRID_W).astype(np.float32)
    nf = RET_HEAD_DIM // 4
    inv = np.power(np.float32(ROPE_BASE), -np.arange(nf, dtype=np.float32) / np.float32(nf)).astype(np.float32)
    ang = np.concatenate([r[:, None] * inv, col[:, None] * inv], -1)
    cos, sin = np.cos(ang), np.sin(ang)
    cos2 = np.concatenate([cos, cos], -1)
    sin2 = np.concatenate([-sin, sin], -1)
    ident = np.ones((RET_CHUNK, RET_HEAD_DIM), np.float32)
    return np.concatenate([cos2, ident], 0).astype(np.float32), np.concatenate([sin2, 0.0 * ident], 0).astype(np.float32)


def _rope_block_table(cfg):
    q = RET_CHUNK
    nc_s, nc_p = cfg["ls"] // q, cfg["lp"] // q
    fwd = [c for _ in range(cfg["n_dec"]) for c in range(nc_s)] + [nc_s] * (cfg["n_ctx"] * nc_p)
    bwd = [nc_s - 1 - c for _ in range(cfg["n_dec"]) for c in range(nc_s)] + [nc_s] * (cfg["n_ctx"] * nc_p)
    return jnp.asarray(np.asarray(fwd, np.int32)), jnp.asarray(np.asarray(bwd, np.int32))


def _ret_kernel(
    fwd_t, bwd_t, first_t, last_t, seq_t, rf_t, rb_t,
    qf_ref, kf_ref, vf_ref, cf_ref, sf_ref, qb_ref, kb_ref, vb_ref, cb_ref, sb_ref, dec_ref, s0_ref,
    of_ref, ob_ref, sout_ref, s_scr,
):
    s = pl.program_id(0)
    q, dim = RET_CHUNK, RET_HEAD_DIM

    @pl.when(first_t[s] == 1)
    def _():
        s_scr[...] = s0_ref[...]

    lg_all = -jnp.exp(dec_ref[...])
    ri = lax.broadcasted_iota(jnp.int32, (q, q), 0)
    ci = lax.broadcasted_iota(jnp.int32, (q, q), 1)
    rpos = lax.broadcasted_iota(jnp.int32, (q, dim), 0)
    probs = []
    for d in range(N_DIR):
        q_ref, k_ref, v_ref, c_ref, sn_ref = (
            (qf_ref, kf_ref, vf_ref, cf_ref, sf_ref) if d == 0 else (qb_ref, kb_ref, vb_ref, cb_ref, sb_ref)
        )
        qa, ka, va = q_ref[...], k_ref[...], v_ref[...]
        cos, sin = c_ref[...], sn_ref[...]
        rel = ((ri - ci) if d == 0 else (ci - ri)).astype(F32)
        pos = (rpos if d == 0 else (q - 1 - rpos)).astype(F32)
        for h in range(RET_HEADS):
            idx = d * RET_HEADS + h
            lg = lg_all[idx:idx + 1, :]
            qh = qa[:, h * dim:(h + 1) * dim]
            kh = ka[:, h * dim:(h + 1) * dim] * dim ** -0.5
            probs.append(dict(
                d=d, h=h, lg=lg, lg1=lg, rel=rel, pos=pos, v=va[:, h * dim:(h + 1) * dim],
                q=qh * cos + pltpu.roll(qh, dim // 2, axis=1) * sin,
                k=kh * cos + pltpu.roll(kh, dim // 2, axis=1) * sin,
            ))
    st = [s_scr[pb["d"], pb["h"]] for pb in probs]
    qk = [_dot_nt(pb["q"], pb["k"]) for pb in probs]
    dmask = [jnp.exp(jnp.where(pb["rel"] >= 0, pb["rel"] * pb["lg"], -1e30)) for pb in probs]
    o_state = [_dot(pb["q"] * jnp.exp((pb["pos"] + 1.0) * pb["lg1"]), s_i) for pb, s_i in zip(probs, st)]
    o_attn = [_dot(qk_i * dm, pb["v"]) for pb, qk_i, dm in zip(probs, qk, dmask)]
    upd = [_dot_tn(pb["k"] * jnp.exp((q - 1.0 - pb["pos"]) * pb["lg1"]), pb["v"]) for pb in probs]
    for pb, s_i, u_i in zip(probs, st, upd):
        s_scr[pb["d"], pb["h"]] = s_i * jnp.exp(q * pb["lg"]) + u_i
    outs = [a + b for a, b in zip(o_state, o_attn)]
    of_ref[...] = jnp.concatenate(outs[:RET_HEADS], axis=-1).astype(BF16)
    ob_ref[...] = jnp.concatenate(outs[RET_HEADS:], axis=-1).astype(BF16)

    @pl.when(last_t[s] == 1)
    def _():
        sout_ref[...] = s_scr[...]


def _ret_call(proj, tables, rope_tabs, rope_blocks, s0, ret_decay, cfg):
    q, w, dim = RET_CHUNK, GROUP_WIDTH, RET_HEAD_DIM
    t = proj.shape[0]
    n_steps = tables[0].shape[0]
    dec = jnp.broadcast_to(ret_decay.reshape(-1, 1), (N_DIR * RET_HEADS, LANES))
    st_shape = (N_DIR, RET_HEADS, dim, dim)
    fmap = lambda col: (lambda s, fw, bw, fi, la, sq, rf, rb: (fw[s], col))
    bmap = lambda col: (lambda s, fw, bw, fi, la, sq, rf, rb: (bw[s], col))
    rfmap = lambda s, fw, bw, fi, la, sq, rf, rb: (rf[s], 0)
    rbmap = lambda s, fw, bw, fi, la, sq, rf, rb: (rb[s], 0)
    st_spec = pl.BlockSpec((None,) + st_shape, lambda s, fw, bw, fi, la, sq, rf, rb: (sq[s], 0, 0, 0, 0))
    blk = lambda m, col: pl.BlockSpec((q, w), m(col))
    rope = lambda m: pl.BlockSpec((q, dim), m)
    grid_spec = pltpu.PrefetchScalarGridSpec(
        num_scalar_prefetch=7,
        grid=(n_steps,),
        in_specs=[
            blk(fmap, COL_R_Q), blk(fmap, COL_R_K), blk(fmap, COL_R_V), rope(rfmap), rope(rfmap),
            blk(bmap, COL_R_Q), blk(bmap, COL_R_K), blk(bmap, COL_R_V), rope(rbmap), rope(rbmap),
            pl.BlockSpec((N_DIR * RET_HEADS, LANES), lambda s, *_: (0, 0)), st_spec,
        ],
        out_specs=[pl.BlockSpec((q, w), fmap(0)), pl.BlockSpec((q, w), bmap(0)), st_spec],
        scratch_shapes=[pltpu.VMEM(st_shape, F32)],
    )
    n_seq = cfg["n_dec"] + cfg["n_ctx"]
    cos2, sin2 = rope_tabs
    return pl.pallas_call(
        _ret_kernel,
        out_shape=[
            jax.ShapeDtypeStruct((t, w), BF16), jax.ShapeDtypeStruct((t, w), BF16),
            jax.ShapeDtypeStruct((n_seq,) + st_shape, F32),
        ],
        grid_spec=grid_spec,
        compiler_params=_params("arbitrary"),
        name="retention_scan",
    )(*tables, *rope_blocks, proj, proj, proj, cos2, sin2, proj, proj, proj, cos2, sin2, dec, s0)


def _layer_norm_rows(y, g, b):
    mu = jnp.mean(y, axis=-1, keepdims=True)
    yc = y - mu
    var = jnp.mean(yc * yc, axis=-1, keepdims=True)
    return yc * lax.rsqrt(var + LN_EPS) * g + b


OUTPROJ_ROW_CHUNK = 256


def _outproj_kernel(
    hys_ref, hyp_ref, sf_ref, sb_ref, sz_ref, gf_ref, gb_ref, gz_ref, rf_ref, rb_ref, rg_ref,
    snw_ref, gnw_ref, w_ref, xs_ref, xp_ref, g1_ref, sh2_ref, sc2_ref, lng_ref, lnb_ref,
    x1_ref, u2_ref, lhs_scr, *, n_sample_tiles, alpha, two_group_x,
):
    i = pl.program_id(0)
    is_sample = i < n_sample_tiles
    w = GROUP_WIDTH
    tm = x1_ref.shape[0]
    rc = min(OUTPROJ_ROW_CHUNK, tm)
    chunks = [slice(r, r + rc) for r in range(0, tm, rc)]
    gnw = gnw_ref[...]
    for rows in chunks:
        y_hy = jnp.where(is_sample, hys_ref[rows, :], hyp_ref[rows, :])
        y_ssd = (sf_ref[rows, :].astype(F32) + sb_ref[rows, :].astype(F32)) * _silu(sz_ref[rows, :])
        y_ssd = y_ssd * lax.rsqrt(jnp.mean(y_ssd * y_ssd, axis=-1, keepdims=True) + RMS_EPS) * snw_ref[...]
        lhs_scr[rows, 0:w] = y_hy.astype(BF16)
        lhs_scr[rows, w:2 * w] = y_ssd.astype(BF16)
        og = gf_ref[rows, :].astype(F32) + gb_ref[rows, :].astype(F32)
        orr = rf_ref[rows, :].astype(F32) + rb_ref[rows, :].astype(F32)
        gz = _silu(gz_ref[rows, :])
        rg = _silu(rg_ref[rows, :])
        for h in range(GDN_HEADS):
            sl = slice(h * GDN_HEAD_DIM, (h + 1) * GDN_HEAD_DIM)
            o = og[:, sl]
            o = o * lax.rsqrt(jnp.mean(o * o, axis=-1, keepdims=True) + RMS_EPS) * gnw * gz[:, sl]
            lhs_scr[rows, 2 * w + h * GDN_HEAD_DIM:2 * w + (h + 1) * GDN_HEAD_DIM] = o.astype(BF16)
        for h in range(RET_HEADS):
            sl = slice(h * RET_HEAD_DIM, (h + 1) * RET_HEAD_DIM)
            o = orr[:, sl]
            mu = jnp.mean(o, axis=-1, keepdims=True)
            oc = o - mu
            o = oc * lax.rsqrt(jnp.mean(oc * oc, axis=-1, keepdims=True) + LN_EPS) * rg[:, sl]
            lhs_scr[rows, 3 * w + h * RET_HEAD_DIM:3 * w + (h + 1) * RET_HEAD_DIM] = o.astype(BF16)
    accs = [jnp.dot(lhs_scr[rows, :], w_ref[...], preferred_element_type=F32) for rows in chunks]
    for rows, acc in zip(chunks, accs):
        x = jnp.where(is_sample, xs_ref[rows, :], xp_ref[rows, :]) if two_group_x else xs_ref[rows, :]
        x1 = _layer_norm_rows(alpha * x + g1_ref[...] * acc, lng_ref[...], lnb_ref[...])
        x1_ref[rows, :] = x1
        u2_ref[rows, :] = (x1 * (1.0 + sc2_ref[...]) + sh2_ref[...]).astype(BF16)


def _outproj_call(hy_s, hy_p, ssd, gdn, ret, proj, ssd_nw, gdn_nw, w_out, x_in, mod4, layer, ln_g, ln_b, cfg):
    t, d, w = cfg["t"], cfg["d"], GROUP_WIDTH
    tm = 2 * OUTPROJ_ROW_CHUNK
    ns = cfg["ts"] // tm
    two_group_x = isinstance(x_in, tuple)
    row = lambda col: pl.BlockSpec((tm, w), lambda i: (i, col))
    vec = lambda n: pl.BlockSpec((None, 1, n), lambda i: (layer, 0, 0))
    mspec = lambda k: _mod_spec(layer, k, d, tm, cfg["ls"], cfg["n_dec"])
    if two_group_x:
        x_specs = _two_group_specs(tm, d, ns, pipeline_mode=pl.Buffered(1))
        x_args = list(x_in)
    else:
        x_specs = [pl.BlockSpec((tm, d), lambda i: (i, 0)), pl.BlockSpec((SUBLANES, d), lambda i: (0, 0))]
        x_args = [x_in, x_in]
    kern = functools.partial(_outproj_kernel, n_sample_tiles=ns, alpha=cfg["alpha"], two_group_x=two_group_x)
    return pl.pallas_call(
        kern,
        out_shape=[jax.ShapeDtypeStruct((t, d), F32), jax.ShapeDtypeStruct((t, d), BF16)],
        grid=(t // tm,),
        in_specs=_two_group_specs(tm, w, ns)
        + [row(0), row(0), row(COL_S_Z), row(0), row(0), row(COL_G_Z), row(0), row(0), row(COL_R_G)]
        + [vec(w), vec(GDN_HEAD_DIM),
           pl.BlockSpec((None, d, d), lambda i: (layer, 0, 0), pipeline_mode=pl.Buffered(1))]
        + x_specs
        + [mspec(2), mspec(3), mspec(4), vec(d), vec(d)],
        out_specs=[pl.BlockSpec((tm, d), lambda i: (i, 0)), pl.BlockSpec((tm, d), lambda i: (i, 0))],
        scratch_shapes=[pltpu.VMEM((tm, d), BF16)],
        compiler_params=_params("arbitrary"),
        name="out_proj_ln",
    )(
        hy_s, hy_p, ssd[0], ssd[1], proj, gdn[0], gdn[1], proj, ret[0], ret[1], proj,
        ssd_nw, gdn_nw, w_out, *x_args, mod4, mod4, mod4, ln_g, ln_b,
    )


def _ffn_up_kernel(u_ref, wg_ref, wv_ref, cwg_ref, cbg_ref, cwv_ref, cbv_ref, o_ref, w_scr, h_scr, *, n_sample_tiles, ls, lp):
    i = pl.program_id(1)
    tn = wg_ref.shape[1]
    seq_len = jnp.where(i < n_sample_tiles, ls, lp)

    @pl.when(i == 0)
    def _():
        w_scr[:, :tn] = wg_ref[...].astype(BF16)
        w_scr[:, tn:] = wv_ref[...].astype(BF16)

    cw = jnp.concatenate([cwg_ref[...], cwv_ref[...]], axis=-1)
    cb = jnp.concatenate([cbg_ref[...], cbv_ref[...]], axis=-1)

    def finish(rows, conv):
        o_ref[rows, :] = (_silu(conv[:, :tn]) * conv[:, tn:]).astype(BF16)

    _matmul_conv_chunks(u_ref, w_scr[...], cw, cb, h_scr, seq_len, min(ls, lp), finish)


def _ffn_up_call(u2, w_up, cw, cb, layer, cfg):
    t, d = u2.shape
    dff = w_up.shape[2] // 2
    tm, tn = cfg["ls"], 256
    nj = dff // tn
    kern = functools.partial(_ffn_up_kernel, n_sample_tiles=cfg["ts"] // tm, ls=cfg["ls"], lp=cfg["lp"])
    gcol = lambda j, i: (layer, 0, j)
    vcol = lambda j, i: (layer, 0, j + nj)
    return pl.pallas_call(
        kern,
        out_shape=jax.ShapeDtypeStruct((t, dff), BF16),
        grid=(nj, t // tm),
        in_specs=[
            pl.BlockSpec((tm, d), lambda j, i: (i, 0)),
            pl.BlockSpec((None, d, tn), gcol), pl.BlockSpec((None, d, tn), vcol),
            pl.BlockSpec((None, CONV_W, tn), gcol), pl.BlockSpec((None, 1, tn), gcol),
            pl.BlockSpec((None, CONV_W, tn), vcol), pl.BlockSpec((None, 1, tn), vcol),
        ],
        out_specs=pl.BlockSpec((tm, tn), lambda j, i: (i, j)),
        scratch_shapes=[pltpu.VMEM((d, 2 * tn), BF16), pltpu.VMEM((tm + 2 * SUBLANES, 2 * tn), F32)],
        compiler_params=_params("arbitrary", "arbitrary"),
        name="ffn_up_conv_glu",
    )(u2, w_up, w_up, cw, cb, cw, cb)


FFN_DOWN_ROW_CHUNK = 256


def _ffn_down_kernel(a_ref, w_ref, x1_ref, g2_ref, shn_ref, scn_ref, lng_ref, lnb_ref, oa_ref, ob_ref, acc_scr,
                     *, alpha, n_sample_tiles, last_layer, n_k):
    i = pl.program_id(0)
    k = pl.program_id(1)
    tm = acc_scr.shape[0]

    if n_k > 1:
        @pl.when(k == 0)
        def _():
            acc_scr[...] = jnp.dot(a_ref[...], w_ref[...], preferred_element_type=F32)

        @pl.when(jnp.logical_and(k > 0, k < n_k - 1))
        def _():
            acc_scr[...] += jnp.dot(a_ref[...], w_ref[...], preferred_element_type=F32)

    @pl.when(k == n_k - 1)
    def _():
        rc = min(FFN_DOWN_ROW_CHUNK, tm)
        chunks = [slice(r, r + rc) for r in range(0, tm, rc)]
        parts = [jnp.dot(a_ref[rows, :], w_ref[...], preferred_element_type=F32) for rows in chunks]
        for rows, part in zip(chunks, parts):
            f = part + acc_scr[rows, :] if n_k > 1 else part
            x2 = _layer_norm_rows(alpha * x1_ref[rows, :] + g2_ref[...] * f, lng_ref[...], lnb_ref[...])
            if last_layer:
                @pl.when(i < n_sample_tiles)
                def _():
                    oa_ref[rows, :] = x2

                @pl.when(i >= n_sample_tiles)
                def _():
                    ob_ref[rows, :] = x2
            else:
                oa_ref[rows, :] = x2
                ob_ref[rows, :] = (x2 * (1.0 + scn_ref[...]) + shn_ref[...]).astype(BF16)


def _ffn_down_call(act, w_down, x1, mod4, layer, ln_g, ln_b, cfg):
    t, dff = act.shape
    d = cfg["d"]
    depth = w_down.shape[0]
    last_layer = layer == depth - 1
    next_layer = min(layer + 1, depth - 1)
    tm = 512
    tk = dff // 4 if (dff // 4) % LANES == 0 else dff
    ns = cfg["ts"] // tm
    vec = lambda: pl.BlockSpec((None, 1, d), lambda i, k: (layer, 0, 0))
    mspec = lambda lay, kk: _mod_spec(lay, kk, d, tm, cfg["ls"], cfg["n_dec"])
    if last_layer:
        out_shape = [jax.ShapeDtypeStruct((cfg["ts"], d), F32), jax.ShapeDtypeStruct((cfg["tp"], d), F32)]
        out_specs = [
            pl.BlockSpec((tm, d), lambda i, k: (jnp.minimum(i, ns - 1), 0)),
            pl.BlockSpec((tm, d), lambda i, k: (jnp.maximum(i - ns, 0), 0)),
        ]
    else:
        out_shape = [jax.ShapeDtypeStruct((t, d), F32), jax.ShapeDtypeStruct((t, d), BF16)]
        out_specs = [pl.BlockSpec((tm, d), lambda i, k: (i, 0)), pl.BlockSpec((tm, d), lambda i, k: (i, 0))]
    return pl.pallas_call(
        functools.partial(_ffn_down_kernel, alpha=cfg["alpha"], n_sample_tiles=ns, last_layer=last_layer, n_k=dff // tk),
        out_shape=out_shape,
        grid=(t // tm, dff // tk),
        in_specs=[
            pl.BlockSpec((tm, tk), lambda i, k: (i, k)),
            pl.BlockSpec((None, tk, d), lambda i, k: (layer, k, 0)),
            pl.BlockSpec((tm, d), lambda i, k: (i, 0)),
            mspec(layer, 5), mspec(next_layer, 0), mspec(next_layer, 1), vec(), vec(),
        ],
        out_specs=out_specs,
        scratch_shapes=[pltpu.VMEM((tm, d), F32)],
        compiler_params=_params("arbitrary", "arbitrary"),
        name="ffn_down_ln",
    )(act, w_down, x1, mod4, mod4, mod4, ln_g, ln_b)


def _reorder_w_in(w_in):
    w = GROUP_WIDTH
    o = N_CONV
    s_z = w_in[..., o:o + w]
    o += w
    s_dt = w_in[..., o:o + N_DIR * SSD_HEADS]
    o += N_DIR * SSD_HEADS
    g_z = w_in[..., o:o + w]
    o += w
    g_beta = w_in[..., o:o + N_DIR * GDN_HEADS]
    o += N_DIR * GDN_HEADS
    g_a = w_in[..., o:o + N_DIR * GDN_HEADS]
    o += N_DIR * GDN_HEADS
    rest = w_in[..., o:]
    main = jnp.concatenate([w_in[..., :N_CONV], s_z, g_z, rest], axis=-1).astype(BF16)
    small = jnp.concatenate([s_dt, g_beta, g_a], axis=-1)
    small = jnp.pad(small, ((0, 0), (0, 0), (0, LANES - small.shape[-1]))).astype(BF16)
    return main, small


def kernel(x_prompt, x_sample, state_ssd, state_gdn, state_ret, c, c_ctx, w_mod, b_mod, w_in, conv_w, conv_b,
           hy_w1, hy_b1, hy_w2, hy_b2, hy_w3, hy_freq, hy_bias, ssd_A_log, ssd_dt_bias, ssd_D, ssd_norm_w,
           gdn_A_log, gdn_dt_bias, gdn_norm_w, ret_decay, w_out, ln1_g, ln1_b, w_up, ffn_conv_w, ffn_conv_b,
           w_down, ln2_g, ln2_b):
    n_ctx, lp, d = x_prompt.shape
    n_dec, ls, _ = x_sample.shape
    depth = w_in.shape[0]
    ts, tp = n_dec * ls, n_ctx * lp
    assert ls & (ls - 1) == 0 and lp & (lp - 1) == 0 and tp % ls == 0 and ls % lp == 0
    assert d == 4 * GROUP_WIDTH and w_in.shape[2] == N_MAIN + 2 * N_DIR * (SSD_HEADS // 2 + GDN_HEADS)
    cfg = dict(d=d, ls=ls, lp=lp, n_dec=n_dec, n_ctx=n_ctx, ts=ts, tp=tp, t=ts + tp, alpha=(2 * depth) ** 0.25)

    xs = x_sample.reshape(ts, d)
    xp = x_prompt.reshape(tp, d)

    mod_rows = -(-(n_dec + 1) // SUBLANES) * SUBLANES
    cond = jnp.concatenate([c, c_ctx[None, :], jnp.zeros((mod_rows - n_dec - 1, d), F32)], axis=0)
    mod = _mod_call(cond, w_mod, b_mod)
    mod4 = mod.reshape(depth, mod_rows, 1, 6 * d)

    w_main, w_small = _reorder_w_in(w_in)
    w_out_b, w_down_b = w_out.astype(BF16), w_down.astype(BF16)
    stack_rows = lambda v: v[:, None, :]

    dft = {l: tuple(jnp.asarray(tab).astype(BF16) for tab in _dft_tables(l)) for l in (ls, lp)}
    rope_tabs = _rope_tables(cfg)
    rope_blocks = _rope_block_table(cfg)
    ssd_rows, gdn_rows = CHUNKS_PER_STEP * SSD_CHUNK, GDN_CHUNKS_PER_STEP * GDN_CHUNK
    tabs = {q: _chunk_tables(cfg, q) for q in sorted({ssd_rows, gdn_rows, RET_CHUNK})}

    def init_state(st, l):
        zeros = jnp.zeros((n_ctx,) + st.shape[2:], F32)
        return jnp.concatenate([st[:, l], zeros], axis=0)

    u = _modulate_call(xs, xp, mod4, 0, cfg)
    x_res = (xs, xp)
    new_ssd, new_gdn, new_ret = [], [], []
    for l in range(depth):
        proj, small = _inproj_call(u, w_main, w_small, conv_w, stack_rows(conv_b), l, cfg)

        hy = {}
        for name, seq_len, n_seq, row0 in (("s", ls, n_dec, 0), ("p", lp, n_ctx, ts)):
            a_tab, b_tab = dft[seq_len]
            prs = _hy_filter_call(seq_len, a_tab, hy_w1[l], hy_b1[l], hy_w2[l], hy_b2[l], hy_w3[l], hy_freq[l])
            z1 = _hy_conv_call(proj, COL_HY_V, row0, proj, COL_HY_X1, row0, hy_bias[l], a_tab, b_tab, prs, 0, seq_len, n_seq)
            hy[name] = _hy_conv_call(z1, 0, 0, proj, COL_HY_X2, row0, hy_bias[l], a_tab, b_tab, prs, 1, seq_len, n_seq)

        ssd = _ssd_call(proj, small, tabs[ssd_rows], init_state(state_ssd, l), ssd_dt_bias[l], ssd_A_log[l], ssd_D[l], cfg)
        gdn = _gdn_call(proj, small, tabs[gdn_rows], init_state(state_gdn, l), gdn_A_log[l], gdn_dt_bias[l], cfg)
        ret = _ret_call(proj, tabs[RET_CHUNK], rope_tabs, rope_blocks, init_state(state_ret, l), ret_decay[l], cfg)
        new_ssd.append(ssd[2][n_dec:])
        new_gdn.append(gdn[2][n_dec:])
        new_ret.append(ret[2][n_dec:])

        x1, u2 = _outproj_call(hy["s"], hy["p"], ssd, gdn, ret, proj, stack_rows(ssd_norm_w), stack_rows(gdn_norm_w),
                               w_out_b, x_res, mod4, l, stack_rows(ln1_g), stack_rows(ln1_b), cfg)
        act = _ffn_up_call(u2, w_up, ffn_conv_w, stack_rows(ffn_conv_b), l, cfg)
        x_res, u = _ffn_down_call(act, w_down_b, x1, mod4, l, stack_rows(ln2_g), stack_rows(ln2_b), cfg)

    y_sample = x_res.reshape(n_dec, ls, d)
    y_prompt = u.reshape(n_ctx, lp, d)
    return (y_prompt, y_sample, jnp.stack(new_ssd, 1), jnp.stack(new_gdn, 1), jnp.stack(new_ret, 1))
```

```python
import functools
import math

import numpy as np
import jax
import jax.numpy as jnp
from jax import lax
from jax.experimental import pallas as pl
from jax.experimental.pallas import tpu as pltpu

F32 = jnp.float32
BF16 = jnp.bfloat16
HIGHEST = lax.Precision.HIGHEST

N_DIR = 2
CONV_W = 3
GROUP_WIDTH = 512
HY_ORDER = 2
HY_BANDS = 8
HY_EMB = 1 + 2 * HY_BANDS
HY_FFN = 64
HY_DECAY_TARGET = 1e-2
HY_FAST_PCT = 0.3
HY_SLOW_PCT = 1.5
SSD_HEADS = 8
SSD_HEAD_DIM = 64
SSD_GROUPS = 2
SSD_HPG = SSD_HEADS // SSD_GROUPS
SSD_STATE = 128
SSD_CHUNK = 128
GDN_HEADS = 4
GDN_HEAD_DIM = 128
GDN_CHUNK = 64
RET_HEADS = 4
RET_HEAD_DIM = 128
RET_CHUNK = 128
GRID_W = 64
ROPE_BASE = 10000.0
LN_EPS = 1e-5
RMS_EPS = 1e-6
N_CONV = 8 * GROUP_WIDTH

LANES = 128
SUBLANES = 8
VMEM_LIMIT_BYTES = 60 * 1024 * 1024

COL_HY_V, COL_HY_X1, COL_HY_X2, COL_S_X, COL_S_BC, COL_G_Q, COL_G_K, COL_G_V = range(8)
COL_S_Z, COL_G_Z, COL_R_Q, COL_R_K, COL_R_V, COL_R_G = range(8, 14)
N_MAIN = 14 * GROUP_WIDTH
SM_SDT = 0
SM_GBETA = 16
SM_GA = 24


def _silu(x):
    return x * jax.nn.sigmoid(x)


def _softplus(x):
    return jnp.maximum(x, 0.0) + jnp.log1p(jnp.exp(-jnp.abs(x)))


def _dot(a, b):
    return jnp.dot(a.astype(BF16), b.astype(BF16), preferred_element_type=F32)


def _dot_nt(a, b):
    return lax.dot_general(a.astype(BF16), b.astype(BF16), (((1,), (1,)), ((), ())), preferred_element_type=F32)


def _dot_tn(a, b):
    return lax.dot_general(a.astype(BF16), b.astype(BF16), (((0,), (0,)), ((), ())), preferred_element_type=F32)


def _dot_hi(a, b):
    return jnp.dot(a, b, preferred_element_type=F32, precision=HIGHEST)


def _params(*semantics):
    return pltpu.CompilerParams(dimension_semantics=semantics, vmem_limit_bytes=VMEM_LIMIT_BYTES)


def _tri(n, lower):
    r = lax.broadcasted_iota(jnp.int32, (n, n), 0)
    c = lax.broadcasted_iota(jnp.int32, (n, n), 1)
    return jnp.where((r >= c) if lower else (r <= c), 1.0, 0.0).astype(F32)


def _mod_kernel(c_ref, w_ref, b_ref, o_ref):
    a = _silu(c_ref[...])
    o_ref[...] = _dot(a, w_ref[...]) + b_ref[...]


def _mod_call(cond, w_mod, b_mod):
    depth, d, n = w_mod.shape
    rows = cond.shape[0]
    tn = 1024
    return pl.pallas_call(
        _mod_kernel,
        out_shape=jax.ShapeDtypeStruct((depth, rows, n), F32),
        grid=(depth, n // tn),
        in_specs=[
            pl.BlockSpec((rows, d), lambda l, j: (0, 0)),
            pl.BlockSpec((None, d, tn), lambda l, j: (l, 0, j)),
            pl.BlockSpec((None, 1, tn), lambda l, j: (l, 0, j)),
        ],
        out_specs=pl.BlockSpec((None, rows, tn), lambda l, j: (l, 0, j)),
        compiler_params=_params("arbitrary", "arbitrary"),
        name="adaln_mod",
    )(cond, w_mod, b_mod.reshape(depth, 1, n))


def _mod_spec(layer, k, d, rows_per_tile, l_sample, n_dec):
    tiles_per_seq = l_sample // rows_per_tile

    def index_map(i, *_):
        return (layer, jnp.minimum(i // tiles_per_seq, n_dec), 0, k)

    return pl.BlockSpec((None, None, 1, d), index_map)


def _modulate_kernel(xs_ref, xp_ref, sh_ref, sc_ref, o_ref, *, n_sample_tiles):
    i = pl.program_id(0)
    x = jnp.where(i < n_sample_tiles, xs_ref[...], xp_ref[...])
    o_ref[...] = (x * (1.0 + sc_ref[...]) + sh_ref[...]).astype(BF16)


def _two_group_specs(tm, d, n_sample_tiles, **spec_kwargs):
    return [
        pl.BlockSpec((tm, d), lambda i, *_: (jnp.minimum(i, n_sample_tiles - 1), 0), **spec_kwargs),
        pl.BlockSpec((tm, d), lambda i, *_: (jnp.maximum(i - n_sample_tiles, 0), 0), **spec_kwargs),
    ]


def _modulate_call(xs, xp, mod4, layer, cfg):
    tm, d = 256, cfg["d"]
    t = cfg["t"]
    ns = cfg["ts"] // tm
    return pl.pallas_call(
        functools.partial(_modulate_kernel, n_sample_tiles=ns),
        out_shape=jax.ShapeDtypeStruct((t, d), BF16),
        grid=(t // tm,),
        in_specs=_two_group_specs(tm, d, ns)
        + [_mod_spec(layer, 0, d, tm, cfg["ls"], cfg["n_dec"]), _mod_spec(layer, 1, d, tm, cfg["ls"], cfg["n_dec"])],
        out_specs=pl.BlockSpec((tm, d), lambda i: (i, 0)),
        compiler_params=_params("arbitrary"),
        name="modulate_in",
    )(xs, xp, mod4, mod4)


CONV_ROW_CHUNK = 256


def _matmul_conv_chunks(u_ref, w, cw, cb, h_scr, seq_len, min_seq_len, finish):
    rows = u_ref.shape[0]
    rc = min(CONV_ROW_CHUNK, rows)
    assert rows % rc == 0 and rc % min_seq_len == 0
    n_chunks = rows // rc
    n = w.shape[1]
    halo = SUBLANES
    first = lax.broadcasted_iota(jnp.int32, (SUBLANES, 1), 0) == 0
    last = lax.broadcasted_iota(jnp.int32, (SUBLANES, 1), 0) == SUBLANES - 1
    tile = lambda v, t: v[t * SUBLANES:(t + 1) * SUBLANES]
    h_scr[0:halo, :] = jnp.zeros((halo, n), F32)
    h_scr[halo + rows:2 * halo + rows, :] = jnp.zeros((halo, n), F32)
    for r in range(n_chunks + 1):
        if r < n_chunks:
            h_scr[halo + r * rc:halo + (r + 1) * rc, :] = jnp.dot(u_ref[r * rc:(r + 1) * rc, :], w, preferred_element_type=F32)
        if r == 0:
            continue
        row0 = (r - 1) * rc
        h = h_scr[halo + row0:halo + row0 + rc, :]
        hp = h_scr[halo + row0 - 1:halo + row0 - 1 + rc, :]
        hn = h_scr[halo + row0 + 1:halo + row0 + 1 + rc, :]
        fix_p, fix_n = {}, {}
        for b in range(0, rc, min_seq_len):
            seq_start = ((row0 + b) & (seq_len - 1)) == 0
            seq_end = ((row0 + b + min_seq_len) & (seq_len - 1)) == 0
            tp, tn_ = b // SUBLANES, (b + min_seq_len) // SUBLANES - 1
            fix_p[tp] = jnp.where(jnp.logical_and(first, seq_start), 0.0, tile(hp, tp))
            fix_n[tn_] = jnp.where(jnp.logical_and(last, seq_end), 0.0, tile(hn, tn_))
        n_tiles = rc // SUBLANES
        hp = jnp.concatenate([fix_p.get(t, tile(hp, t)) for t in range(n_tiles)], axis=0)
        hn = jnp.concatenate([fix_n.get(t, tile(hn, t)) for t in range(n_tiles)], axis=0)
        finish(slice(row0, row0 + rc), hp * cw[0:1] + h * cw[1:2] + hn * cw[2:3] + cb)


def _inproj_kernel(u_ref, w_ref, ws_ref, cw_ref, cb_ref, o_ref, os_ref, h_scr, *, n_conv_tiles, n_sample_tiles, ls, lp):
    i = pl.program_id(0)
    j = pl.program_id(1)
    seq_len = jnp.where(i < n_sample_tiles, ls, lp)

    @pl.when(j < n_conv_tiles)
    def _():
        def finish(rows, conv):
            o_ref[rows, :] = conv

        _matmul_conv_chunks(u_ref, w_ref[...], cw_ref[...], cb_ref[...], h_scr, seq_len, min(ls, lp), finish)

    @pl.when(j >= n_conv_tiles)
    def _():
        o_ref[...] = jnp.dot(u_ref[...], w_ref[...], preferred_element_type=F32)

    @pl.when(j == 0)
    def _():
        os_ref[...] = jnp.dot(u_ref[...], ws_ref[...], preferred_element_type=F32)


def _inproj_call(u, w_main, w_small, cw, cb, layer, cfg):
    t, d = u.shape
    tm, tn = cfg["ls"], 512
    n_conv_tiles = N_CONV // tn
    kern = functools.partial(
        _inproj_kernel, n_conv_tiles=n_conv_tiles, n_sample_tiles=cfg["ts"] // tm, ls=cfg["ls"], lp=cfg["lp"]
    )
    conv_col = lambda i, j: (layer, 0, jnp.minimum(j, n_conv_tiles - 1))
    return pl.pallas_call(
        kern,
        out_shape=[jax.ShapeDtypeStruct((t, N_MAIN), F32), jax.ShapeDtypeStruct((t, LANES), F32)],
        grid=(t // tm, N_MAIN // tn),
        in_specs=[
            pl.BlockSpec((tm, d), lambda i, j: (i, 0)),
            pl.BlockSpec((None, d, tn), lambda i, j: (layer, 0, j)),
            pl.BlockSpec((None, d, LANES), lambda i, j: (layer, 0, 0)),
            pl.BlockSpec((None, CONV_W, tn), conv_col),
            pl.BlockSpec((None, 1, tn), conv_col),
        ],
        out_specs=[pl.BlockSpec((tm, tn), lambda i, j: (i, j)), pl.BlockSpec((tm, LANES), lambda i, j: (i, 0))],
        scratch_shapes=[pltpu.VMEM((tm + 2 * SUBLANES, tn), F32)],
        compiler_params=_params("arbitrary", "arbitrary"),
        name="in_proj_conv",
    )(u, w_main, w_small, cw, cb)


@functools.lru_cache(maxsize=None)
def _dft_tables(l):
    n = 2 * l
    k = np.arange(l, dtype=np.int64)
    ang = (2.0 * math.pi / n) * ((k[:, None] * k[None, :]) % n)
    cos, sin = np.cos(ang), np.sin(ang)
    alt = np.where(k % 2 == 0, 1.0, -1.0)
    a_im = np.where(k[:, None] == 0, alt[None, :], -sin)
    a = np.concatenate([cos, a_im], axis=0)
    ck = np.where(k == 0, 1.0, 2.0) / n
    b_re = cos * ck[None, :]
    b_im = np.where(k[None, :] == 0, alt[:, None] / n, -sin * (2.0 / n))
    b = np.concatenate([b_re, b_im], axis=1)
    return a.astype(np.float32), b.astype(np.float32)


@functools.lru_cache(maxsize=None)
def _hy_feats(l):
    pos = np.arange(l, dtype=np.float32)
    t = pos / np.float32(l - 1)
    bands = np.linspace(1e-4, HY_BANDS - 1, HY_BANDS, dtype=np.float32)
    ang = np.float32(2.0 * math.pi / l) * pos[:, None] * bands
    feats = np.concatenate([t[:, None], np.cos(ang), -np.sin(ang)], -1).astype(np.float32)
    feats = np.pad(feats, ((0, 0), (0, LANES - HY_EMB)))
    return feats, t[:, None]


@functools.lru_cache(maxsize=None)
def _hy_deltas(c):
    lo, hi = math.log(HY_DECAY_TARGET) / HY_SLOW_PCT, math.log(HY_DECAY_TARGET) / HY_FAST_PCT
    return np.abs(np.linspace(lo, hi, c, dtype=np.float32))[None, :]


def _hy_filter_kernel(
    feats_ref, t_ref, dl_ref, w1_ref, b1_ref, w2_ref, b2_ref, w3_ref, fq_ref, are_ref, aim_ref,
    p_ref, r_ref, s_ref, sum_scr, dif_scr, nyq_scr,
):
    f = pl.program_id(0)
    l = feats_ref.shape[0]
    c = dl_ref.shape[1]
    kb = are_ref.shape[0]

    @pl.when(f == 0)
    def _():
        fq = fq_ref[...]
        h = jnp.sin(fq[0:1] * (_dot_hi(feats_ref[...], w1_ref[...]) + b1_ref[...]))
        h = jnp.sin(fq[1:2] * (_dot_hi(h, w2_ref[...]) + b2_ref[...]))
        decay = jnp.exp(-t_ref[...] * dl_ref[...])
        row = lax.broadcasted_iota(jnp.int32, (l, 1), 0)
        sign = jnp.where((row & 1) == 0, 1.0, -1.0)
        for o in range(HY_ORDER):
            fwd = _dot_hi(h, w3_ref[:, (2 * o) * c:(2 * o + 1) * c]) * decay
            bwd = _dot_hi(h, w3_ref[:, (2 * o + 1) * c:(2 * o + 2) * c]) * decay
            bwd = jnp.where(row == 0, 0.0, bwd)
            norm = jnp.sum(jnp.abs(fwd), axis=0, keepdims=True) + jnp.sum(jnp.abs(bwd), axis=0, keepdims=True)
            fwd = fwd / norm
            bwd = bwd / norm
            ssum = fwd + bwd
            sum_scr[o] = ssum.astype(BF16)
            dif_scr[o] = (fwd - bwd).astype(BF16)
            nyq_scr[o] = jnp.broadcast_to(jnp.sum(ssum * sign, axis=0, keepdims=True), (SUBLANES, c))

    grow = f * kb + lax.broadcasted_iota(jnp.int32, (kb, 1), 0)
    for o in range(HY_ORDER):
        kre = jnp.dot(are_ref[...], sum_scr[o], preferred_element_type=F32)
        kim = jnp.dot(aim_ref[...], dif_scr[o], preferred_element_type=F32)
        p_ref[o] = kre
        r_ref[o] = jnp.where(grow == 0, 0.0, kim)
        s_ref[o] = jnp.where(grow == 0, nyq_scr[o][0:1], kre)


def _hy_filter_call(l, a_tab, w1, b1, w2, b2, w3, freq):
    c = GROUP_WIDTH
    kb = min(l, 256)
    nf = l // kb
    feats, t = _hy_feats(l)
    w1p = jnp.pad(w1, ((0, LANES - HY_EMB), (0, 0)))
    full = lambda shape: pl.BlockSpec(shape, lambda f: tuple(0 for _ in shape))
    out = jax.ShapeDtypeStruct((HY_ORDER, l, c), F32)
    out_spec = pl.BlockSpec((HY_ORDER, kb, c), lambda f: (0, f, 0))
    return pl.pallas_call(
        _hy_filter_kernel,
        out_shape=[out, out, out],
        grid=(nf,),
        in_specs=[
            full((l, LANES)), full((l, 1)), full((1, c)), full((LANES, HY_FFN)), full((1, HY_FFN)),
            full((HY_FFN, HY_FFN)), full((1, HY_FFN)), full((HY_FFN, 2 * HY_ORDER * c)), full((2, HY_FFN)),
            pl.BlockSpec((kb, l), lambda f: (f, 0)),
            pl.BlockSpec((kb, l), lambda f: (f + nf, 0)),
        ],
        out_specs=[out_spec, out_spec, out_spec],
        scratch_shapes=[
            pltpu.VMEM((HY_ORDER, l, c), BF16), pltpu.VMEM((HY_ORDER, l, c), BF16),
            pltpu.VMEM((HY_ORDER, SUBLANES, c), F32),
        ],
        compiler_params=_params("arbitrary"),
        name="hyena_filter",
    )(feats, t, _hy_deltas(c), w1p, b1[None, :], w2, b2[None, :], w3, freq, a_tab, a_tab)


def _hy_conv_kernel(z_ref, g_ref, bias_ref, are_ref, aim_ref, bre_ref, bim_ref, p_ref, r_ref, s_ref, o_ref, zb_scr):
    f = pl.program_id(2)

    @pl.when(f == 0)
    def _():
        zb_scr[...] = z_ref[...].astype(BF16)
        o_ref[...] = jnp.zeros_like(o_ref)

    zb = zb_scr[...]
    re = jnp.dot(are_ref[...], zb, preferred_element_type=F32)
    im = jnp.dot(aim_ref[...], zb, preferred_element_type=F32)
    p, r, s = p_ref[...], r_ref[...], s_ref[...]
    re2 = re * p - im * r
    im2 = re * r + im * s
    o_ref[...] += _dot(bre_ref[...], re2) + _dot(bim_ref[...], im2)

    @pl.when(f == pl.num_programs(2) - 1)
    def _():
        o_ref[...] = g_ref[...] * (o_ref[...] + z_ref[...] * bias_ref[...])


def _hy_conv_call(z, z_col, z_row0, g, g_col, g_row0, bias, a_tab, b_tab, prs, order, l, n_seq):
    c = GROUP_WIDTH
    ct = c
    ncb = c // ct
    kb = min(l, 512)
    nf = l // kb
    p, r, s = prs
    zspec = pl.BlockSpec((l, ct), lambda b, cb, f: (z_row0 // l + b, z_col * ncb + cb))
    gspec = pl.BlockSpec((l, ct), lambda b, cb, f: (g_row0 // l + b, g_col * ncb + cb))
    kspec = pl.BlockSpec((None, kb, ct), lambda b, cb, f: (order, f, cb))
    return pl.pallas_call(
        _hy_conv_kernel,
        out_shape=jax.ShapeDtypeStruct((n_seq * l, c), F32),
        grid=(n_seq, ncb, nf),
        in_specs=[
            zspec, gspec,
            pl.BlockSpec((None, 1, ct), lambda b, cb, f: (order, 0, cb)),
            pl.BlockSpec((kb, l), lambda b, cb, f: (f, 0)),
            pl.BlockSpec((kb, l), lambda b, cb, f: (f + nf, 0)),
            pl.BlockSpec((l, kb), lambda b, cb, f: (0, f)),
            pl.BlockSpec((l, kb), lambda b, cb, f: (0, f + nf)),
            kspec, kspec, kspec,
        ],
        out_specs=pl.BlockSpec((l, ct), lambda b, cb, f: (b, cb)),
        scratch_shapes=[pltpu.VMEM((l, ct), BF16)],
        compiler_params=_params("arbitrary", "arbitrary", "arbitrary"),
        name="hyena_conv",
    )(z, g, bias.reshape(HY_ORDER, 1, c), a_tab, a_tab, b_tab, b_tab, p, r, s)


CHUNKS_PER_STEP = 2
GDN_CHUNKS_PER_STEP = 4


def _scan_order(d, n=CHUNKS_PER_STEP):
    return list(range(n)) if d == 0 else list(range(n - 1, -1, -1))


def _chunk_tables(cfg, q):
    fwd, bwd, first, last, seq = [], [], [], [], []
    sid = 0
    for n_seq, l, row0 in ((cfg["n_dec"], cfg["ls"], 0), (cfg["n_ctx"], cfg["lp"], cfg["ts"])):
        nc = l // q
        for b in range(n_seq):
            base = (row0 + b * l) // q
            for c in range(nc):
                fwd.append(base + c)
                bwd.append(base + nc - 1 - c)
                first.append(int(c == 0))
                last.append(int(c == nc - 1))
                seq.append(sid)
            sid += 1
    return tuple(jnp.asarray(np.asarray(a, np.int32)) for a in (fwd, bwd, first, last, seq))


def _scan_masks(q, d):
    r = lax.broadcasted_iota(jnp.int32, (q, q), 0)
    c = lax.broadcasted_iota(jnp.int32, (q, q), 1)
    return ((r >= c), (r > c)) if d == 0 else ((r <= c), (r < c))


def _cumsum_pair(a_col, a_row, q, d):
    lo, up = _tri(q, True), _tri(q, False)
    if d == 0:
        return _dot_hi(lo, a_col), _dot_hi(a_row, up)
    return _dot_hi(up, a_col), _dot_hi(a_row, lo)


def _ssd_kernel(
    fwd_t, bwd_t, first_t, last_t, seq_t,
    xf_ref, bcf_ref, smf_ref, xb_ref, bcb_ref, smb_ref, pr_ref, pc_ref, dsk_ref, h0_ref,
    yf_ref, yb_ref, hout_ref, h_scr,
):
    s = pl.program_id(0)
    q, p, n = SSD_CHUNK, SSD_HEAD_DIM, SSD_STATE

    @pl.when(first_t[s] == 1)
    def _():
        h_scr[...] = h0_ref[...]

    bias_r, alog_r = pr_ref[0:1], pr_ref[1:2]
    bias_c, alog_c = pc_ref[:, 0:1], pc_ref[:, 1:2]
    assert q == LANES == n and 2 * p == LANES
    even_half = lax.broadcasted_iota(jnp.int32, (q, LANES), 1) < p
    probs = []
    for d in range(N_DIR):
        x_ref, bc_ref, sm_ref = (xf_ref, bcf_ref, smf_ref) if d == 0 else (xb_ref, bcb_ref, smb_ref)
        incl, _ = _scan_masks(q, d)
        end = q - 1 if d == 0 else 0
        for k in _scan_order(d):
            rows = slice(k * q, (k + 1) * q)
            x = _silu(x_ref[rows, :])
            x_t = x.T
            bc = _silu(bc_ref[rows, :])
            sm = sm_ref[rows, :]
            a_c = _softplus(sm + bias_r) * (-jnp.exp(alog_r))
            dt_r = _softplus(sm.T + bias_c)
            acs_c, acs_r = _cumsum_pair(a_c, (dt_r * (-jnp.exp(alog_c)))[SM_SDT:SM_SDT + N_DIR * SSD_HEADS], q, d)
            for g in range(SSD_GROUPS):
                bm = bc[:, g * n:(g + 1) * n]
                cm = bc[:, (SSD_GROUPS + g) * n:(SSD_GROUPS + g + 1) * n]
                cb = _dot_nt(cm, bm)
                for e in range(SSD_HPG):
                    hd = g * SSD_HPG + e
                    idx = d * SSD_HEADS + hd
                    pair = hd // 2
                    row = acs_r[idx:idx + 1, :]
                    dt_row = dt_r[SM_SDT + idx:SM_SDT + idx + 1, :]
                    tot = row[:, end:end + 1]
                    probs.append(dict(
                        d=d, k=k, hd=hd, bm=bm, cm=cm, cb=cb, incl=incl, row=row, dt_row=dt_row, tot=tot,
                        colb=jnp.broadcast_to(acs_c[:, idx:idx + 1], (q, q)),
                        x_pair=x[:, pair * LANES:(pair + 1) * LANES], x_t=x_t[hd * p:(hd + 1) * p, :],
                        skip=dsk_ref[d:d + 1, pair * LANES:(pair + 1) * LANES],
                    ))
    lmat = [jnp.exp(jnp.where(pb["incl"], pb["colb"] - pb["row"], -1e30)) * pb["dt_row"] for pb in probs]
    y_diag = [_dot(pb["cb"] * lm, pb["x_pair"]) for pb, lm in zip(probs, lmat)]
    states = [_dot(pb["x_t"] * (pb["dt_row"] * jnp.exp(pb["tot"] - pb["row"])), pb["bm"]) for pb in probs]
    c_dec = [pb["cm"] * jnp.exp(pb["colb"]) for pb in probs]
    state = {(d, hd): h_scr[d, hd] for d in range(N_DIR) for hd in range(SSD_HEADS)}
    ys = {}
    for step in range(CHUNKS_PER_STEP):
        group = [(i, pb) for i, pb in enumerate(probs) if _scan_order(pb["d"])[step] == pb["k"]]
        pair_state = {
            (d, pr_): jnp.concatenate([state[d, 2 * pr_], state[d, 2 * pr_ + 1]], axis=0)
            for d in range(N_DIR) for pr_ in range(SSD_HEADS // 2)
        }
        y_off = [_dot_nt(c_dec[i], pair_state[pb["d"], pb["hd"] // 2]) for i, pb in group]
        for (i, pb), yo in zip(group, y_off):
            ys[pb["d"], pb["k"], pb["hd"]] = y_diag[i] + yo
            state[pb["d"], pb["hd"]] = state[pb["d"], pb["hd"]] * jnp.exp(pb["tot"]) + states[i]
    for (d, hd), h_new in state.items():
        h_scr[d, hd] = h_new
    by_key = {(pb["d"], pb["k"], pb["hd"]): pb for pb in probs}
    for d, y_ref in ((0, yf_ref), (1, yb_ref)):
        for k in range(CHUNKS_PER_STEP):
            for pr_ in range(SSD_HEADS // 2):
                pb = by_key[d, k, 2 * pr_]
                y_pair = jnp.where(even_half, ys[d, k, 2 * pr_], ys[d, k, 2 * pr_ + 1]) + pb["skip"] * pb["x_pair"]
                y_ref[k * q:(k + 1) * q, pr_ * LANES:(pr_ + 1) * LANES] = y_pair.astype(BF16)

    @pl.when(last_t[s] == 1)
    def _():
        hout_ref[...] = h_scr[...]


def _ssd_call(proj, small, tables, h0, dt_bias, a_log, d_skip, cfg):
    q, w = CHUNKS_PER_STEP * SSD_CHUNK, GROUP_WIDTH
    t = proj.shape[0]
    n_steps = tables[0].shape[0]
    nlane = N_DIR * SSD_HEADS
    pr = jnp.zeros((SUBLANES, LANES), F32)
    pr = pr.at[0, SM_SDT:SM_SDT + nlane].set(dt_bias.reshape(-1)).at[1, SM_SDT:SM_SDT + nlane].set(a_log.reshape(-1))
    pc = pr.T
    dsk = jnp.repeat(d_skip, SSD_HEAD_DIM, axis=-1)
    st_shape = (N_DIR, SSD_HEADS, SSD_HEAD_DIM, SSD_STATE)
    fmap = lambda col: (lambda s, fw, bw, fi, la, sq: (fw[s], col))
    bmap = lambda col: (lambda s, fw, bw, fi, la, sq: (bw[s], col))
    const = lambda shape: pl.BlockSpec(shape, lambda s, *_: tuple(0 for _ in shape))
    st_spec = pl.BlockSpec((None,) + st_shape, lambda s, fw, bw, fi, la, sq: (sq[s], 0, 0, 0, 0))
    grid_spec = pltpu.PrefetchScalarGridSpec(
        num_scalar_prefetch=5,
        grid=(n_steps,),
        in_specs=[
            pl.BlockSpec((q, w), fmap(COL_S_X)), pl.BlockSpec((q, w), fmap(COL_S_BC)), pl.BlockSpec((q, LANES), fmap(0)),
            pl.BlockSpec((q, w), bmap(COL_S_X)), pl.BlockSpec((q, w), bmap(COL_S_BC)), pl.BlockSpec((q, LANES), bmap(0)),
            const((SUBLANES, LANES)), const((LANES, SUBLANES)), const((N_DIR, w)), st_spec,
        ],
        out_specs=[pl.BlockSpec((q, w), fmap(0)), pl.BlockSpec((q, w), bmap(0)), st_spec],
        scratch_shapes=[pltpu.VMEM(st_shape, F32)],
    )
    n_seq = cfg["n_dec"] + cfg["n_ctx"]
    return pl.pallas_call(
        _ssd_kernel,
        out_shape=[
            jax.ShapeDtypeStruct((t, w), BF16), jax.ShapeDtypeStruct((t, w), BF16),
            jax.ShapeDtypeStruct((n_seq,) + st_shape, F32),
        ],
        grid_spec=grid_spec,
        compiler_params=_params("arbitrary"),
        name="ssd_scan",
    )(*tables, proj, proj, small, proj, proj, small, pr, pc, dsk, h0)


TRI_BLOCK = 2 * SUBLANES


def _unit_tri_inverses(mats, lowers):
    n = mats[0].shape[0]
    blk = TRI_BLOCK
    sub = blk // SUBLANES
    lane = lax.broadcasted_iota(jnp.int32, (blk, n), 1)
    row = lax.broadcasted_iota(jnp.int32, (blk, n), 0)
    group = (lax.broadcasted_iota(jnp.int32, (SUBLANES, LANES), 1) // blk) * blk
    tiles = lambda v: [v[t * SUBLANES:(t + 1) * SUBLANES, :] for t in range(sub)]
    eye = jnp.where(lane % blk == row, 1.0, 0.0)
    d_t, inv_t = [], []
    for a in mats:
        packed = jnp.zeros((blk, n), F32)
        for bi in range(n // blk):
            packed = jnp.where(lane // blk == bi, a[bi * blk:(bi + 1) * blk, :], packed)
        if n < LANES:
            packed = jnp.concatenate([packed, jnp.zeros((blk, LANES - n), F32)], axis=1)
        d_t.append(tiles(packed))
        inv_t.append(tiles(eye))
    for step in range(blk):
        for i, lower in enumerate(lowers):
            jj = step if lower else blk - 1 - step
            st, jr = divmod(jj, SUBLANES)
            done_row = inv_t[i][st][jr:jr + 1, :]
            for s in (range(st, sub) if lower else range(st + 1)):
                col = jnp.take_along_axis(d_t[i][s], group + jj, axis=1)[:, :n]
                inv_t[i][s] = inv_t[i][s] - col * done_row
    ts = []
    for i in range(len(mats)):
        inv = jnp.concatenate(inv_t[i], axis=0)
        ts.append(jnp.concatenate([jnp.where(lane // blk == bi, inv, 0.0) for bi in range(n // blk)], axis=0))
    r = lax.broadcasted_iota(jnp.int32, (n, n), 0)
    c = lax.broadcasted_iota(jnp.int32, (n, n), 1)
    size = blk
    while size < n:
        level = (r // (2 * size) == c // (2 * size)) & (r // size != c // size)
        half = [_dot(t, jnp.where(level, a, 0.0)) for t, a in zip(ts, mats)]
        ts = [t - _dot(h, t) for t, h in zip(ts, half)]
        size *= 2
    return ts


def _l2norm_heads(x, heads, dim, scale):
    outs = []
    for h in range(heads):
        xh = x[:, h * dim:(h + 1) * dim]
        outs.append(xh * (lax.rsqrt(jnp.sum(xh * xh, axis=-1, keepdims=True) + 1e-6) * scale))
    return outs


def _gdn_kernel(
    fwd_t, bwd_t, first_t, last_t, seq_t,
    qf_ref, kf_ref, vf_ref, smf_ref, qb_ref, kb_ref, vb_ref, smb_ref, pr_ref, pc_ref, s0_ref,
    of_ref, ob_ref, sout_ref, s_scr,
):
    s = pl.program_id(0)
    q, dim = GDN_CHUNK, GDN_HEAD_DIM

    @pl.when(first_t[s] == 1)
    def _():
        s_scr[...] = s0_ref[...]

    bias_r, alog_r = pr_ref[0:1], pr_ref[1:2]
    bias_c, alog_c = pc_ref[:, 0:1], pc_ref[:, 1:2]
    n_sub = GDN_CHUNKS_PER_STEP
    probs = []
    for d in range(N_DIR):
        q_ref, k_ref, v_ref, sm_ref = (qf_ref, kf_ref, vf_ref, smf_ref) if d == 0 else (qb_ref, kb_ref, vb_ref, smb_ref)
        incl, strict = _scan_masks(q, d)
        end = q - 1 if d == 0 else 0
        for c in _scan_order(d, n_sub):
            rows = slice(c * q, (c + 1) * q)
            qs = _l2norm_heads(_silu(q_ref[rows, :]), GDN_HEADS, dim, dim ** -0.5)
            ks = _l2norm_heads(_silu(k_ref[rows, :]), GDN_HEADS, dim, 1.0)
            v = _silu(v_ref[rows, :])
            sm = sm_ref[rows, :]
            beta_c = jax.nn.sigmoid(sm)
            g_c = -jnp.exp(alog_r) * _softplus(sm + bias_r)
            g_r = -jnp.exp(alog_c) * _softplus(sm.T + bias_c)
            gc_c, gc_r = _cumsum_pair(g_c, g_r[SM_GA:SM_GA + N_DIR * GDN_HEADS], q, d)
            for h in range(GDN_HEADS):
                idx = d * GDN_HEADS + h
                row = gc_r[idx:idx + 1, :]
                probs.append(dict(
                    d=d, c=c, h=h, q=qs[h], k=ks[h], v=v[:, h * dim:(h + 1) * dim], row=row, tot=row[:, end:end + 1],
                    colb=jnp.broadcast_to(gc_c[:, SM_GA + idx:SM_GA + idx + 1], (q, dim)),
                    betab=jnp.broadcast_to(beta_c[:, SM_GBETA + idx:SM_GBETA + idx + 1], (q, dim)),
                    incl=incl, strict=strict,
                ))
    k16 = [p["k"].astype(BF16) for p in probs]
    kk = [_dot_nt(k, k) for k in k16]
    qk = [_dot_nt(p["q"], k) for p, k in zip(probs, k16)]
    dmask = [jnp.exp(jnp.where(p["incl"], p["colb"][:, :q] - p["row"], -1e30)) for p in probs]
    a = [jnp.where(p["strict"], p["betab"][:, :q] * kk_i * dm, 0.0) for p, kk_i, dm in zip(probs, kk, dmask)]
    ecol = [jnp.exp(p["colb"]) for p in probs]
    rhs = [
        jnp.concatenate([p["v"] * p["betab"], p["k"] * (p["betab"] * e)], axis=-1).astype(BF16)
        for p, e in zip(probs, ecol)
    ]
    inv = _unit_tri_inverses(a, [p["d"] == 0 for p in probs])
    sol = [_dot(t, r) for t, r in zip(inv, rhs)]
    u_sol = [x[:, :dim] for x in sol]
    w_sol = [x[:, dim:].astype(BF16) for x in sol]
    attn = [(qk_i * dm).astype(BF16) for qk_i, dm in zip(qk, dmask)]
    q_dec = [(p["q"] * e).astype(BF16) for p, e in zip(probs, ecol)]
    k_dec = [(p["k"] * jnp.exp(p["tot"] - p["colb"])).astype(BF16) for p in probs]
    state = {(d, h): s_scr[d, h] for d in range(N_DIR) for h in range(GDN_HEADS)}
    outs = {}
    for step in range(n_sub):
        group = [(i, p) for i, p in enumerate(probs) if _scan_order(p["d"], n_sub)[step] == p["c"]]
        w_s = [_dot(w_sol[i], state[p["d"], p["h"]]) for i, p in group]
        q_s = [_dot(q_dec[i], state[p["d"], p["h"]]) for i, p in group]
        vv = [u_sol[i] - ws for (i, p), ws in zip(group, w_s)]
        o_at = [_dot(attn[i], vv_i) for (i, p), vv_i in zip(group, vv)]
        upd = [_dot_tn(k_dec[i], vv_i) for (i, p), vv_i in zip(group, vv)]
        for (i, p), qs_i, oa, u_i in zip(group, q_s, o_at, upd):
            outs[p["d"], p["c"], p["h"]] = qs_i + oa
            state[p["d"], p["h"]] = state[p["d"], p["h"]] * jnp.exp(p["tot"]) + u_i
    for (d, h), s_new in state.items():
        s_scr[d, h] = s_new
    for d, o_ref in ((0, of_ref), (1, ob_ref)):
        for c in range(n_sub):
            o_ref[c * q:(c + 1) * q, :] = jnp.concatenate([outs[d, c, h] for h in range(GDN_HEADS)], axis=-1).astype(BF16)

    @pl.when(last_t[s] == 1)
    def _():
        sout_ref[...] = s_scr[...]


def _gdn_call(proj, small, tables, s0, a_log, dt_bias, cfg):
    q, w = GDN_CHUNKS_PER_STEP * GDN_CHUNK, GROUP_WIDTH
    t = proj.shape[0]
    n_steps = tables[0].shape[0]
    nlane = N_DIR * GDN_HEADS
    pr = jnp.zeros((SUBLANES, LANES), F32)
    pr = pr.at[0, SM_GA:SM_GA + nlane].set(dt_bias.reshape(-1)).at[1, SM_GA:SM_GA + nlane].set(a_log.reshape(-1))
    pc = pr.T
    st_shape = (N_DIR, GDN_HEADS, GDN_HEAD_DIM, GDN_HEAD_DIM)
    fmap = lambda col: (lambda s, fw, bw, fi, la, sq: (fw[s], col))
    bmap = lambda col: (lambda s, fw, bw, fi, la, sq: (bw[s], col))
    const = lambda shape: pl.BlockSpec(shape, lambda s, *_: tuple(0 for _ in shape))
    st_spec = pl.BlockSpec((None,) + st_shape, lambda s, fw, bw, fi, la, sq: (sq[s], 0, 0, 0, 0))
    blk = lambda m, col: pl.BlockSpec((q, w), m(col))
    grid_spec = pltpu.PrefetchScalarGridSpec(
        num_scalar_prefetch=5,
        grid=(n_steps,),
        in_specs=[
            blk(fmap, COL_G_Q), blk(fmap, COL_G_K), blk(fmap, COL_G_V), pl.BlockSpec((q, LANES), fmap(0)),
            blk(bmap, COL_G_Q), blk(bmap, COL_G_K), blk(bmap, COL_G_V), pl.BlockSpec((q, LANES), bmap(0)),
            const((SUBLANES, LANES)), const((LANES, SUBLANES)), st_spec,
        ],
        out_specs=[pl.BlockSpec((q, w), fmap(0)), pl.BlockSpec((q, w), bmap(0)), st_spec],
        scratch_shapes=[pltpu.VMEM(st_shape, F32)],
    )
    n_seq = cfg["n_dec"] + cfg["n_ctx"]
    return pl.pallas_call(
        _gdn_kernel,
        out_shape=[
            jax.ShapeDtypeStruct((t, w), BF16), jax.ShapeDtypeStruct((t, w), BF16),
            jax.ShapeDtypeStruct((n_seq,) + st_shape, F32),
        ],
        grid_spec=grid_spec,
        compiler_params=_params("arbitrary"),
        name="gdn_scan",
    )(*tables, proj, proj, proj, small, proj, proj, proj, small, pr, pc, s0)


def _rope_tables(cfg):
    l = cfg["ls"]
    pos = np.arange(l)
    r = (pos // GRID_W).astype(np.float32)
    col = (pos % GRID_W).astype(np.float32)
    nf = RET_HEAD_DIM // 4
    inv = np.power(np.float32(ROPE_BASE), -np.arange(nf, dtype=np.float32) / np.float32(nf)).astype(np.float32)
    ang = np.concatenate([r[:, None] * inv, col[:, None] * inv], -1)
    cos, sin = np.cos(ang), np.sin(ang)
    cos2 = np.concatenate([cos, cos], -1)
    sin2 = np.concatenate([-sin, sin], -1)
    ident = np.ones((CHUNKS_PER_STEP * RET_CHUNK, RET_HEAD_DIM), np.float32)
    return np.concatenate([cos2, ident], 0).astype(np.float32), np.concatenate([sin2, 0.0 * ident], 0).astype(np.float32)


def _rope_block_table(cfg):
    q = CHUNKS_PER_STEP * RET_CHUNK
    nc_s, nc_p = cfg["ls"] // q, cfg["lp"] // q
    fwd = [c for _ in range(cfg["n_dec"]) for c in range(nc_s)] + [nc_s] * (cfg["n_ctx"] * nc_p)
    bwd = [nc_s - 1 - c for _ in range(cfg["n_dec"]) for c in range(nc_s)] + [nc_s] * (cfg["n_ctx"] * nc_p)
    return jnp.asarray(np.asarray(fwd, np.int32)), jnp.asarray(np.asarray(bwd, np.int32))


def _ret_kernel(
    fwd_t, bwd_t, first_t, last_t, seq_t, rf_t, rb_t,
    qf_ref, kf_ref, vf_ref, cf_ref, sf_ref, qb_ref, kb_ref, vb_ref, cb_ref, sb_ref, dec_ref, s0_ref,
    of_ref, ob_ref, sout_ref, s_scr,
):
    s = pl.program_id(0)
    q, dim = RET_CHUNK, RET_HEAD_DIM

    @pl.when(first_t[s] == 1)
    def _():
        s_scr[...] = s0_ref[...]

    lg_all = -jnp.exp(dec_ref[...])
    ri = lax.broadcasted_iota(jnp.int32, (q, q), 0)
    ci = lax.broadcasted_iota(jnp.int32, (q, q), 1)
    rpos = lax.broadcasted_iota(jnp.int32, (q, dim), 0)
    n_sub = CHUNKS_PER_STEP
    probs = []
    for d in range(N_DIR):
        q_ref, k_ref, v_ref, c_ref, sn_ref = (
            (qf_ref, kf_ref, vf_ref, cf_ref, sf_ref) if d == 0 else (qb_ref, kb_ref, vb_ref, cb_ref, sb_ref)
        )
        rel = ((ri - ci) if d == 0 else (ci - ri)).astype(F32)
        pos = (rpos if d == 0 else (q - 1 - rpos)).astype(F32)
        for c in _scan_order(d, n_sub):
            rows = slice(c * q, (c + 1) * q)
            qa, ka, va = q_ref[rows, :], k_ref[rows, :], v_ref[rows, :]
            cos, sin = c_ref[rows, :], sn_ref[rows, :]
            for h in range(RET_HEADS):
                idx = d * RET_HEADS + h
                qh = qa[:, h * dim:(h + 1) * dim]
                kh = ka[:, h * dim:(h + 1) * dim] * dim ** -0.5
                probs.append(dict(
                    d=d, c=c, h=h, lg=lg_all[idx:idx + 1, :], rel=rel, pos=pos, v=va[:, h * dim:(h + 1) * dim],
                    q=qh * cos + pltpu.roll(qh, dim // 2, axis=1) * sin,
                    k=kh * cos + pltpu.roll(kh, dim // 2, axis=1) * sin,
                ))
    qk = [_dot_nt(pb["q"], pb["k"]) for pb in probs]
    dmask = [jnp.exp(jnp.where(pb["rel"] >= 0, pb["rel"] * pb["lg"], -1e30)) for pb in probs]
    o_attn = [_dot(qk_i * dm, pb["v"]) for pb, qk_i, dm in zip(probs, qk, dmask)]
    q_dec = [pb["q"] * jnp.exp((pb["pos"] + 1.0) * pb["lg"]) for pb in probs]
    upd = [_dot_tn(pb["k"] * jnp.exp((q - 1.0 - pb["pos"]) * pb["lg"]), pb["v"]) for pb in probs]
    state = {(d, h): s_scr[d, h] for d in range(N_DIR) for h in range(RET_HEADS)}
    outs = {}
    for step in range(n_sub):
        group = [(i, pb) for i, pb in enumerate(probs) if _scan_order(pb["d"], n_sub)[step] == pb["c"]]
        o_state = [_dot(q_dec[i], state[pb["d"], pb["h"]]) for i, pb in group]
        for (i, pb), os_i in zip(group, o_state):
            outs[pb["d"], pb["c"], pb["h"]] = os_i + o_attn[i]
            state[pb["d"], pb["h"]] = state[pb["d"], pb["h"]] * jnp.exp(q * pb["lg"]) + upd[i]
    for (d, h), s_new in state.items():
        s_scr[d, h] = s_new
    for d, o_ref in ((0, of_ref), (1, ob_ref)):
        for c in range(n_sub):
            o_ref[c * q:(c + 1) * q, :] = jnp.concatenate([outs[d, c, h] for h in range(RET_HEADS)], axis=-1).astype(BF16)

    @pl.when(last_t[s] == 1)
    def _():
        sout_ref[...] = s_scr[...]


def _ret_call(proj, tables, rope_tabs, rope_blocks, s0, ret_decay, cfg):
    q, w, dim = CHUNKS_PER_STEP * RET_CHUNK, GROUP_WIDTH, RET_HEAD_DIM
    t = proj.shape[0]
    n_steps = tables[0].shape[0]
    dec = jnp.broadcast_to(ret_decay.reshape(-1, 1), (N_DIR * RET_HEADS, LANES))
    st_shape = (N_DIR, RET_HEADS, dim, dim)
    fmap = lambda col: (lambda s, fw, bw, fi, la, sq, rf, rb: (fw[s], col))
    bmap = lambda col: (lambda s, fw, bw, fi, la, sq, rf, rb: (bw[s], col))
    rfmap = lambda s, fw, bw, fi, la, sq, rf, rb: (rf[s], 0)
    rbmap = lambda s, fw, bw, fi, la, sq, rf, rb: (rb[s], 0)
    st_spec = pl.BlockSpec((None,) + st_shape, lambda s, fw, bw, fi, la, sq, rf, rb: (sq[s], 0, 0, 0, 0))
    blk = lambda m, col: pl.BlockSpec((q, w), m(col))
    rope = lambda m: pl.BlockSpec((q, dim), m)
    grid_spec = pltpu.PrefetchScalarGridSpec(
        num_scalar_prefetch=7,
        grid=(n_steps,),
        in_specs=[
            blk(fmap, COL_R_Q), blk(fmap, COL_R_K), blk(fmap, COL_R_V), rope(rfmap), rope(rfmap),
            blk(bmap, COL_R_Q), blk(bmap, COL_R_K), blk(bmap, COL_R_V), rope(rbmap), rope(rbmap),
            pl.BlockSpec((N_DIR * RET_HEADS, LANES), lambda s, *_: (0, 0)), st_spec,
        ],
        out_specs=[pl.BlockSpec((q, w), fmap(0)), pl.BlockSpec((q, w), bmap(0)), st_spec],
        scratch_shapes=[pltpu.VMEM(st_shape, F32)],
    )
    n_seq = cfg["n_dec"] + cfg["n_ctx"]
    cos2, sin2 = rope_tabs
    return pl.pallas_call(
        _ret_kernel,
        out_shape=[
            jax.ShapeDtypeStruct((t, w), BF16), jax.ShapeDtypeStruct((t, w), BF16),
            jax.ShapeDtypeStruct((n_seq,) + st_shape, F32),
        ],
        grid_spec=grid_spec,
        compiler_params=_params("arbitrary"),
        name="retention_scan",
    )(*tables, *rope_blocks, proj, proj, proj, cos2, sin2, proj, proj, proj, cos2, sin2, dec, s0)


def _layer_norm_rows(y, g, b):
    mu = jnp.mean(y, axis=-1, keepdims=True)
    yc = y - mu
    var = jnp.mean(yc * yc, axis=-1, keepdims=True)
    return yc * lax.rsqrt(var + LN_EPS) * g + b


OUTPROJ_ROW_CHUNK = 256


def _outproj_kernel(
    hys_ref, hyp_ref, sf_ref, sb_ref, sz_ref, gf_ref, gb_ref, gz_ref, rf_ref, rb_ref, rg_ref,
    snw_ref, gnw_ref, w_ref, xs_ref, xp_ref, g1_ref, sh2_ref, sc2_ref, lng_ref, lnb_ref,
    x1_ref, u2_ref, lhs_scr, *, n_sample_tiles, alpha, two_group_x,
):
    i = pl.program_id(0)
    is_sample = i < n_sample_tiles
    w = GROUP_WIDTH
    tm = x1_ref.shape[0]
    rc = min(OUTPROJ_ROW_CHUNK, tm)
    chunks = [slice(r, r + rc) for r in range(0, tm, rc)]
    gnw = gnw_ref[...]
    for rows in chunks:
        y_hy = jnp.where(is_sample, hys_ref[rows, :], hyp_ref[rows, :])
        y_ssd = (sf_ref[rows, :].astype(F32) + sb_ref[rows, :].astype(F32)) * _silu(sz_ref[rows, :])
        y_ssd = y_ssd * lax.rsqrt(jnp.mean(y_ssd * y_ssd, axis=-1, keepdims=True) + RMS_EPS) * snw_ref[...]
        lhs_scr[rows, 0:w] = y_hy.astype(BF16)
        lhs_scr[rows, w:2 * w] = y_ssd.astype(BF16)
        og = gf_ref[rows, :].astype(F32) + gb_ref[rows, :].astype(F32)
        orr = rf_ref[rows, :].astype(F32) + rb_ref[rows, :].astype(F32)
        gz = _silu(gz_ref[rows, :])
        rg = _silu(rg_ref[rows, :])
        for h in range(GDN_HEADS):
            sl = slice(h * GDN_HEAD_DIM, (h + 1) * GDN_HEAD_DIM)
            o = og[:, sl]
            o = o * lax.rsqrt(jnp.mean(o * o, axis=-1, keepdims=True) + RMS_EPS) * gnw * gz[:, sl]
            lhs_scr[rows, 2 * w + h * GDN_HEAD_DIM:2 * w + (h + 1) * GDN_HEAD_DIM] = o.astype(BF16)
        for h in range(RET_HEADS):
            sl = slice(h * RET_HEAD_DIM, (h + 1) * RET_HEAD_DIM)
            o = orr[:, sl]
            mu = jnp.mean(o, axis=-1, keepdims=True)
            oc = o - mu
            o = oc * lax.rsqrt(jnp.mean(oc * oc, axis=-1, keepdims=True) + LN_EPS) * rg[:, sl]
            lhs_scr[rows, 3 * w + h * RET_HEAD_DIM:3 * w + (h + 1) * RET_HEAD_DIM] = o.astype(BF16)
    accs = [jnp.dot(lhs_scr[rows, :], w_ref[...], preferred_element_type=F32) for rows in chunks]
    for rows, acc in zip(chunks, accs):
        x = jnp.where(is_sample, xs_ref[rows, :], xp_ref[rows, :]) if two_group_x else xs_ref[rows, :]
        x1 = _layer_norm_rows(alpha * x + g1_ref[...] * acc, lng_ref[...], lnb_ref[...])
        x1_ref[rows, :] = x1
        u2_ref[rows, :] = (x1 * (1.0 + sc2_ref[...]) + sh2_ref[...]).astype(BF16)


def _outproj_call(hy_s, hy_p, ssd, gdn, ret, proj, ssd_nw, gdn_nw, w_out, x_in, mod4, layer, ln_g, ln_b, cfg):
    t, d, w = cfg["t"], cfg["d"], GROUP_WIDTH
    two_group_x = isinstance(x_in, tuple)
    tm = OUTPROJ_ROW_CHUNK if two_group_x else 2 * OUTPROJ_ROW_CHUNK
    ns = cfg["ts"] // tm
    row = lambda col: pl.BlockSpec((tm, w), lambda i: (i, col))
    vec = lambda n: pl.BlockSpec((None, 1, n), lambda i: (layer, 0, 0))
    mspec = lambda k: _mod_spec(layer, k, d, tm, cfg["ls"], cfg["n_dec"])
    if two_group_x:
        x_specs = _two_group_specs(tm, d, ns)
        x_args = list(x_in)
    else:
        x_specs = [pl.BlockSpec((tm, d), lambda i: (i, 0)), pl.BlockSpec((SUBLANES, d), lambda i: (0, 0))]
        x_args = [x_in, x_in]
    kern = functools.partial(_outproj_kernel, n_sample_tiles=ns, alpha=cfg["alpha"], two_group_x=two_group_x)
    return pl.pallas_call(
        kern,
        out_shape=[jax.ShapeDtypeStruct((t, d), F32), jax.ShapeDtypeStruct((t, d), BF16)],
        grid=(t // tm,),
        in_specs=_two_group_specs(tm, w, ns)
        + [row(0), row(0), row(COL_S_Z), row(0), row(0), row(COL_G_Z), row(0), row(0), row(COL_R_G)]
        + [vec(w), vec(GDN_HEAD_DIM),
           pl.BlockSpec((None, d, d), lambda i: (layer, 0, 0), pipeline_mode=pl.Buffered(1))]
        + x_specs
        + [mspec(2), mspec(3), mspec(4), vec(d), vec(d)],
        out_specs=[pl.BlockSpec((tm, d), lambda i: (i, 0)), pl.BlockSpec((tm, d), lambda i: (i, 0))],
        scratch_shapes=[pltpu.VMEM((tm, d), BF16)],
        compiler_params=_params("arbitrary"),
        name="out_proj_ln",
    )(
        hy_s, hy_p, ssd[0], ssd[1], proj, gdn[0], gdn[1], proj, ret[0], ret[1], proj,
        ssd_nw, gdn_nw, w_out, *x_args, mod4, mod4, mod4, ln_g, ln_b,
    )


def _ffn_up_kernel(u_ref, wg_ref, wv_ref, cwg_ref, cbg_ref, cwv_ref, cbv_ref, o_ref, w_scr, h_scr, *, n_sample_tiles, ls, lp):
    i = pl.program_id(1)
    tn = wg_ref.shape[1]
    seq_len = jnp.where(i < n_sample_tiles, ls, lp)

    @pl.when(i == 0)
    def _():
        w_scr[:, :tn] = wg_ref[...].astype(BF16)
        w_scr[:, tn:] = wv_ref[...].astype(BF16)

    cw = jnp.concatenate([cwg_ref[...], cwv_ref[...]], axis=-1)
    cb = jnp.concatenate([cbg_ref[...], cbv_ref[...]], axis=-1)

    def finish(rows, conv):
        o_ref[rows, :] = (_silu(conv[:, :tn]) * conv[:, tn:]).astype(BF16)

    _matmul_conv_chunks(u_ref, w_scr[...], cw, cb, h_scr, seq_len, min(ls, lp), finish)


def _ffn_up_call(u2, w_up, cw, cb, layer, cfg):
    t, d = u2.shape
    dff = w_up.shape[2] // 2
    tm, tn = cfg["ls"], 256
    nj = dff // tn
    kern = functools.partial(_ffn_up_kernel, n_sample_tiles=cfg["ts"] // tm, ls=cfg["ls"], lp=cfg["lp"])
    gcol = lambda j, i: (layer, 0, j)
    vcol = lambda j, i: (layer, 0, j + nj)
    return pl.pallas_call(
        kern,
        out_shape=jax.ShapeDtypeStruct((t, dff), BF16),
        grid=(nj, t // tm),
        in_specs=[
            pl.BlockSpec((tm, d), lambda j, i: (i, 0)),
            pl.BlockSpec((None, d, tn), gcol), pl.BlockSpec((None, d, tn), vcol),
            pl.BlockSpec((None, CONV_W, tn), gcol), pl.BlockSpec((None, 1, tn), gcol),
            pl.BlockSpec((None, CONV_W, tn), vcol), pl.BlockSpec((None, 1, tn), vcol),
        ],
        out_specs=pl.BlockSpec((tm, tn), lambda j, i: (i, j)),
        scratch_shapes=[pltpu.VMEM((d, 2 * tn), BF16), pltpu.VMEM((tm + 2 * SUBLANES, 2 * tn), F32)],
        compiler_params=_params("arbitrary", "arbitrary"),
        name="ffn_up_conv_glu",
    )(u2, w_up, w_up, cw, cb, cw, cb)


FFN_DOWN_ROW_CHUNK = 256


def _ffn_down_kernel(a_ref, w_ref, x1_ref, g2_ref, shn_ref, scn_ref, lng_ref, lnb_ref, oa_ref, ob_ref, acc_scr,
                     *, alpha, n_sample_tiles, last_layer, n_k):
    i = pl.program_id(0)
    k = pl.program_id(1)
    tm = acc_scr.shape[0]

    if n_k > 1:
        @pl.when(k == 0)
        def _():
            acc_scr[...] = jnp.dot(a_ref[...], w_ref[...], preferred_element_type=F32)

        @pl.when(jnp.logical_and(k > 0, k < n_k - 1))
        def _():
            acc_scr[...] += jnp.dot(a_ref[...], w_ref[...], preferred_element_type=F32)

    @pl.when(k == n_k - 1)
    def _():
        rc = min(FFN_DOWN_ROW_CHUNK, tm)
        chunks = [slice(r, r + rc) for r in range(0, tm, rc)]
        parts = [jnp.dot(a_ref[rows, :], w_ref[...], preferred_element_type=F32) for rows in chunks]
        for rows, part in zip(chunks, parts):
            f = part + acc_scr[rows, :] if n_k > 1 else part
            x2 = _layer_norm_rows(alpha * x1_ref[rows, :] + g2_ref[...] * f, lng_ref[...], lnb_ref[...])
            if last_layer:
                @pl.when(i < n_sample_tiles)
                def _():
                    oa_ref[rows, :] = x2

                @pl.when(i >= n_sample_tiles)
                def _():
                    ob_ref[rows, :] = x2
            else:
                oa_ref[rows, :] = x2
                ob_ref[rows, :] = (x2 * (1.0 + scn_ref[...]) + shn_ref[...]).astype(BF16)


def _ffn_down_call(act, w_down, x1, mod4, layer, ln_g, ln_b, cfg):
    t, dff = act.shape
    d = cfg["d"]
    depth = w_down.shape[0]
    last_layer = layer == depth - 1
    next_layer = min(layer + 1, depth - 1)
    tm = 512
    tk = dff // 4 if (dff // 4) % LANES == 0 else dff
    ns = cfg["ts"] // tm
    vec = lambda: pl.BlockSpec((None, 1, d), lambda i, k: (layer, 0, 0))
    mspec = lambda lay, kk: _mod_spec(lay, kk, d, tm, cfg["ls"], cfg["n_dec"])
    if last_layer:
        out_shape = [jax.ShapeDtypeStruct((cfg["ts"], d), F32), jax.ShapeDtypeStruct((cfg["tp"], d), F32)]
        out_specs = [
            pl.BlockSpec((tm, d), lambda i, k: (jnp.minimum(i, ns - 1), 0)),
            pl.BlockSpec((tm, d), lambda i, k: (jnp.maximum(i - ns, 0), 0)),
        ]
    else:
        out_shape = [jax.ShapeDtypeStruct((t, d), F32), jax.ShapeDtypeStruct((t, d), BF16)]
        out_specs = [pl.BlockSpec((tm, d), lambda i, k: (i, 0)), pl.BlockSpec((tm, d), lambda i, k: (i, 0))]
    return pl.pallas_call(
        functools.partial(_ffn_down_kernel, alpha=cfg["alpha"], n_sample_tiles=ns, last_layer=last_layer, n_k=dff // tk),
        out_shape=out_shape,
        grid=(t // tm, dff // tk),
        in_specs=[
            pl.BlockSpec((tm, tk), lambda i, k: (i, k)),
            pl.BlockSpec((None, tk, d), lambda i, k: (layer, k, 0)),
            pl.BlockSpec((tm, d), lambda i, k: (i, 0)),
            mspec(layer, 5), mspec(next_layer, 0), mspec(next_layer, 1), vec(), vec(),
        ],
        out_specs=out_specs,
        scratch_shapes=[pltpu.VMEM((tm, d), F32)],
        compiler_params=_params("arbitrary", "arbitrary"),
        name="ffn_down_ln",
    )(act, w_down, x1, mod4, mod4, mod4, ln_g, ln_b)


def _reorder_w_in(w_in):
    w = GROUP_WIDTH
    o = N_CONV
    s_z = w_in[..., o:o + w]
    o += w
    s_dt = w_in[..., o:o + N_DIR * SSD_HEADS]
    o += N_DIR * SSD_HEADS
    g_z = w_in[..., o:o + w]
    o += w
    g_beta = w_in[..., o:o + N_DIR * GDN_HEADS]
    o += N_DIR * GDN_HEADS
    g_a = w_in[..., o:o + N_DIR * GDN_HEADS]
    o += N_DIR * GDN_HEADS
    rest = w_in[..., o:]
    main = jnp.concatenate([w_in[..., :N_CONV], s_z, g_z, rest], axis=-1).astype(BF16)
    small = jnp.concatenate([s_dt, g_beta, g_a], axis=-1)
    small = jnp.pad(small, ((0, 0), (0, 0), (0, LANES - small.shape[-1]))).astype(BF16)
    return main, small


def kernel(x_prompt, x_sample, state_ssd, state_gdn, state_ret, c, c_ctx, w_mod, b_mod, w_in, conv_w, conv_b,
           hy_w1, hy_b1, hy_w2, hy_b2, hy_w3, hy_freq, hy_bias, ssd_A_log, ssd_dt_bias, ssd_D, ssd_norm_w,
           gdn_A_log, gdn_dt_bias, gdn_norm_w, ret_decay, w_out, ln1_g, ln1_b, w_up, ffn_conv_w, ffn_conv_b,
           w_down, ln2_g, ln2_b):
    n_ctx, lp, d = x_prompt.shape
    n_dec, ls, _ = x_sample.shape
    depth = w_in.shape[0]
    ts, tp = n_dec * ls, n_ctx * lp
    assert ls & (ls - 1) == 0 and lp & (lp - 1) == 0 and tp % ls == 0 and ls % lp == 0
    assert d == 4 * GROUP_WIDTH and w_in.shape[2] == N_MAIN + 2 * N_DIR * (SSD_HEADS // 2 + GDN_HEADS)
    cfg = dict(d=d, ls=ls, lp=lp, n_dec=n_dec, n_ctx=n_ctx, ts=ts, tp=tp, t=ts + tp, alpha=(2 * depth) ** 0.25)

    xs = x_sample.reshape(ts, d)
    xp = x_prompt.reshape(tp, d)

    mod_rows = -(-(n_dec + 1) // SUBLANES) * SUBLANES
    cond = jnp.concatenate([c, c_ctx[None, :], jnp.zeros((mod_rows - n_dec - 1, d), F32)], axis=0)
    mod = _mod_call(cond, w_mod, b_mod)
    mod4 = mod.reshape(depth, mod_rows, 1, 6 * d)

    w_main, w_small = _reorder_w_in(w_in)
    w_out_b, w_down_b = w_out.astype(BF16), w_down.astype(BF16)
    stack_rows = lambda v: v[:, None, :]

    dft = {l: tuple(jnp.asarray(tab).astype(BF16) for tab in _dft_tables(l)) for l in (ls, lp)}
    rope_tabs = _rope_tables(cfg)
    rope_blocks = _rope_block_table(cfg)
    ssd_rows, gdn_rows, ret_rows = CHUNKS_PER_STEP * SSD_CHUNK, GDN_CHUNKS_PER_STEP * GDN_CHUNK, CHUNKS_PER_STEP * RET_CHUNK
    tabs = {q: _chunk_tables(cfg, q) for q in sorted({ssd_rows, gdn_rows, ret_rows})}

    def init_state(st, l):
        zeros = jnp.zeros((n_ctx,) + st.shape[2:], F32)
        return jnp.concatenate([st[:, l], zeros], axis=0)

    u = _modulate_call(xs, xp, mod4, 0, cfg)
    x_res = (xs, xp)
    new_ssd, new_gdn, new_ret = [], [], []
    for l in range(depth):
        proj, small = _inproj_call(u, w_main, w_small, conv_w, stack_rows(conv_b), l, cfg)

        hy = {}
        for name, seq_len, n_seq, row0 in (("s", ls, n_dec, 0), ("p", lp, n_ctx, ts)):
            a_tab, b_tab = dft[seq_len]
            prs = _hy_filter_call(seq_len, a_tab, hy_w1[l], hy_b1[l], hy_w2[l], hy_b2[l], hy_w3[l], hy_freq[l])
            z1 = _hy_conv_call(proj, COL_HY_V, row0, proj, COL_HY_X1, row0, hy_bias[l], a_tab, b_tab, prs, 0, seq_len, n_seq)
            hy[name] = _hy_conv_call(z1, 0, 0, proj, COL_HY_X2, row0, hy_bias[l], a_tab, b_tab, prs, 1, seq_len, n_seq)

        ssd = _ssd_call(proj, small, tabs[ssd_rows], init_state(state_ssd, l), ssd_dt_bias[l], ssd_A_log[l], ssd_D[l], cfg)
        gdn = _gdn_call(proj, small, tabs[gdn_rows], init_state(state_gdn, l), gdn_A_log[l], gdn_dt_bias[l], cfg)
        ret = _ret_call(proj, tabs[ret_rows], rope_tabs, rope_blocks, init_state(state_ret, l), ret_decay[l], cfg)
        new_ssd.append(ssd[2][n_dec:])
        new_gdn.append(gdn[2][n_dec:])
        new_ret.append(ret[2][n_dec:])

        x1, u2 = _outproj_call(hy["s"], hy["p"], ssd, gdn, ret, proj, stack_rows(ssd_norm_w), stack_rows(gdn_norm_w),
                               w_out_b, x_res, mod4, l, stack_rows(ln1_g), stack_rows(ln1_b), cfg)
        act = _ffn_up_call(u2, w_up, ffn_conv_w, stack_rows(ffn_conv_b), l, cfg)
        x_res, u = _ffn_down_call(act, w_down_b, x1, mod4, l, stack_rows(ln2_g), stack_rows(ln2_b), cfg)

    y_sample = x_res.reshape(n_dec, ls, d)
    y_prompt = u.reshape(n_ctx, lp, d)
    return (y_prompt, y_sample, jnp.stack(new_ssd, 1), jnp.stack(new_gdn, 1), jnp.stack(new_ret, 1))
```

```python
import functools
import math

import numpy as np
import jax
import jax.numpy as jnp
from jax import lax
from jax.experimental import pallas as pl
from jax.experimental.pallas import tpu as pltpu

F32 = jnp.float32
BF16 = jnp.bfloat16
HIGHEST = lax.Precision.HIGHEST

N_DIR = 2
CONV_W = 3
GROUP_WIDTH = 512
HY_ORDER = 2
HY_BANDS = 8
HY_EMB = 1 + 2 * HY_BANDS
HY_FFN = 64
HY_DECAY_TARGET = 1e-2
HY_FAST_PCT = 0.3
HY_SLOW_PCT = 1.5
SSD_HEADS = 8
SSD_HEAD_DIM = 64
SSD_GROUPS = 2
SSD_HPG = SSD_HEADS // SSD_GROUPS
SSD_STATE = 128
SSD_CHUNK = 128
GDN_HEADS = 4
GDN_HEAD_DIM = 128
GDN_CHUNK = 64
RET_HEADS = 4
RET_HEAD_DIM = 128
RET_CHUNK = 128
GRID_W = 64
ROPE_BASE = 10000.0
LN_EPS = 1e-5
RMS_EPS = 1e-6
N_CONV = 8 * GROUP_WIDTH

LANES = 128
SUBLANES = 8
VMEM_LIMIT_BYTES = 60 * 1024 * 1024

COL_HY_V, COL_HY_X1, COL_HY_X2, COL_S_X, COL_S_BC, COL_G_Q, COL_G_K, COL_G_V = range(8)
COL_S_Z, COL_G_Z, COL_R_Q, COL_R_K, COL_R_V, COL_R_G = range(8, 14)
N_MAIN = 14 * GROUP_WIDTH
SM_SDT = 0
SM_GBETA = 16
SM_GA = 24


def _silu(x):
    return x * jax.nn.sigmoid(x)


def _softplus(x):
    return jnp.maximum(x, 0.0) + jnp.log1p(jnp.exp(-jnp.abs(x)))


def _dot(a, b):
    return jnp.dot(a.astype(BF16), b.astype(BF16), preferred_element_type=F32)


def _dot_nt(a, b):
    return lax.dot_general(a.astype(BF16), b.astype(BF16), (((1,), (1,)), ((), ())), preferred_element_type=F32)


def _dot_tn(a, b):
    return lax.dot_general(a.astype(BF16), b.astype(BF16), (((0,), (0,)), ((), ())), preferred_element_type=F32)


def _dot_hi(a, b):
    return jnp.dot(a, b, preferred_element_type=F32, precision=HIGHEST)


def _params(*semantics):
    return pltpu.CompilerParams(dimension_semantics=semantics, vmem_limit_bytes=VMEM_LIMIT_BYTES)


def _tri(n, lower):
    r = lax.broadcasted_iota(jnp.int32, (n, n), 0)
    c = lax.broadcasted_iota(jnp.int32, (n, n), 1)
    return jnp.where((r >= c) if lower else (r <= c), 1.0, 0.0).astype(F32)


def _mod_kernel(c_ref, w_ref, b_ref, o_ref):
    a = _silu(c_ref[...])
    o_ref[...] = _dot(a, w_ref[...]) + b_ref[...]


def _mod_call(cond, w_mod, b_mod):
    depth, d, n = w_mod.shape
    rows = cond.shape[0]
    tn = 1024
    return pl.pallas_call(
        _mod_kernel,
        out_shape=jax.ShapeDtypeStruct((depth, rows, n), F32),
        grid=(depth, n // tn),
        in_specs=[
            pl.BlockSpec((rows, d), lambda l, j: (0, 0)),
            pl.BlockSpec((None, d, tn), lambda l, j: (l, 0, j)),
            pl.BlockSpec((None, 1, tn), lambda l, j: (l, 0, j)),
        ],
        out_specs=pl.BlockSpec((None, rows, tn), lambda l, j: (l, 0, j)),
        compiler_params=_params("arbitrary", "arbitrary"),
        name="adaln_mod",
    )(cond, w_mod, b_mod.reshape(depth, 1, n))


def _mod_spec(layer, k, d, rows_per_tile, l_sample, n_dec):
    tiles_per_seq = l_sample // rows_per_tile

    def index_map(i, *_):
        return (layer, jnp.minimum(i // tiles_per_seq, n_dec), 0, k)

    return pl.BlockSpec((None, None, 1, d), index_map)


def _modulate_kernel(xs_ref, xp_ref, sh_ref, sc_ref, o_ref, *, n_sample_tiles):
    i = pl.program_id(0)
    x = jnp.where(i < n_sample_tiles, xs_ref[...], xp_ref[...])
    o_ref[...] = (x * (1.0 + sc_ref[...]) + sh_ref[...]).astype(BF16)


def _two_group_specs(tm, d, n_sample_tiles, **spec_kwargs):
    return [
        pl.BlockSpec((tm, d), lambda i, *_: (jnp.minimum(i, n_sample_tiles - 1), 0), **spec_kwargs),
        pl.BlockSpec((tm, d), lambda i, *_: (jnp.maximum(i - n_sample_tiles, 0), 0), **spec_kwargs),
    ]


def _modulate_call(xs, xp, mod4, layer, cfg):
    tm, d = 256, cfg["d"]
    t = cfg["t"]
    ns = cfg["ts"] // tm
    return pl.pallas_call(
        functools.partial(_modulate_kernel, n_sample_tiles=ns),
        out_shape=jax.ShapeDtypeStruct((t, d), BF16),
        grid=(t // tm,),
        in_specs=_two_group_specs(tm, d, ns)
        + [_mod_spec(layer, 0, d, tm, cfg["ls"], cfg["n_dec"]), _mod_spec(layer, 1, d, tm, cfg["ls"], cfg["n_dec"])],
        out_specs=pl.BlockSpec((tm, d), lambda i: (i, 0)),
        compiler_params=_params("arbitrary"),
        name="modulate_in",
    )(xs, xp, mod4, mod4)


CONV_ROW_CHUNK = 256


def _matmul_conv_chunks(u_ref, w, cw, cb, h_scr, seq_len, min_seq_len, finish):
    rows = u_ref.shape[0]
    rc = min(CONV_ROW_CHUNK, rows)
    assert rows % rc == 0 and rc % min_seq_len == 0
    n_chunks = rows // rc
    n = w.shape[1]
    halo = SUBLANES
    first = lax.broadcasted_iota(jnp.int32, (SUBLANES, 1), 0) == 0
    last = lax.broadcasted_iota(jnp.int32, (SUBLANES, 1), 0) == SUBLANES - 1
    tile = lambda v, t: v[t * SUBLANES:(t + 1) * SUBLANES]
    h_scr[0:halo, :] = jnp.zeros((halo, n), F32)
    h_scr[halo + rows:2 * halo + rows, :] = jnp.zeros((halo, n), F32)
    for r in range(n_chunks + 1):
        if r < n_chunks:
            h_scr[halo + r * rc:halo + (r + 1) * rc, :] = jnp.dot(u_ref[r * rc:(r + 1) * rc, :], w, preferred_element_type=F32)
        if r == 0:
            continue
        row0 = (r - 1) * rc
        h = h_scr[halo + row0:halo + row0 + rc, :]
        hp = h_scr[halo + row0 - 1:halo + row0 - 1 + rc, :]
        hn = h_scr[halo + row0 + 1:halo + row0 + 1 + rc, :]
        fix_p, fix_n = {}, {}
        for b in range(0, rc, min_seq_len):
            seq_start = ((row0 + b) & (seq_len - 1)) == 0
            seq_end = ((row0 + b + min_seq_len) & (seq_len - 1)) == 0
            tp, tn_ = b // SUBLANES, (b + min_seq_len) // SUBLANES - 1
            fix_p[tp] = jnp.where(jnp.logical_and(first, seq_start), 0.0, tile(hp, tp))
            fix_n[tn_] = jnp.where(jnp.logical_and(last, seq_end), 0.0, tile(hn, tn_))
        n_tiles = rc // SUBLANES
        hp = jnp.concatenate([fix_p.get(t, tile(hp, t)) for t in range(n_tiles)], axis=0)
        hn = jnp.concatenate([fix_n.get(t, tile(hn, t)) for t in range(n_tiles)], axis=0)
        finish(slice(row0, row0 + rc), hp * cw[0:1] + h * cw[1:2] + hn * cw[2:3] + cb)


def _inproj_kernel(u_ref, w_ref, ws_ref, cw_ref, cb_ref, o_ref, os_ref, h_scr, *, n_conv_tiles, n_sample_tiles, ls, lp):
    i = pl.program_id(0)
    j = pl.program_id(1)
    seq_len = jnp.where(i < n_sample_tiles, ls, lp)

    @pl.when(j < n_conv_tiles)
    def _():
        def finish(rows, conv):
            o_ref[rows, :] = conv

        _matmul_conv_chunks(u_ref, w_ref[...], cw_ref[...], cb_ref[...], h_scr, seq_len, min(ls, lp), finish)

    @pl.when(j >= n_conv_tiles)
    def _():
        o_ref[...] = jnp.dot(u_ref[...], w_ref[...], preferred_element_type=F32)

    @pl.when(j == 0)
    def _():
        os_ref[...] = jnp.dot(u_ref[...], ws_ref[...], preferred_element_type=F32)


def _inproj_call(u, w_main, w_small, cw, cb, layer, cfg):
    t, d = u.shape
    tm, tn = cfg["ls"], 512
    n_conv_tiles = N_CONV // tn
    kern = functools.partial(
        _inproj_kernel, n_conv_tiles=n_conv_tiles, n_sample_tiles=cfg["ts"] // tm, ls=cfg["ls"], lp=cfg["lp"]
    )
    conv_col = lambda i, j: (layer, 0, jnp.minimum(j, n_conv_tiles - 1))
    return pl.pallas_call(
        kern,
        out_shape=[jax.ShapeDtypeStruct((t, N_MAIN), F32), jax.ShapeDtypeStruct((t, LANES), F32)],
        grid=(t // tm, N_MAIN // tn),
        in_specs=[
            pl.BlockSpec((tm, d), lambda i, j: (i, 0)),
            pl.BlockSpec((None, d, tn), lambda i, j: (layer, 0, j)),
            pl.BlockSpec((None, d, LANES), lambda i, j: (layer, 0, 0)),
            pl.BlockSpec((None, CONV_W, tn), conv_col),
            pl.BlockSpec((None, 1, tn), conv_col),
        ],
        out_specs=[pl.BlockSpec((tm, tn), lambda i, j: (i, j)), pl.BlockSpec((tm, LANES), lambda i, j: (i, 0))],
        scratch_shapes=[pltpu.VMEM((tm + 2 * SUBLANES, tn), F32)],
        compiler_params=_params("arbitrary", "arbitrary"),
        name="in_proj_conv",
    )(u, w_main, w_small, cw, cb)


@functools.lru_cache(maxsize=None)
def _dft_tables(l):
    n = 2 * l
    k = np.arange(l, dtype=np.int64)
    ang = (2.0 * math.pi / n) * ((k[:, None] * k[None, :]) % n)
    cos, sin = np.cos(ang), np.sin(ang)
    alt = np.where(k % 2 == 0, 1.0, -1.0)
    a_im = np.where(k[:, None] == 0, alt[None, :], -sin)
    a = np.concatenate([cos, a_im], axis=0)
    ck = np.where(k == 0, 1.0, 2.0) / n
    b_re = cos * ck[None, :]
    b_im = np.where(k[None, :] == 0, alt[:, None] / n, -sin * (2.0 / n))
    b = np.concatenate([b_re, b_im], axis=1)
    return a.astype(np.float32), b.astype(np.float32)


@functools.lru_cache(maxsize=None)
def _hy_feats(l):
    pos = np.arange(l, dtype=np.float32)
    t = pos / np.float32(l - 1)
    bands = np.linspace(1e-4, HY_BANDS - 1, HY_BANDS, dtype=np.float32)
    ang = np.float32(2.0 * math.pi / l) * pos[:, None] * bands
    feats = np.concatenate([t[:, None], np.cos(ang), -np.sin(ang)], -1).astype(np.float32)
    feats = np.pad(feats, ((0, 0), (0, LANES - HY_EMB)))
    return feats, t[:, None]


@functools.lru_cache(maxsize=None)
def _hy_deltas(c):
    lo, hi = math.log(HY_DECAY_TARGET) / HY_SLOW_PCT, math.log(HY_DECAY_TARGET) / HY_FAST_PCT
    return np.abs(np.linspace(lo, hi, c, dtype=np.float32))[None, :]


def _hy_filter_kernel(
    feats_ref, t_ref, dl_ref, w1_ref, b1_ref, w2_ref, b2_ref, w3_ref, fq_ref, are_ref, aim_ref,
    p_ref, r_ref, s_ref, sum_scr, dif_scr, nyq_scr,
):
    f = pl.program_id(0)
    l = feats_ref.shape[0]
    c = dl_ref.shape[1]
    kb = are_ref.shape[0]

    @pl.when(f == 0)
    def _():
        fq = fq_ref[...]
        h = jnp.sin(fq[0:1] * (_dot_hi(feats_ref[...], w1_ref[...]) + b1_ref[...]))
        h = jnp.sin(fq[1:2] * (_dot_hi(h, w2_ref[...]) + b2_ref[...]))
        decay = jnp.exp(-t_ref[...] * dl_ref[...])
        row = lax.broadcasted_iota(jnp.int32, (l, 1), 0)
        sign = jnp.where((row & 1) == 0, 1.0, -1.0)
        for o in range(HY_ORDER):
            fwd = _dot_hi(h, w3_ref[:, (2 * o) * c:(2 * o + 1) * c]) * decay
            bwd = _dot_hi(h, w3_ref[:, (2 * o + 1) * c:(2 * o + 2) * c]) * decay
            bwd = jnp.where(row == 0, 0.0, bwd)
            norm = jnp.sum(jnp.abs(fwd), axis=0, keepdims=True) + jnp.sum(jnp.abs(bwd), axis=0, keepdims=True)
            fwd = fwd / norm
            bwd = bwd / norm
            ssum = fwd + bwd
            sum_scr[o] = ssum.astype(BF16)
            dif_scr[o] = (fwd - bwd).astype(BF16)
            nyq_scr[o] = jnp.broadcast_to(jnp.sum(ssum * sign, axis=0, keepdims=True), (SUBLANES, c))

    grow = f * kb + lax.broadcasted_iota(jnp.int32, (kb, 1), 0)
    for o in range(HY_ORDER):
        kre = jnp.dot(are_ref[...], sum_scr[o], preferred_element_type=F32)
        kim = jnp.dot(aim_ref[...], dif_scr[o], preferred_element_type=F32)
        p_ref[o] = kre
        r_ref[o] = jnp.where(grow == 0, 0.0, kim)
        s_ref[o] = jnp.where(grow == 0, nyq_scr[o][0:1], kre)


def _hy_filter_call(l, a_tab, w1, b1, w2, b2, w3, freq):
    c = GROUP_WIDTH
    kb = min(l, 256)
    nf = l // kb
    feats, t = _hy_feats(l)
    w1p = jnp.pad(w1, ((0, LANES - HY_EMB), (0, 0)))
    full = lambda shape: pl.BlockSpec(shape, lambda f: tuple(0 for _ in shape))
    out = jax.ShapeDtypeStruct((HY_ORDER, l, c), F32)
    out_spec = pl.BlockSpec((HY_ORDER, kb, c), lambda f: (0, f, 0))
    return pl.pallas_call(
        _hy_filter_kernel,
        out_shape=[out, out, out],
        grid=(nf,),
        in_specs=[
            full((l, LANES)), full((l, 1)), full((1, c)), full((LANES, HY_FFN)), full((1, HY_FFN)),
            full((HY_FFN, HY_FFN)), full((1, HY_FFN)), full((HY_FFN, 2 * HY_ORDER * c)), full((2, HY_FFN)),
            pl.BlockSpec((kb, l), lambda f: (f, 0)),
            pl.BlockSpec((kb, l), lambda f: (f + nf, 0)),
        ],
        out_specs=[out_spec, out_spec, out_spec],
        scratch_shapes=[
            pltpu.VMEM((HY_ORDER, l, c), BF16), pltpu.VMEM((HY_ORDER, l, c), BF16),
            pltpu.VMEM((HY_ORDER, SUBLANES, c), F32),
        ],
        compiler_params=_params("arbitrary"),
        name="hyena_filter",
    )(feats, t, _hy_deltas(c), w1p, b1[None, :], w2, b2[None, :], w3, freq, a_tab, a_tab)


HY_ROWS_PER_STEP = 1024


def _hy_conv_kernel(z_ref, g_ref, bias_ref, are_ref, aim_ref, bre_ref, bim_ref, p_ref, r_ref, s_ref, o_ref, zb_scr,
                    *, seq_len):
    f = pl.program_id(2)

    @pl.when(f == 0)
    def _():
        zb_scr[...] = z_ref[...].astype(BF16)
        o_ref[...] = jnp.zeros_like(o_ref)

    p, r, s = p_ref[...], r_ref[...], s_ref[...]
    for b in range(z_ref.shape[0] // seq_len):
        rows = slice(b * seq_len, (b + 1) * seq_len)
        zb = zb_scr[rows, :]
        re = jnp.dot(are_ref[...], zb, preferred_element_type=F32)
        im = jnp.dot(aim_ref[...], zb, preferred_element_type=F32)
        re2 = re * p - im * r
        im2 = re * r + im * s
        o_ref[rows, :] += _dot(bre_ref[...], re2) + _dot(bim_ref[...], im2)

    @pl.when(f == pl.num_programs(2) - 1)
    def _():
        o_ref[...] = g_ref[...] * (o_ref[...] + z_ref[...] * bias_ref[...])


def _hy_conv_call(z, z_col, z_row0, g, g_col, g_row0, bias, a_tab, b_tab, prs, order, l, n_seq):
    c = GROUP_WIDTH
    ct = c
    ncb = c // ct
    kb = min(l, 512)
    nf = l // kb
    p, r, s = prs
    nb = max(n for n in (1, 2, 4, 8) if n_seq % n == 0 and n * l <= max(l, HY_ROWS_PER_STEP))
    rows = nb * l
    assert z_row0 % rows == 0 and g_row0 % rows == 0
    zspec = pl.BlockSpec((rows, ct), lambda b, cb, f: (z_row0 // rows + b, z_col * ncb + cb))
    gspec = pl.BlockSpec((rows, ct), lambda b, cb, f: (g_row0 // rows + b, g_col * ncb + cb))
    kspec = pl.BlockSpec((None, kb, ct), lambda b, cb, f: (order, f, cb))
    return pl.pallas_call(
        functools.partial(_hy_conv_kernel, seq_len=l),
        out_shape=jax.ShapeDtypeStruct((n_seq * l, c), F32),
        grid=(n_seq // nb, ncb, nf),
        in_specs=[
            zspec, gspec,
            pl.BlockSpec((None, 1, ct), lambda b, cb, f: (order, 0, cb)),
            pl.BlockSpec((kb, l), lambda b, cb, f: (f, 0)),
            pl.BlockSpec((kb, l), lambda b, cb, f: (f + nf, 0)),
            pl.BlockSpec((l, kb), lambda b, cb, f: (0, f)),
            pl.BlockSpec((l, kb), lambda b, cb, f: (0, f + nf)),
            kspec, kspec, kspec,
        ],
        out_specs=pl.BlockSpec((rows, ct), lambda b, cb, f: (b, cb)),
        scratch_shapes=[pltpu.VMEM((rows, ct), BF16)],
        compiler_params=_params("arbitrary", "arbitrary", "arbitrary"),
        name="hyena_conv",
    )(z, g, bias.reshape(HY_ORDER, 1, c), a_tab, a_tab, b_tab, b_tab, p, r, s)


CHUNKS_PER_STEP = 2
GDN_CHUNKS_PER_STEP = 4


def _scan_order(d, n=CHUNKS_PER_STEP):
    return list(range(n)) if d == 0 else list(range(n - 1, -1, -1))


def _chunk_tables(cfg, q):
    fwd, bwd, first, last, seq = [], [], [], [], []
    sid = 0
    for n_seq, l, row0 in ((cfg["n_dec"], cfg["ls"], 0), (cfg["n_ctx"], cfg["lp"], cfg["ts"])):
        nc = l // q
        for b in range(n_seq):
            base = (row0 + b * l) // q
            for c in range(nc):
                fwd.append(base + c)
                bwd.append(base + nc - 1 - c)
                first.append(int(c == 0))
                last.append(int(c == nc - 1))
                seq.append(sid)
            sid += 1
    return tuple(jnp.asarray(np.asarray(a, np.int32)) for a in (fwd, bwd, first, last, seq))


def _scan_masks(q, d):
    r = lax.broadcasted_iota(jnp.int32, (q, q), 0)
    c = lax.broadcasted_iota(jnp.int32, (q, q), 1)
    return ((r >= c), (r > c)) if d == 0 else ((r <= c), (r < c))


def _cumsum_pair(a_col, a_row, q, d):
    lo, up = _tri(q, True), _tri(q, False)
    if d == 0:
        return _dot_hi(lo, a_col), _dot_hi(a_row, up)
    return _dot_hi(up, a_col), _dot_hi(a_row, lo)


def _ssd_kernel(
    fwd_t, bwd_t, first_t, last_t, seq_t,
    xf_ref, bcf_ref, smf_ref, xb_ref, bcb_ref, smb_ref, pr_ref, pc_ref, dsk_ref, h0_ref,
    yf_ref, yb_ref, hout_ref, h_scr,
):
    s = pl.program_id(0)
    q, p, n = SSD_CHUNK, SSD_HEAD_DIM, SSD_STATE

    @pl.when(first_t[s] == 1)
    def _():
        h_scr[...] = h0_ref[...]

    bias_r, alog_r = pr_ref[0:1], pr_ref[1:2]
    bias_c, alog_c = pc_ref[:, 0:1], pc_ref[:, 1:2]
    assert q == LANES == n and 2 * p == LANES
    even_half = lax.broadcasted_iota(jnp.int32, (q, LANES), 1) < p
    probs = []
    for d in range(N_DIR):
        x_ref, bc_ref, sm_ref = (xf_ref, bcf_ref, smf_ref) if d == 0 else (xb_ref, bcb_ref, smb_ref)
        incl, _ = _scan_masks(q, d)
        end = q - 1 if d == 0 else 0
        for k in _scan_order(d):
            rows = slice(k * q, (k + 1) * q)
            x = _silu(x_ref[rows, :])
            x_t = x.T
            bc = _silu(bc_ref[rows, :])
            sm = sm_ref[rows, :]
            a_c = _softplus(sm + bias_r) * (-jnp.exp(alog_r))
            dt_r = _softplus(sm.T + bias_c)
            acs_c, acs_r = _cumsum_pair(a_c, (dt_r * (-jnp.exp(alog_c)))[SM_SDT:SM_SDT + N_DIR * SSD_HEADS], q, d)
            for g in range(SSD_GROUPS):
                bm = bc[:, g * n:(g + 1) * n]
                cm = bc[:, (SSD_GROUPS + g) * n:(SSD_GROUPS + g + 1) * n]
                cb = _dot_nt(cm, bm)
                for e in range(SSD_HPG):
                    hd = g * SSD_HPG + e
                    idx = d * SSD_HEADS + hd
                    pair = hd // 2
                    row = acs_r[idx:idx + 1, :]
                    dt_row = dt_r[SM_SDT + idx:SM_SDT + idx + 1, :]
                    tot = row[:, end:end + 1]
                    probs.append(dict(
                        d=d, k=k, hd=hd, bm=bm, cm=cm, cb=cb, incl=incl, row=row, dt_row=dt_row, tot=tot,
                        colb=jnp.broadcast_to(acs_c[:, idx:idx + 1], (q, q)),
                        x_pair=x[:, pair * LANES:(pair + 1) * LANES], x_t=x_t[hd * p:(hd + 1) * p, :],
                        skip=dsk_ref[d:d + 1, pair * LANES:(pair + 1) * LANES],
                    ))
    lmat = [jnp.exp(jnp.where(pb["incl"], pb["colb"] - pb["row"], -1e30)) * pb["dt_row"] for pb in probs]
    y_diag = [_dot(pb["cb"] * lm, pb["x_pair"]) for pb, lm in zip(probs, lmat)]
    states = [_dot(pb["x_t"] * (pb["dt_row"] * jnp.exp(pb["tot"] - pb["row"])), pb["bm"]) for pb in probs]
    c_dec = [pb["cm"] * jnp.exp(pb["colb"]) for pb in probs]
    state = {(d, hd): h_scr[d, hd] for d in range(N_DIR) for hd in range(SSD_HEADS)}
    ys = {}
    for step in range(CHUNKS_PER_STEP):
        group = [(i, pb) for i, pb in enumerate(probs) if _scan_order(pb["d"])[step] == pb["k"]]
        pair_state = {
            (d, pr_): jnp.concatenate([state[d, 2 * pr_], state[d, 2 * pr_ + 1]], axis=0)
            for d in range(N_DIR) for pr_ in range(SSD_HEADS // 2)
        }
        y_off = [_dot_nt(c_dec[i], pair_state[pb["d"], pb["hd"] // 2]) for i, pb in group]
        for (i, pb), yo in zip(group, y_off):
            ys[pb["d"], pb["k"], pb["hd"]] = y_diag[i] + yo
            state[pb["d"], pb["hd"]] = state[pb["d"], pb["hd"]] * jnp.exp(pb["tot"]) + states[i]
    for (d, hd), h_new in state.items():
        h_scr[d, hd] = h_new
    by_key = {(pb["d"], pb["k"], pb["hd"]): pb for pb in probs}
    for d, y_ref in ((0, yf_ref), (1, yb_ref)):
        for k in range(CHUNKS_PER_STEP):
            for pr_ in range(SSD_HEADS // 2):
                pb = by_key[d, k, 2 * pr_]
                y_pair = jnp.where(even_half, ys[d, k, 2 * pr_], ys[d, k, 2 * pr_ + 1]) + pb["skip"] * pb["x_pair"]
                y_ref[k * q:(k + 1) * q, pr_ * LANES:(pr_ + 1) * LANES] = y_pair.astype(BF16)

    @pl.when(last_t[s] == 1)
    def _():
        hout_ref[...] = h_scr[...]


def _ssd_call(proj, small, tables, h0, dt_bias, a_log, d_skip, cfg):
    q, w = CHUNKS_PER_STEP * SSD_CHUNK, GROUP_WIDTH
    t = proj.shape[0]
    n_steps = tables[0].shape[0]
    nlane = N_DIR * SSD_HEADS
    pr = jnp.zeros((SUBLANES, LANES), F32)
    pr = pr.at[0, SM_SDT:SM_SDT + nlane].set(dt_bias.reshape(-1)).at[1, SM_SDT:SM_SDT + nlane].set(a_log.reshape(-1))
    pc = pr.T
    dsk = jnp.repeat(d_skip, SSD_HEAD_DIM, axis=-1)
    st_shape = (N_DIR, SSD_HEADS, SSD_HEAD_DIM, SSD_STATE)
    fmap = lambda col: (lambda s, fw, bw, fi, la, sq: (fw[s], col))
    bmap = lambda col: (lambda s, fw, bw, fi, la, sq: (bw[s], col))
    const = lambda shape: pl.BlockSpec(shape, lambda s, *_: tuple(0 for _ in shape))
    st_spec = pl.BlockSpec((None,) + st_shape, lambda s, fw, bw, fi, la, sq: (sq[s], 0, 0, 0, 0))
    grid_spec = pltpu.PrefetchScalarGridSpec(
        num_scalar_prefetch=5,
        grid=(n_steps,),
        in_specs=[
            pl.BlockSpec((q, w), fmap(COL_S_X)), pl.BlockSpec((q, w), fmap(COL_S_BC)), pl.BlockSpec((q, LANES), fmap(0)),
            pl.BlockSpec((q, w), bmap(COL_S_X)), pl.BlockSpec((q, w), bmap(COL_S_BC)), pl.BlockSpec((q, LANES), bmap(0)),
            const((SUBLANES, LANES)), const((LANES, SUBLANES)), const((N_DIR, w)), st_spec,
        ],
        out_specs=[pl.BlockSpec((q, w), fmap(0)), pl.BlockSpec((q, w), bmap(0)), st_spec],
        scratch_shapes=[pltpu.VMEM(st_shape, F32)],
    )
    n_seq = cfg["n_dec"] + cfg["n_ctx"]
    return pl.pallas_call(
        _ssd_kernel,
        out_shape=[
            jax.ShapeDtypeStruct((t, w), BF16), jax.ShapeDtypeStruct((t, w), BF16),
            jax.ShapeDtypeStruct((n_seq,) + st_shape, F32),
        ],
        grid_spec=grid_spec,
        compiler_params=_params("arbitrary"),
        name="ssd_scan",
    )(*tables, proj, proj, small, proj, proj, small, pr, pc, dsk, h0)


TRI_BLOCK = 2 * SUBLANES


def _unit_tri_inverses(mats, lowers):
    n = mats[0].shape[0]
    blk = TRI_BLOCK
    sub = blk // SUBLANES
    lane = lax.broadcasted_iota(jnp.int32, (blk, n), 1)
    row = lax.broadcasted_iota(jnp.int32, (blk, n), 0)
    group = (lax.broadcasted_iota(jnp.int32, (SUBLANES, LANES), 1) // blk) * blk
    tiles = lambda v: [v[t * SUBLANES:(t + 1) * SUBLANES, :] for t in range(sub)]
    eye = jnp.where(lane % blk == row, 1.0, 0.0)
    d_t, inv_t = [], []
    for a in mats:
        packed = jnp.zeros((blk, n), F32)
        for bi in range(n // blk):
            packed = jnp.where(lane // blk == bi, a[bi * blk:(bi + 1) * blk, :], packed)
        if n < LANES:
            packed = jnp.concatenate([packed, jnp.zeros((blk, LANES - n), F32)], axis=1)
        d_t.append(tiles(packed))
        inv_t.append(tiles(eye))
    for step in range(blk):
        for i, lower in enumerate(lowers):
            jj = step if lower else blk - 1 - step
            st, jr = divmod(jj, SUBLANES)
            done_row = inv_t[i][st][jr:jr + 1, :]
            for s in (range(st, sub) if lower else range(st + 1)):
                col = jnp.take_along_axis(d_t[i][s], group + jj, axis=1)[:, :n]
                inv_t[i][s] = inv_t[i][s] - col * done_row
    ts = []
    for i in range(len(mats)):
        inv = jnp.concatenate(inv_t[i], axis=0)
        ts.append(jnp.concatenate([jnp.where(lane // blk == bi, inv, 0.0) for bi in range(n // blk)], axis=0))
    r = lax.broadcasted_iota(jnp.int32, (n, n), 0)
    c = lax.broadcasted_iota(jnp.int32, (n, n), 1)
    size = blk
    while size < n:
        level = (r // (2 * size) == c // (2 * size)) & (r // size != c // size)
        half = [_dot(t, jnp.where(level, a, 0.0)) for t, a in zip(ts, mats)]
        ts = [t - _dot(h, t) for t, h in zip(ts, half)]
        size *= 2
    return ts


def _l2norm_heads(x, heads, dim, scale):
    outs = []
    for h in range(heads):
        xh = x[:, h * dim:(h + 1) * dim]
        outs.append(xh * (lax.rsqrt(jnp.sum(xh * xh, axis=-1, keepdims=True) + 1e-6) * scale))
    return outs


def _gdn_kernel(
    fwd_t, bwd_t, first_t, last_t, seq_t,
    qf_ref, kf_ref, vf_ref, smf_ref, qb_ref, kb_ref, vb_ref, smb_ref, pr_ref, pc_ref, s0_ref,
    of_ref, ob_ref, sout_ref, s_scr,
):
    s = pl.program_id(0)
    q, dim = GDN_CHUNK, GDN_HEAD_DIM

    @pl.when(first_t[s] == 1)
    def _():
        s_scr[...] = s0_ref[...]

    bias_r, alog_r = pr_ref[0:1], pr_ref[1:2]
    bias_c, alog_c = pc_ref[:, 0:1], pc_ref[:, 1:2]
    n_sub = GDN_CHUNKS_PER_STEP
    probs = []
    for d in range(N_DIR):
        q_ref, k_ref, v_ref, sm_ref = (qf_ref, kf_ref, vf_ref, smf_ref) if d == 0 else (qb_ref, kb_ref, vb_ref, smb_ref)
        incl, strict = _scan_masks(q, d)
        end = q - 1 if d == 0 else 0
        for c in _scan_order(d, n_sub):
            rows = slice(c * q, (c + 1) * q)
            qs = _l2norm_heads(_silu(q_ref[rows, :]), GDN_HEADS, dim, dim ** -0.5)
            ks = _l2norm_heads(_silu(k_ref[rows, :]), GDN_HEADS, dim, 1.0)
            v = _silu(v_ref[rows, :])
            sm = sm_ref[rows, :]
            beta_c = jax.nn.sigmoid(sm)
            g_c = -jnp.exp(alog_r) * _softplus(sm + bias_r)
            g_r = -jnp.exp(alog_c) * _softplus(sm.T + bias_c)
            gc_c, gc_r = _cumsum_pair(g_c, g_r[SM_GA:SM_GA + N_DIR * GDN_HEADS], q, d)
            for h in range(GDN_HEADS):
                idx = d * GDN_HEADS + h
                row = gc_r[idx:idx + 1, :]
                probs.append(dict(
                    d=d, c=c, h=h, q=qs[h], k=ks[h], v=v[:, h * dim:(h + 1) * dim], row=row, tot=row[:, end:end + 1],
                    colb=jnp.broadcast_to(gc_c[:, SM_GA + idx:SM_GA + idx + 1], (q, dim)),
                    betab=jnp.broadcast_to(beta_c[:, SM_GBETA + idx:SM_GBETA + idx + 1], (q, dim)),
                    incl=incl, strict=strict,
                ))
    k16 = [p["k"].astype(BF16) for p in probs]
    kk = [_dot_nt(k, k) for k in k16]
    qk = [_dot_nt(p["q"], k) for p, k in zip(probs, k16)]
    dmask = [jnp.exp(jnp.where(p["incl"], p["colb"][:, :q] - p["row"], -1e30)) for p in probs]
    a = [jnp.where(p["strict"], p["betab"][:, :q] * kk_i * dm, 0.0) for p, kk_i, dm in zip(probs, kk, dmask)]
    ecol = [jnp.exp(p["colb"]) for p in probs]
    rhs = [
        jnp.concatenate([p["v"] * p["betab"], p["k"] * (p["betab"] * e)], axis=-1).astype(BF16)
        for p, e in zip(probs, ecol)
    ]
    inv = _unit_tri_inverses(a, [p["d"] == 0 for p in probs])
    sol = [_dot(t, r) for t, r in zip(inv, rhs)]
    u_sol = [x[:, :dim] for x in sol]
    w_sol = [x[:, dim:].astype(BF16) for x in sol]
    attn = [(qk_i * dm).astype(BF16) for qk_i, dm in zip(qk, dmask)]
    q_dec = [(p["q"] * e).astype(BF16) for p, e in zip(probs, ecol)]
    k_dec = [(p["k"] * jnp.exp(p["tot"] - p["colb"])).astype(BF16) for p in probs]
    state = {(d, h): s_scr[d, h] for d in range(N_DIR) for h in range(GDN_HEADS)}
    outs = {}
    for step in range(n_sub):
        group = [(i, p) for i, p in enumerate(probs) if _scan_order(p["d"], n_sub)[step] == p["c"]]
        w_s = [_dot(w_sol[i], state[p["d"], p["h"]]) for i, p in group]
        q_s = [_dot(q_dec[i], state[p["d"], p["h"]]) for i, p in group]
        vv = [u_sol[i] - ws for (i, p), ws in zip(group, w_s)]
        o_at = [_dot(attn[i], vv_i) for (i, p), vv_i in zip(group, vv)]
        upd = [_dot_tn(k_dec[i], vv_i) for (i, p), vv_i in zip(group, vv)]
        for (i, p), qs_i, oa, u_i in zip(group, q_s, o_at, upd):
            outs[p["d"], p["c"], p["h"]] = qs_i + oa
            state[p["d"], p["h"]] = state[p["d"], p["h"]] * jnp.exp(p["tot"]) + u_i
    for (d, h), s_new in state.items():
        s_scr[d, h] = s_new
    for d, o_ref in ((0, of_ref), (1, ob_ref)):
        for c in range(n_sub):
            o_ref[c * q:(c + 1) * q, :] = jnp.concatenate([outs[d, c, h] for h in range(GDN_HEADS)], axis=-1).astype(BF16)

    @pl.when(last_t[s] == 1)
    def _():
        sout_ref[...] = s_scr[...]


def _gdn_call(proj, small, tables, s0, a_log, dt_bias, cfg):
    q, w = GDN_CHUNKS_PER_STEP * GDN_CHUNK, GROUP_WIDTH
    t = proj.shape[0]
    n_steps = tables[0].shape[0]
    nlane = N_DIR * GDN_HEADS
    pr = jnp.zeros((SUBLANES, LANES), F32)
    pr = pr.at[0, SM_GA:SM_GA + nlane].set(dt_bias.reshape(-1)).at[1, SM_GA:SM_GA + nlane].set(a_log.reshape(-1))
    pc = pr.T
    st_shape = (N_DIR, GDN_HEADS, GDN_HEAD_DIM, GDN_HEAD_DIM)
    fmap = lambda col: (lambda s, fw, bw, fi, la, sq: (fw[s], col))
    bmap = lambda col: (lambda s, fw, bw, fi, la, sq: (bw[s], col))
    const = lambda shape: pl.BlockSpec(shape, lambda s, *_: tuple(0 for _ in shape))
    st_spec = pl.BlockSpec((None,) + st_shape, lambda s, fw, bw, fi, la, sq: (sq[s], 0, 0, 0, 0))
    blk = lambda m, col: pl.BlockSpec((q, w), m(col))
    grid_spec = pltpu.PrefetchScalarGridSpec(
        num_scalar_prefetch=5,
        grid=(n_steps,),
        in_specs=[
            blk(fmap, COL_G_Q), blk(fmap, COL_G_K), blk(fmap, COL_G_V), pl.BlockSpec((q, LANES), fmap(0)),
            blk(bmap, COL_G_Q), blk(bmap, COL_G_K), blk(bmap, COL_G_V), pl.BlockSpec((q, LANES), bmap(0)),
            const((SUBLANES, LANES)), const((LANES, SUBLANES)), st_spec,
        ],
        out_specs=[pl.BlockSpec((q, w), fmap(0)), pl.BlockSpec((q, w), bmap(0)), st_spec],
        scratch_shapes=[pltpu.VMEM(st_shape, F32)],
    )
    n_seq = cfg["n_dec"] + cfg["n_ctx"]
    return pl.pallas_call(
        _gdn_kernel,
        out_shape=[
            jax.ShapeDtypeStruct((t, w), BF16), jax.ShapeDtypeStruct((t, w), BF16),
            jax.ShapeDtypeStruct((n_seq,) + st_shape, F32),
        ],
        grid_spec=grid_spec,
        compiler_params=_params("arbitrary"),
        name="gdn_scan",
    )(*tables, proj, proj, proj, small, proj, proj, proj, small, pr, pc, s0)


def _rope_tables(cfg):
    l = cfg["ls"]
    pos = np.arange(l)
    r = (pos // GRID_W).astype(np.float32)
    col = (pos % GRID_W).astype(np.float32)
    nf = RET_HEAD_DIM // 4
    inv = np.power(np.float32(ROPE_BASE), -np.arange(nf, dtype=np.float32) / np.float32(nf)).astype(np.float32)
    ang = np.concatenate([r[:, None] * inv, col[:, None] * inv], -1)
    cos, sin = np.cos(ang), np.sin(ang)
    cos2 = np.concatenate([cos, cos], -1)
    sin2 = np.concatenate([-sin, sin], -1)
    ident = np.ones((CHUNKS_PER_STEP * RET_CHUNK, RET_HEAD_DIM), np.float32)
    return np.concatenate([cos2, ident], 0).astype(np.float32), np.concatenate([sin2, 0.0 * ident], 0).astype(np.float32)


def _rope_block_table(cfg):
    q = CHUNKS_PER_STEP * RET_CHUNK
    nc_s, nc_p = cfg["ls"] // q, cfg["lp"] // q
    fwd = [c for _ in range(cfg["n_dec"]) for c in range(nc_s)] + [nc_s] * (cfg["n_ctx"] * nc_p)
    bwd = [nc_s - 1 - c for _ in range(cfg["n_dec"]) for c in range(nc_s)] + [nc_s] * (cfg["n_ctx"] * nc_p)
    return jnp.asarray(np.asarray(fwd, np.int32)), jnp.asarray(np.asarray(bwd, np.int32))


def _ret_kernel(
    fwd_t, bwd_t, first_t, last_t, seq_t, rf_t, rb_t,
    qf_ref, kf_ref, vf_ref, cf_ref, sf_ref, qb_ref, kb_ref, vb_ref, cb_ref, sb_ref, dec_ref, s0_ref,
    of_ref, ob_ref, sout_ref, s_scr,
):
    s = pl.program_id(0)
    q, dim = RET_CHUNK, RET_HEAD_DIM

    @pl.when(first_t[s] == 1)
    def _():
        s_scr[...] = s0_ref[...]

    lg_all = -jnp.exp(dec_ref[...])
    ri = lax.broadcasted_iota(jnp.int32, (q, q), 0)
    ci = lax.broadcasted_iota(jnp.int32, (q, q), 1)
    rpos = lax.broadcasted_iota(jnp.int32, (q, dim), 0)
    n_sub = CHUNKS_PER_STEP
    probs = []
    for d in range(N_DIR):
        q_ref, k_ref, v_ref, c_ref, sn_ref = (
            (qf_ref, kf_ref, vf_ref, cf_ref, sf_ref) if d == 0 else (qb_ref, kb_ref, vb_ref, cb_ref, sb_ref)
        )
        rel = ((ri - ci) if d == 0 else (ci - ri)).astype(F32)
        pos = (rpos if d == 0 else (q - 1 - rpos)).astype(F32)
        for c in _scan_order(d, n_sub):
            rows = slice(c * q, (c + 1) * q)
            qa, ka, va = q_ref[rows, :], k_ref[rows, :], v_ref[rows, :]
            cos, sin = c_ref[rows, :], sn_ref[rows, :]
            for h in range(RET_HEADS):
                idx = d * RET_HEADS + h
                qh = qa[:, h * dim:(h + 1) * dim]
                kh = ka[:, h * dim:(h + 1) * dim] * dim ** -0.5
                probs.append(dict(
                    d=d, c=c, h=h, lg=lg_all[idx:idx + 1, :], rel=rel, pos=pos, v=va[:, h * dim:(h + 1) * dim],
                    q=qh * cos + pltpu.roll(qh, dim // 2, axis=1) * sin,
                    k=kh * cos + pltpu.roll(kh, dim // 2, axis=1) * sin,
                ))
    qk = [_dot_nt(pb["q"], pb["k"]) for pb in probs]
    dmask = [jnp.exp(jnp.where(pb["rel"] >= 0, pb["rel"] * pb["lg"], -1e30)) for pb in probs]
    o_attn = [_dot(qk_i * dm, pb["v"]) for pb, qk_i, dm in zip(probs, qk, dmask)]
    q_dec = [pb["q"] * jnp.exp((pb["pos"] + 1.0) * pb["lg"]) for pb in probs]
    upd = [_dot_tn(pb["k"] * jnp.exp((q - 1.0 - pb["pos"]) * pb["lg"]), pb["v"]) for pb in probs]
    state = {(d, h): s_scr[d, h] for d in range(N_DIR) for h in range(RET_HEADS)}
    outs = {}
    for step in range(n_sub):
        group = [(i, pb) for i, pb in enumerate(probs) if _scan_order(pb["d"], n_sub)[step] == pb["c"]]
        o_state = [_dot(q_dec[i], state[pb["d"], pb["h"]]) for i, pb in group]
        for (i, pb), os_i in zip(group, o_state):
            outs[pb["d"], pb["c"], pb["h"]] = os_i + o_attn[i]
            state[pb["d"], pb["h"]] = state[pb["d"], pb["h"]] * jnp.exp(q * pb["lg"]) + upd[i]
    for (d, h), s_new in state.items():
        s_scr[d, h] = s_new
    for d, o_ref in ((0, of_ref), (1, ob_ref)):
        for c in range(n_sub):
            o_ref[c * q:(c + 1) * q, :] = jnp.concatenate([outs[d, c, h] for h in range(RET_HEADS)], axis=-1).astype(BF16)

    @pl.when(last_t[s] == 1)
    def _():
        sout_ref[...] = s_scr[...]


def _ret_call(proj, tables, rope_tabs, rope_blocks, s0, ret_decay, cfg):
    q, w, dim = CHUNKS_PER_STEP * RET_CHUNK, GROUP_WIDTH, RET_HEAD_DIM
    t = proj.shape[0]
    n_steps = tables[0].shape[0]
    dec = jnp.broadcast_to(ret_decay.reshape(-1, 1), (N_DIR * RET_HEADS, LANES))
    st_shape = (N_DIR, RET_HEADS, dim, dim)
    fmap = lambda col: (lambda s, fw, bw, fi, la, sq, rf, rb: (fw[s], col))
    bmap = lambda col: (lambda s, fw, bw, fi, la, sq, rf, rb: (bw[s], col))
    rfmap = lambda s, fw, bw, fi, la, sq, rf, rb: (rf[s], 0)
    rbmap = lambda s, fw, bw, fi, la, sq, rf, rb: (rb[s], 0)
    st_spec = pl.BlockSpec((None,) + st_shape, lambda s, fw, bw, fi, la, sq, rf, rb: (sq[s], 0, 0, 0, 0))
    blk = lambda m, col: pl.BlockSpec((q, w), m(col))
    rope = lambda m: pl.BlockSpec((q, dim), m)
    grid_spec = pltpu.PrefetchScalarGridSpec(
        num_scalar_prefetch=7,
        grid=(n_steps,),
        in_specs=[
            blk(fmap, COL_R_Q), blk(fmap, COL_R_K), blk(fmap, COL_R_V), rope(rfmap), rope(rfmap),
            blk(bmap, COL_R_Q), blk(bmap, COL_R_K), blk(bmap, COL_R_V), rope(rbmap), rope(rbmap),
            pl.BlockSpec((N_DIR * RET_HEADS, LANES), lambda s, *_: (0, 0)), st_spec,
        ],
        out_specs=[pl.BlockSpec((q, w), fmap(0)), pl.BlockSpec((q, w), bmap(0)), st_spec],
        scratch_shapes=[pltpu.VMEM(st_shape, F32)],
    )
    n_seq = cfg["n_dec"] + cfg["n_ctx"]
    cos2, sin2 = rope_tabs
    return pl.pallas_call(
        _ret_kernel,
        out_shape=[
            jax.ShapeDtypeStruct((t, w), BF16), jax.ShapeDtypeStruct((t, w), BF16),
            jax.ShapeDtypeStruct((n_seq,) + st_shape, F32),
        ],
        grid_spec=grid_spec,
        compiler_params=_params("arbitrary"),
        name="retention_scan",
    )(*tables, *rope_blocks, proj, proj, proj, cos2, sin2, proj, proj, proj, cos2, sin2, dec, s0)


def _layer_norm_rows(y, g, b):
    mu = jnp.mean(y, axis=-1, keepdims=True)
    yc = y - mu
    var = jnp.mean(yc * yc, axis=-1, keepdims=True)
    return yc * lax.rsqrt(var + LN_EPS) * g + b


OUTPROJ_ROW_CHUNK = 256


def _outproj_kernel(
    hys_ref, hyp_ref, sf_ref, sb_ref, sz_ref, gf_ref, gb_ref, gz_ref, rf_ref, rb_ref, rg_ref,
    snw_ref, gnw_ref, w_ref, xs_ref, xp_ref, g1_ref, sh2_ref, sc2_ref, lng_ref, lnb_ref,
    x1_ref, u2_ref, lhs_scr, *, n_sample_tiles, alpha, two_group_x,
):
    i = pl.program_id(0)
    is_sample = i < n_sample_tiles
    w = GROUP_WIDTH
    tm = x1_ref.shape[0]
    rc = min(OUTPROJ_ROW_CHUNK, tm)
    chunks = [slice(r, r + rc) for r in range(0, tm, rc)]
    gnw = gnw_ref[...]
    for rows in chunks:
        y_hy = jnp.where(is_sample, hys_ref[rows, :], hyp_ref[rows, :])
        y_ssd = (sf_ref[rows, :].astype(F32) + sb_ref[rows, :].astype(F32)) * _silu(sz_ref[rows, :])
        y_ssd = y_ssd * lax.rsqrt(jnp.mean(y_ssd * y_ssd, axis=-1, keepdims=True) + RMS_EPS) * snw_ref[...]
        lhs_scr[rows, 0:w] = y_hy.astype(BF16)
        lhs_scr[rows, w:2 * w] = y_ssd.astype(BF16)
        og = gf_ref[rows, :].astype(F32) + gb_ref[rows, :].astype(F32)
        orr = rf_ref[rows, :].astype(F32) + rb_ref[rows, :].astype(F32)
        gz = _silu(gz_ref[rows, :])
        rg = _silu(rg_ref[rows, :])
        for h in range(GDN_HEADS):
            sl = slice(h * GDN_HEAD_DIM, (h + 1) * GDN_HEAD_DIM)
            o = og[:, sl]
            o = o * lax.rsqrt(jnp.mean(o * o, axis=-1, keepdims=True) + RMS_EPS) * gnw * gz[:, sl]
            lhs_scr[rows, 2 * w + h * GDN_HEAD_DIM:2 * w + (h + 1) * GDN_HEAD_DIM] = o.astype(BF16)
        for h in range(RET_HEADS):
            sl = slice(h * RET_HEAD_DIM, (h + 1) * RET_HEAD_DIM)
            o = orr[:, sl]
            mu = jnp.mean(o, axis=-1, keepdims=True)
            oc = o - mu
            o = oc * lax.rsqrt(jnp.mean(oc * oc, axis=-1, keepdims=True) + LN_EPS) * rg[:, sl]
            lhs_scr[rows, 3 * w + h * RET_HEAD_DIM:3 * w + (h + 1) * RET_HEAD_DIM] = o.astype(BF16)
    accs = [jnp.dot(lhs_scr[rows, :], w_ref[...], preferred_element_type=F32) for rows in chunks]
    for rows, acc in zip(chunks, accs):
        x = jnp.where(is_sample, xs_ref[rows, :], xp_ref[rows, :]) if two_group_x else xs_ref[rows, :]
        x1 = _layer_norm_rows(alpha * x + g1_ref[...] * acc, lng_ref[...], lnb_ref[...])
        x1_ref[rows, :] = x1
        u2_ref[rows, :] = (x1 * (1.0 + sc2_ref[...]) + sh2_ref[...]).astype(BF16)


def _outproj_call(hy_s, hy_p, ssd, gdn, ret, proj, ssd_nw, gdn_nw, w_out, x_in, mod4, layer, ln_g, ln_b, cfg):
    t, d, w = cfg["t"], cfg["d"], GROUP_WIDTH
    two_group_x = isinstance(x_in, tuple)
    tm = OUTPROJ_ROW_CHUNK if two_group_x else 2 * OUTPROJ_ROW_CHUNK
    ns = cfg["ts"] // tm
    row = lambda col: pl.BlockSpec((tm, w), lambda i: (i, col))
    vec = lambda n: pl.BlockSpec((None, 1, n), lambda i: (layer, 0, 0))
    mspec = lambda k: _mod_spec(layer, k, d, tm, cfg["ls"], cfg["n_dec"])
    if two_group_x:
        x_specs = _two_group_specs(tm, d, ns)
        x_args = list(x_in)
    else:
        x_specs = [pl.BlockSpec((tm, d), lambda i: (i, 0)), pl.BlockSpec((SUBLANES, d), lambda i: (0, 0))]
        x_args = [x_in, x_in]
    kern = functools.partial(_outproj_kernel, n_sample_tiles=ns, alpha=cfg["alpha"], two_group_x=two_group_x)
    return pl.pallas_call(
        kern,
        out_shape=[jax.ShapeDtypeStruct((t, d), F32), jax.ShapeDtypeStruct((t, d), BF16)],
        grid=(t // tm,),
        in_specs=_two_group_specs(tm, w, ns)
        + [row(0), row(0), row(COL_S_Z), row(0), row(0), row(COL_G_Z), row(0), row(0), row(COL_R_G)]
        + [vec(w), vec(GDN_HEAD_DIM),
           pl.BlockSpec((None, d, d), lambda i: (layer, 0, 0), pipeline_mode=pl.Buffered(1))]
        + x_specs
        + [mspec(2), mspec(3), mspec(4), vec(d), vec(d)],
        out_specs=[pl.BlockSpec((tm, d), lambda i: (i, 0)), pl.BlockSpec((tm, d), lambda i: (i, 0))],
        scratch_shapes=[pltpu.VMEM((tm, d), BF16)],
        compiler_params=_params("arbitrary"),
        name="out_proj_ln",
    )(
        hy_s, hy_p, ssd[0], ssd[1], proj, gdn[0], gdn[1], proj, ret[0], ret[1], proj,
        ssd_nw, gdn_nw, w_out, *x_args, mod4, mod4, mod4, ln_g, ln_b,
    )


def _ffn_up_kernel(u_ref, wg_ref, wv_ref, cwg_ref, cbg_ref, cwv_ref, cbv_ref, o_ref, w_scr, h_scr, *, n_sample_tiles, ls, lp):
    i = pl.program_id(1)
    tn = wg_ref.shape[1]
    seq_len = jnp.where(i < n_sample_tiles, ls, lp)

    @pl.when(i == 0)
    def _():
        w_scr[:, :tn] = wg_ref[...].astype(BF16)
        w_scr[:, tn:] = wv_ref[...].astype(BF16)

    cw = jnp.concatenate([cwg_ref[...], cwv_ref[...]], axis=-1)
    cb = jnp.concatenate([cbg_ref[...], cbv_ref[...]], axis=-1)

    def finish(rows, conv):
        o_ref[rows, :] = (_silu(conv[:, :tn]) * conv[:, tn:]).astype(BF16)

    _matmul_conv_chunks(u_ref, w_scr[...], cw, cb, h_scr, seq_len, min(ls, lp), finish)


def _ffn_up_call(u2, w_up, cw, cb, layer, cfg):
    t, d = u2.shape
    dff = w_up.shape[2] // 2
    tm, tn = cfg["ls"], 256
    nj = dff // tn
    kern = functools.partial(_ffn_up_kernel, n_sample_tiles=cfg["ts"] // tm, ls=cfg["ls"], lp=cfg["lp"])
    gcol = lambda j, i: (layer, 0, j)
    vcol = lambda j, i: (layer, 0, j + nj)
    return pl.pallas_call(
        kern,
        out_shape=jax.ShapeDtypeStruct((t, dff), BF16),
        grid=(nj, t // tm),
        in_specs=[
            pl.BlockSpec((tm, d), lambda j, i: (i, 0)),
            pl.BlockSpec((None, d, tn), gcol), pl.BlockSpec((None, d, tn), vcol),
            pl.BlockSpec((None, CONV_W, tn), gcol), pl.BlockSpec((None, 1, tn), gcol),
            pl.BlockSpec((None, CONV_W, tn), vcol), pl.BlockSpec((None, 1, tn), vcol),
        ],
        out_specs=pl.BlockSpec((tm, tn), lambda j, i: (i, j)),
        scratch_shapes=[pltpu.VMEM((d, 2 * tn), BF16), pltpu.VMEM((tm + 2 * SUBLANES, 2 * tn), F32)],
        compiler_params=_params("arbitrary", "arbitrary"),
        name="ffn_up_conv_glu",
    )(u2, w_up, w_up, cw, cb, cw, cb)


FFN_DOWN_ROW_CHUNK = 256


def _ffn_down_kernel(a_ref, w_ref, x1_ref, g2_ref, shn_ref, scn_ref, lng_ref, lnb_ref, oa_ref, ob_ref, acc_scr,
                     *, alpha, n_sample_tiles, last_layer, n_k):
    i = pl.program_id(0)
    k = pl.program_id(1)
    tm = acc_scr.shape[0]

    if n_k > 1:
        @pl.when(k == 0)
        def _():
            acc_scr[...] = jnp.dot(a_ref[...], w_ref[...], preferred_element_type=F32)

        @pl.when(jnp.logical_and(k > 0, k < n_k - 1))
        def _():
            acc_scr[...] += jnp.dot(a_ref[...], w_ref[...], preferred_element_type=F32)

    @pl.when(k == n_k - 1)
    def _():
        rc = min(FFN_DOWN_ROW_CHUNK, tm)
        chunks = [slice(r, r + rc) for r in range(0, tm, rc)]
        parts = [jnp.dot(a_ref[rows, :], w_ref[...], preferred_element_type=F32) for rows in chunks]
        x2s = []
        for rows, part in zip(chunks, parts):
            f = part + acc_scr[rows, :] if n_k > 1 else part
            x2 = _layer_norm_rows(alpha * x1_ref[rows, :] + g2_ref[...] * f, lng_ref[...], lnb_ref[...])
            x2s.append(x2)
            if not last_layer:
                oa_ref[rows, :] = x2
                ob_ref[rows, :] = (x2 * (1.0 + scn_ref[...]) + shn_ref[...]).astype(BF16)
        if last_layer:
            @pl.when(i < n_sample_tiles)
            def _():
                for rows, x2 in zip(chunks, x2s):
                    oa_ref[rows, :] = x2

            @pl.when(i >= n_sample_tiles)
            def _():
                for rows, x2 in zip(chunks, x2s):
                    ob_ref[rows, :] = x2


def _ffn_down_call(act, w_down, x1, mod4, layer, ln_g, ln_b, cfg):
    t, dff = act.shape
    d = cfg["d"]
    depth = w_down.shape[0]
    last_layer = layer == depth - 1
    next_layer = min(layer + 1, depth - 1)
    tm = 512
    tk = dff // 4 if (dff // 4) % LANES == 0 else dff
    ns = cfg["ts"] // tm
    vec = lambda: pl.BlockSpec((None, 1, d), lambda i, k: (layer, 0, 0))
    mspec = lambda lay, kk: _mod_spec(lay, kk, d, tm, cfg["ls"], cfg["n_dec"])
    if last_layer:
        out_shape = [jax.ShapeDtypeStruct((cfg["ts"], d), F32), jax.ShapeDtypeStruct((cfg["tp"], d), F32)]
        out_specs = [
            pl.BlockSpec((tm, d), lambda i, k: (jnp.minimum(i, ns - 1), 0)),
            pl.BlockSpec((tm, d), lambda i, k: (jnp.maximum(i - ns, 0), 0)),
        ]
    else:
        out_shape = [jax.ShapeDtypeStruct((t, d), F32), jax.ShapeDtypeStruct((t, d), BF16)]
        out_specs = [pl.BlockSpec((tm, d), lambda i, k: (i, 0)), pl.BlockSpec((tm, d), lambda i, k: (i, 0))]
    return pl.pallas_call(
        functools.partial(_ffn_down_kernel, alpha=cfg["alpha"], n_sample_tiles=ns, last_layer=last_layer, n_k=dff // tk),
        out_shape=out_shape,
        grid=(t // tm, dff // tk),
        in_specs=[
            pl.BlockSpec((tm, tk), lambda i, k: (i, k)),
            pl.BlockSpec((None, tk, d), lambda i, k: (layer, k, 0)),
            pl.BlockSpec((tm, d), lambda i, k: (i, 0)),
            mspec(layer, 5), mspec(next_layer, 0), mspec(next_layer, 1), vec(), vec(),
        ],
        out_specs=out_specs,
        scratch_shapes=[pltpu.VMEM((tm, d), F32)],
        compiler_params=_params("arbitrary", "arbitrary"),
        name="ffn_down_ln",
    )(act, w_down, x1, mod4, mod4, mod4, ln_g, ln_b)


def _reorder_w_in(w_in):
    w = GROUP_WIDTH
    o = N_CONV
    s_z = w_in[..., o:o + w]
    o += w
    s_dt = w_in[..., o:o + N_DIR * SSD_HEADS]
    o += N_DIR * SSD_HEADS
    g_z = w_in[..., o:o + w]
    o += w
    g_beta = w_in[..., o:o + N_DIR * GDN_HEADS]
    o += N_DIR * GDN_HEADS
    g_a = w_in[..., o:o + N_DIR * GDN_HEADS]
    o += N_DIR * GDN_HEADS
    rest = w_in[..., o:]
    main = jnp.concatenate([w_in[..., :N_CONV], s_z, g_z, rest], axis=-1).astype(BF16)
    small = jnp.concatenate([s_dt, g_beta, g_a], axis=-1)
    small = jnp.pad(small, ((0, 0), (0, 0), (0, LANES - small.shape[-1]))).astype(BF16)
    return main, small


def kernel(x_prompt, x_sample, state_ssd, state_gdn, state_ret, c, c_ctx, w_mod, b_mod, w_in, conv_w, conv_b,
           hy_w1, hy_b1, hy_w2, hy_b2, hy_w3, hy_freq, hy_bias, ssd_A_log, ssd_dt_bias, ssd_D, ssd_norm_w,
           gdn_A_log, gdn_dt_bias, gdn_norm_w, ret_decay, w_out, ln1_g, ln1_b, w_up, ffn_conv_w, ffn_conv_b,
           w_down, ln2_g, ln2_b):
    n_ctx, lp, d = x_prompt.shape
    n_dec, ls, _ = x_sample.shape
    depth = w_in.shape[0]
    ts, tp = n_dec * ls, n_ctx * lp
    assert ls & (ls - 1) == 0 and lp & (lp - 1) == 0 and tp % ls == 0 and ls % lp == 0
    assert d == 4 * GROUP_WIDTH and w_in.shape[2] == N_MAIN + 2 * N_DIR * (SSD_HEADS // 2 + GDN_HEADS)
    cfg = dict(d=d, ls=ls, lp=lp, n_dec=n_dec, n_ctx=n_ctx, ts=ts, tp=tp, t=ts + tp, alpha=(2 * depth) ** 0.25)

    xs = x_sample.reshape(ts, d)
    xp = x_prompt.reshape(tp, d)

    mod_rows = -(-(n_dec + 1) // SUBLANES) * SUBLANES
    cond = jnp.concatenate([c, c_ctx[None, :], jnp.zeros((mod_rows - n_dec - 1, d), F32)], axis=0)
    mod = _mod_call(cond, w_mod, b_mod)
    mod4 = mod.reshape(depth, mod_rows, 1, 6 * d)

    w_main, w_small = _reorder_w_in(w_in)
    w_out_b, w_down_b = w_out.astype(BF16), w_down.astype(BF16)
    stack_rows = lambda v: v[:, None, :]

    dft = {l: tuple(jnp.asarray(tab).astype(BF16) for tab in _dft_tables(l)) for l in (ls, lp)}
    rope_tabs = _rope_tables(cfg)
    rope_blocks = _rope_block_table(cfg)
    ssd_rows, gdn_rows, ret_rows = CHUNKS_PER_STEP * SSD_CHUNK, GDN_CHUNKS_PER_STEP * GDN_CHUNK, CHUNKS_PER_STEP * RET_CHUNK
    tabs = {q: _chunk_tables(cfg, q) for q in sorted({ssd_rows, gdn_rows, ret_rows})}

    def init_state(st, l):
        zeros = jnp.zeros((n_ctx,) + st.shape[2:], F32)
        return jnp.concatenate([st[:, l], zeros], axis=0)

    u = _modulate_call(xs, xp, mod4, 0, cfg)
    x_res = (xs, xp)
    new_ssd, new_gdn, new_ret = [], [], []
    for l in range(depth):
        proj, small = _inproj_call(u, w_main, w_small, conv_w, stack_rows(conv_b), l, cfg)

        hy = {}
        for name, seq_len, n_seq, row0 in (("s", ls, n_dec, 0), ("p", lp, n_ctx, ts)):
            a_tab, b_tab = dft[seq_len]
            prs = _hy_filter_call(seq_len, a_tab, hy_w1[l], hy_b1[l], hy_w2[l], hy_b2[l], hy_w3[l], hy_freq[l])
            z1 = _hy_conv_call(proj, COL_HY_V, row0, proj, COL_HY_X1, row0, hy_bias[l], a_tab, b_tab, prs, 0, seq_len, n_seq)
            hy[name] = _hy_conv_call(z1, 0, 0, proj, COL_HY_X2, row0, hy_bias[l], a_tab, b_tab, prs, 1, seq_len, n_seq)

        ssd = _ssd_call(proj, small, tabs[ssd_rows], init_state(state_ssd, l), ssd_dt_bias[l], ssd_A_log[l], ssd_D[l], cfg)
        gdn = _gdn_call(proj, small, tabs[gdn_rows], init_state(state_gdn, l), gdn_A_log[l], gdn_dt_bias[l], cfg)
        ret = _ret_call(proj, tabs[ret_rows], rope_tabs, rope_blocks, init_state(state_ret, l), ret_decay[l], cfg)
        new_ssd.append(ssd[2][n_dec:])
        new_gdn.append(gdn[2][n_dec:])
        new_ret.append(ret[2][n_dec:])

        x1, u2 = _outproj_call(hy["s"], hy["p"], ssd, gdn, ret, proj, stack_rows(ssd_norm_w), stack_rows(gdn_norm_w),
                               w_out_b, x_res, mod4, l, stack_rows(ln1_g), stack_rows(ln1_b), cfg)
        act = _ffn_up_call(u2, w_up, ffn_conv_w, stack_rows(ffn_conv_b), l, cfg)
        x_res, u = _ffn_down_call(act, w_down_b, x1, mod4, l, stack_rows(ln2_g), stack_rows(ln2_b), cfg)

    y_sample = x_res.reshape(n_dec, ls, d)
    y_prompt = u.reshape(n_ctx, lp, d)
    return (y_prompt, y_sample, jnp.stack(new_ssd, 1), jnp.stack(new_gdn, 1), jnp.stack(new_ret, 1))
```

```python
import functools
import math

import numpy as np
import jax
import jax.numpy as jnp
from jax import lax
from jax.experimental import pallas as pl
from jax.experimental.pallas import tpu as pltpu

F32 = jnp.float32
BF16 = jnp.bfloat16
HIGHEST = lax.Precision.HIGHEST

N_DIR = 2
CONV_W = 3
GROUP_WIDTH = 512
HY_ORDER = 2
HY_BANDS = 8
HY_EMB = 1 + 2 * HY_BANDS
HY_FFN = 64
HY_DECAY_TARGET = 1e-2
HY_FAST_PCT = 0.3
HY_SLOW_PCT = 1.5
SSD_HEADS = 8
SSD_HEAD_DIM = 64
SSD_GROUPS = 2
SSD_HPG = SSD_HEADS // SSD_GROUPS
SSD_STATE = 128
SSD_CHUNK = 128
GDN_HEADS = 4
GDN_HEAD_DIM = 128
GDN_CHUNK = 64
RET_HEADS = 4
RET_HEAD_DIM = 128
RET_CHUNK = 128
GRID_W = 64
ROPE_BASE = 10000.0
LN_EPS = 1e-5
RMS_EPS = 1e-6
N_CONV = 8 * GROUP_WIDTH

LANES = 128
SUBLANES = 8
VMEM_LIMIT_BYTES = 60 * 1024 * 1024

COL_HY_V, COL_HY_X1, COL_HY_X2, COL_S_X, COL_S_BC, COL_G_Q, COL_G_K, COL_G_V = range(8)
COL_S_Z, COL_G_Z, COL_R_Q, COL_R_K, COL_R_V, COL_R_G = range(8, 14)
N_MAIN = 14 * GROUP_WIDTH
SM_SDT = 0
SM_GBETA = 16
SM_GA = 24


def _silu(x):
    return x * jax.nn.sigmoid(x)


def _softplus(x):
    return jnp.maximum(x, 0.0) + jnp.log1p(jnp.exp(-jnp.abs(x)))


def _dot(a, b):
    return jnp.dot(a.astype(BF16), b.astype(BF16), preferred_element_type=F32)


def _dot_nt(a, b):
    return lax.dot_general(a.astype(BF16), b.astype(BF16), (((1,), (1,)), ((), ())), preferred_element_type=F32)


def _dot_tn(a, b):
    return lax.dot_general(a.astype(BF16), b.astype(BF16), (((0,), (0,)), ((), ())), preferred_element_type=F32)


def _dot_hi(a, b):
    return jnp.dot(a, b, preferred_element_type=F32, precision=HIGHEST)


def _params(*semantics):
    return pltpu.CompilerParams(dimension_semantics=semantics, vmem_limit_bytes=VMEM_LIMIT_BYTES)


def _tri(n, lower):
    r = lax.broadcasted_iota(jnp.int32, (n, n), 0)
    c = lax.broadcasted_iota(jnp.int32, (n, n), 1)
    return jnp.where((r >= c) if lower else (r <= c), 1.0, 0.0).astype(F32)


def _mod_kernel(c_ref, w_ref, b_ref, o_ref):
    a = _silu(c_ref[...])
    o_ref[...] = _dot(a, w_ref[...]) + b_ref[...]


def _mod_call(cond, w_mod, b_mod):
    depth, d, n = w_mod.shape
    rows = cond.shape[0]
    tn = 1024
    return pl.pallas_call(
        _mod_kernel,
        out_shape=jax.ShapeDtypeStruct((depth, rows, n), F32),
        grid=(depth, n // tn),
        in_specs=[
            pl.BlockSpec((rows, d), lambda l, j: (0, 0)),
            pl.BlockSpec((None, d, tn), lambda l, j: (l, 0, j)),
            pl.BlockSpec((None, 1, tn), lambda l, j: (l, 0, j)),
        ],
        out_specs=pl.BlockSpec((None, rows, tn), lambda l, j: (l, 0, j)),
        compiler_params=_params("arbitrary", "arbitrary"),
        name="adaln_mod",
    )(cond, w_mod, b_mod.reshape(depth, 1, n))


def _mod_spec(layer, k, d, rows_per_tile, l_sample, n_dec):
    tiles_per_seq = l_sample // rows_per_tile

    def index_map(i, *_):
        return (layer, jnp.minimum(i // tiles_per_seq, n_dec), 0, k)

    return pl.BlockSpec((None, None, 1, d), index_map)


def _modulate_kernel(xs_ref, xp_ref, sh_ref, sc_ref, o_ref, *, n_sample_tiles):
    i = pl.program_id(0)
    x = jnp.where(i < n_sample_tiles, xs_ref[...], xp_ref[...])
    o_ref[...] = (x * (1.0 + sc_ref[...]) + sh_ref[...]).astype(BF16)


def _two_group_specs(tm, d, n_sample_tiles, **spec_kwargs):
    return [
        pl.BlockSpec((tm, d), lambda i, *_: (jnp.minimum(i, n_sample_tiles - 1), 0), **spec_kwargs),
        pl.BlockSpec((tm, d), lambda i, *_: (jnp.maximum(i - n_sample_tiles, 0), 0), **spec_kwargs),
    ]


def _modulate_call(xs, xp, mod4, layer, cfg):
    tm, d = 256, cfg["d"]
    t = cfg["t"]
    ns = cfg["ts"] // tm
    return pl.pallas_call(
        functools.partial(_modulate_kernel, n_sample_tiles=ns),
        out_shape=jax.ShapeDtypeStruct((t, d), BF16),
        grid=(t // tm,),
        in_specs=_two_group_specs(tm, d, ns)
        + [_mod_spec(layer, 0, d, tm, cfg["ls"], cfg["n_dec"]), _mod_spec(layer, 1, d, tm, cfg["ls"], cfg["n_dec"])],
        out_specs=pl.BlockSpec((tm, d), lambda i: (i, 0)),
        compiler_params=_params("arbitrary"),
        name="modulate_in",
    )(xs, xp, mod4, mod4)


CONV_ROW_CHUNK = 256


def _matmul_conv_chunks(u_ref, w, cw, cb, h_scr, seq_len, min_seq_len, finish):
    rows = u_ref.shape[0]
    rc = min(CONV_ROW_CHUNK, rows)
    assert rows % rc == 0 and rc % min_seq_len == 0
    n_chunks = rows // rc
    n = w.shape[1]
    halo = SUBLANES
    first = lax.broadcasted_iota(jnp.int32, (SUBLANES, 1), 0) == 0
    last = lax.broadcasted_iota(jnp.int32, (SUBLANES, 1), 0) == SUBLANES - 1
    tile = lambda v, t: v[t * SUBLANES:(t + 1) * SUBLANES]
    h_scr[0:halo, :] = jnp.zeros((halo, n), F32)
    h_scr[halo + rows:2 * halo + rows, :] = jnp.zeros((halo, n), F32)
    for r in range(n_chunks + 1):
        if r < n_chunks:
            h_scr[halo + r * rc:halo + (r + 1) * rc, :] = jnp.dot(u_ref[r * rc:(r + 1) * rc, :], w, preferred_element_type=F32)
        if r == 0:
            continue
        row0 = (r - 1) * rc
        h = h_scr[halo + row0:halo + row0 + rc, :]
        hp = h_scr[halo + row0 - 1:halo + row0 - 1 + rc, :]
        hn = h_scr[halo + row0 + 1:halo + row0 + 1 + rc, :]
        fix_p, fix_n = {}, {}
        for b in range(0, rc, min_seq_len):
            seq_start = ((row0 + b) & (seq_len - 1)) == 0
            seq_end = ((row0 + b + min_seq_len) & (seq_len - 1)) == 0
            tp, tn_ = b // SUBLANES, (b + min_seq_len) // SUBLANES - 1
            fix_p[tp] = jnp.where(jnp.logical_and(first, seq_start), 0.0, tile(hp, tp))
            fix_n[tn_] = jnp.where(jnp.logical_and(last, seq_end), 0.0, tile(hn, tn_))
        n_tiles = rc // SUBLANES
        hp = jnp.concatenate([fix_p.get(t, tile(hp, t)) for t in range(n_tiles)], axis=0)
        hn = jnp.concatenate([fix_n.get(t, tile(hn, t)) for t in range(n_tiles)], axis=0)
        finish(slice(row0, row0 + rc), hp * cw[0:1] + h * cw[1:2] + hn * cw[2:3] + cb)


def _inproj_kernel(u_ref, w_ref, ws_ref, cw_ref, cb_ref, o_ref, os_ref, h_scr, *, n_conv_tiles, n_sample_tiles, ls, lp):
    i = pl.program_id(0)
    j = pl.program_id(1)
    seq_len = jnp.where(i < n_sample_tiles, ls, lp)

    @pl.when(j < n_conv_tiles)
    def _():
        def finish(rows, conv):
            o_ref[rows, :] = conv

        _matmul_conv_chunks(u_ref, w_ref[...], cw_ref[...], cb_ref[...], h_scr, seq_len, min(ls, lp), finish)

    @pl.when(j >= n_conv_tiles)
    def _():
        o_ref[...] = jnp.dot(u_ref[...], w_ref[...], preferred_element_type=F32)

    @pl.when(j == 0)
    def _():
        os_ref[...] = jnp.dot(u_ref[...], ws_ref[...], preferred_element_type=F32)


def _inproj_call(u, w_main, w_small, cw, cb, layer, cfg):
    t, d = u.shape
    tm, tn = cfg["ls"], 512
    n_conv_tiles = N_CONV // tn
    kern = functools.partial(
        _inproj_kernel, n_conv_tiles=n_conv_tiles, n_sample_tiles=cfg["ts"] // tm, ls=cfg["ls"], lp=cfg["lp"]
    )
    conv_col = lambda i, j: (layer, 0, jnp.minimum(j, n_conv_tiles - 1))
    return pl.pallas_call(
        kern,
        out_shape=[jax.ShapeDtypeStruct((t, N_MAIN), F32), jax.ShapeDtypeStruct((t, LANES), F32)],
        grid=(t // tm, N_MAIN // tn),
        in_specs=[
            pl.BlockSpec((tm, d), lambda i, j: (i, 0)),
            pl.BlockSpec((None, d, tn), lambda i, j: (layer, 0, j)),
            pl.BlockSpec((None, d, LANES), lambda i, j: (layer, 0, 0)),
            pl.BlockSpec((None, CONV_W, tn), conv_col),
            pl.BlockSpec((None, 1, tn), conv_col),
        ],
        out_specs=[pl.BlockSpec((tm, tn), lambda i, j: (i, j)), pl.BlockSpec((tm, LANES), lambda i, j: (i, 0))],
        scratch_shapes=[pltpu.VMEM((tm + 2 * SUBLANES, tn), F32)],
        compiler_params=_params("arbitrary", "arbitrary"),
        name="in_proj_conv",
    )(u, w_main, w_small, cw, cb)


@functools.lru_cache(maxsize=None)
def _dft_tables(l):
    n = 2 * l
    k = np.arange(l, dtype=np.int64)
    ang = (2.0 * math.pi / n) * ((k[:, None] * k[None, :]) % n)
    cos, sin = np.cos(ang), np.sin(ang)
    alt = np.where(k % 2 == 0, 1.0, -1.0)
    a_im = np.where(k[:, None] == 0, alt[None, :], -sin)
    a = np.concatenate([cos, a_im], axis=0)
    ck = np.where(k == 0, 1.0, 2.0) / n
    b_re = cos * ck[None, :]
    b_im = np.where(k[None, :] == 0, alt[:, None] / n, -sin * (2.0 / n))
    b = np.concatenate([b_re, b_im], axis=1)
    return a.astype(np.float32), b.astype(np.float32)


@functools.lru_cache(maxsize=None)
def _hy_feats(l):
    pos = np.arange(l, dtype=np.float32)
    t = pos / np.float32(l - 1)
    bands = np.linspace(1e-4, HY_BANDS - 1, HY_BANDS, dtype=np.float32)
    ang = np.float32(2.0 * math.pi / l) * pos[:, None] * bands
    feats = np.concatenate([t[:, None], np.cos(ang), -np.sin(ang)], -1).astype(np.float32)
    feats = np.pad(feats, ((0, 0), (0, LANES - HY_EMB)))
    return feats, t[:, None]


@functools.lru_cache(maxsize=None)
def _hy_deltas(c):
    lo, hi = math.log(HY_DECAY_TARGET) / HY_SLOW_PCT, math.log(HY_DECAY_TARGET) / HY_FAST_PCT
    return np.abs(np.linspace(lo, hi, c, dtype=np.float32))[None, :]


def _hy_filter_kernel(
    feats_ref, t_ref, dl_ref, w1_ref, b1_ref, w2_ref, b2_ref, w3_ref, fq_ref, are_ref, aim_ref,
    p_ref, r_ref, s_ref, sum_scr, dif_scr, nyq_scr,
):
    f = pl.program_id(0)
    l = feats_ref.shape[0]
    c = dl_ref.shape[1]
    kb = are_ref.shape[0]

    @pl.when(f == 0)
    def _():
        fq = fq_ref[...]
        h = jnp.sin(fq[0:1] * (_dot_hi(feats_ref[...], w1_ref[...]) + b1_ref[...]))
        h = jnp.sin(fq[1:2] * (_dot_hi(h, w2_ref[...]) + b2_ref[...]))
        decay = jnp.exp(-t_ref[...] * dl_ref[...])
        row = lax.broadcasted_iota(jnp.int32, (l, 1), 0)
        sign = jnp.where((row & 1) == 0, 1.0, -1.0)
        for o in range(HY_ORDER):
            fwd = _dot_hi(h, w3_ref[:, (2 * o) * c:(2 * o + 1) * c]) * decay
            bwd = _dot_hi(h, w3_ref[:, (2 * o + 1) * c:(2 * o + 2) * c]) * decay
            bwd = jnp.where(row == 0, 0.0, bwd)
            norm = jnp.sum(jnp.abs(fwd), axis=0, keepdims=True) + jnp.sum(jnp.abs(bwd), axis=0, keepdims=True)
            fwd = fwd / norm
            bwd = bwd / norm
            ssum = fwd + bwd
            sum_scr[o] = ssum.astype(BF16)
            dif_scr[o] = (fwd - bwd).astype(BF16)
            nyq_scr[o] = jnp.broadcast_to(jnp.sum(ssum * sign, axis=0, keepdims=True), (SUBLANES, c))

    grow = f * kb + lax.broadcasted_iota(jnp.int32, (kb, 1), 0)
    for o in range(HY_ORDER):
        kre = jnp.dot(are_ref[...], sum_scr[o], preferred_element_type=F32)
        kim = jnp.dot(aim_ref[...], dif_scr[o], preferred_element_type=F32)
        p_ref[o] = kre
        r_ref[o] = jnp.where(grow == 0, 0.0, kim)
        s_ref[o] = jnp.where(grow == 0, nyq_scr[o][0:1], kre)


def _hy_filter_call(l, a_tab, w1, b1, w2, b2, w3, freq):
    c = GROUP_WIDTH
    kb = min(l, 256)
    nf = l // kb
    feats, t = _hy_feats(l)
    w1p = jnp.pad(w1, ((0, LANES - HY_EMB), (0, 0)))
    full = lambda shape: pl.BlockSpec(shape, lambda f: tuple(0 for _ in shape))
    out = jax.ShapeDtypeStruct((HY_ORDER, l, c), F32)
    out_spec = pl.BlockSpec((HY_ORDER, kb, c), lambda f: (0, f, 0))
    return pl.pallas_call(
        _hy_filter_kernel,
        out_shape=[out, out, out],
        grid=(nf,),
        in_specs=[
            full((l, LANES)), full((l, 1)), full((1, c)), full((LANES, HY_FFN)), full((1, HY_FFN)),
            full((HY_FFN, HY_FFN)), full((1, HY_FFN)), full((HY_FFN, 2 * HY_ORDER * c)), full((2, HY_FFN)),
            pl.BlockSpec((kb, l), lambda f: (f, 0)),
            pl.BlockSpec((kb, l), lambda f: (f + nf, 0)),
        ],
        out_specs=[out_spec, out_spec, out_spec],
        scratch_shapes=[
            pltpu.VMEM((HY_ORDER, l, c), BF16), pltpu.VMEM((HY_ORDER, l, c), BF16),
            pltpu.VMEM((HY_ORDER, SUBLANES, c), F32),
        ],
        compiler_params=_params("arbitrary"),
        name="hyena_filter",
    )(feats, t, _hy_deltas(c), w1p, b1[None, :], w2, b2[None, :], w3, freq, a_tab, a_tab)


HY_ROWS_PER_STEP = 1024


def _hy_conv_kernel(z_ref, g_ref, bias_ref, are_ref, aim_ref, bre_ref, bim_ref, p_ref, r_ref, s_ref, o_ref, zb_scr,
                    *, seq_len):
    f = pl.program_id(2)

    @pl.when(f == 0)
    def _():
        zb_scr[...] = z_ref[...].astype(BF16)
        o_ref[...] = jnp.zeros_like(o_ref)

    p, r, s = p_ref[...], r_ref[...], s_ref[...]
    for b in range(z_ref.shape[0] // seq_len):
        rows = slice(b * seq_len, (b + 1) * seq_len)
        zb = zb_scr[rows, :]
        re = jnp.dot(are_ref[...], zb, preferred_element_type=F32)
        im = jnp.dot(aim_ref[...], zb, preferred_element_type=F32)
        re2 = re * p - im * r
        im2 = re * r + im * s
        o_ref[rows, :] += _dot(bre_ref[...], re2) + _dot(bim_ref[...], im2)

    @pl.when(f == pl.num_programs(2) - 1)
    def _():
        o_ref[...] = g_ref[...] * (o_ref[...] + z_ref[...] * bias_ref[...])


def _hy_conv_call(z, z_col, z_row0, g, g_col, g_row0, bias, a_tab, b_tab, prs, order, l, n_seq):
    c = GROUP_WIDTH
    ct = c
    ncb = c // ct
    kb = min(l, 512)
    nf = l // kb
    p, r, s = prs
    nb = max(n for n in (1, 2, 4, 8) if n_seq % n == 0 and n * l <= max(l, HY_ROWS_PER_STEP))
    rows = nb * l
    assert z_row0 % rows == 0 and g_row0 % rows == 0
    zspec = pl.BlockSpec((rows, ct), lambda b, cb, f: (z_row0 // rows + b, z_col * ncb + cb))
    gspec = pl.BlockSpec((rows, ct), lambda b, cb, f: (g_row0 // rows + b, g_col * ncb + cb))
    kspec = pl.BlockSpec((None, kb, ct), lambda b, cb, f: (order, f, cb))
    return pl.pallas_call(
        functools.partial(_hy_conv_kernel, seq_len=l),
        out_shape=jax.ShapeDtypeStruct((n_seq * l, c), F32),
        grid=(n_seq // nb, ncb, nf),
        in_specs=[
            zspec, gspec,
            pl.BlockSpec((None, 1, ct), lambda b, cb, f: (order, 0, cb)),
            pl.BlockSpec((kb, l), lambda b, cb, f: (f, 0)),
            pl.BlockSpec((kb, l), lambda b, cb, f: (f + nf, 0)),
            pl.BlockSpec((l, kb), lambda b, cb, f: (0, f)),
            pl.BlockSpec((l, kb), lambda b, cb, f: (0, f + nf)),
            kspec, kspec, kspec,
        ],
        out_specs=pl.BlockSpec((rows, ct), lambda b, cb, f: (b, cb)),
        scratch_shapes=[pltpu.VMEM((rows, ct), BF16)],
        compiler_params=_params("arbitrary", "arbitrary", "arbitrary"),
        name="hyena_conv",
    )(z, g, bias.reshape(HY_ORDER, 1, c), a_tab, a_tab, b_tab, b_tab, p, r, s)


CHUNKS_PER_STEP = 2
GDN_CHUNKS_PER_STEP = 4


def _scan_order(d, n=CHUNKS_PER_STEP):
    return list(range(n)) if d == 0 else list(range(n - 1, -1, -1))


def _chunk_tables(cfg, q):
    fwd, bwd, first, last, seq = [], [], [], [], []
    sid = 0
    for n_seq, l, row0 in ((cfg["n_dec"], cfg["ls"], 0), (cfg["n_ctx"], cfg["lp"], cfg["ts"])):
        nc = l // q
        for b in range(n_seq):
            base = (row0 + b * l) // q
            for c in range(nc):
                fwd.append(base + c)
                bwd.append(base + nc - 1 - c)
                first.append(int(c == 0))
                last.append(int(c == nc - 1))
                seq.append(sid)
            sid += 1
    return tuple(jnp.asarray(np.asarray(a, np.int32)) for a in (fwd, bwd, first, last, seq))


def _scan_masks(q, d):
    r = lax.broadcasted_iota(jnp.int32, (q, q), 0)
    c = lax.broadcasted_iota(jnp.int32, (q, q), 1)
    return ((r >= c), (r > c)) if d == 0 else ((r <= c), (r < c))


def _cumsum_pair(a_col, a_row, q, d):
    lo, up = _tri(q, True), _tri(q, False)
    if d == 0:
        return _dot_hi(lo, a_col), _dot_hi(a_row, up)
    return _dot_hi(up, a_col), _dot_hi(a_row, lo)


def _ssd_kernel(
    fwd_t, bwd_t, first_t, last_t, seq_t,
    xf_ref, bcf_ref, smf_ref, xb_ref, bcb_ref, smb_ref, pr_ref, pc_ref, dsk_ref, h0_ref,
    yf_ref, yb_ref, hout_ref, h_scr,
):
    s = pl.program_id(0)
    q, p, n = SSD_CHUNK, SSD_HEAD_DIM, SSD_STATE

    @pl.when(first_t[s] == 1)
    def _():
        h_scr[...] = h0_ref[...]

    bias_r, alog_r = pr_ref[0:1], pr_ref[1:2]
    bias_c, alog_c = pc_ref[:, 0:1], pc_ref[:, 1:2]
    assert q == LANES == n and 2 * p == LANES
    even_half = lax.broadcasted_iota(jnp.int32, (q, LANES), 1) < p
    probs = []
    for d in range(N_DIR):
        x_ref, bc_ref, sm_ref = (xf_ref, bcf_ref, smf_ref) if d == 0 else (xb_ref, bcb_ref, smb_ref)
        incl, _ = _scan_masks(q, d)
        end = q - 1 if d == 0 else 0
        for k in _scan_order(d):
            rows = slice(k * q, (k + 1) * q)
            x = _silu(x_ref[rows, :])
            x_t = x.T
            bc = _silu(bc_ref[rows, :])
            sm = sm_ref[rows, :]
            a_c = _softplus(sm + bias_r) * (-jnp.exp(alog_r))
            dt_r = _softplus(sm.T + bias_c)
            acs_c, acs_r = _cumsum_pair(a_c, (dt_r * (-jnp.exp(alog_c)))[SM_SDT:SM_SDT + N_DIR * SSD_HEADS], q, d)
            for g in range(SSD_GROUPS):
                bm = bc[:, g * n:(g + 1) * n]
                cm = bc[:, (SSD_GROUPS + g) * n:(SSD_GROUPS + g + 1) * n]
                cb = _dot_nt(cm, bm)
                for e in range(SSD_HPG):
                    hd = g * SSD_HPG + e
                    idx = d * SSD_HEADS + hd
                    pair = hd // 2
                    row = acs_r[idx:idx + 1, :]
                    dt_row = dt_r[SM_SDT + idx:SM_SDT + idx + 1, :]
                    tot = row[:, end:end + 1]
                    probs.append(dict(
                        d=d, k=k, hd=hd, bm=bm, cm=cm, cb=cb, incl=incl, row=row, dt_row=dt_row, tot=tot,
                        colb=jnp.broadcast_to(acs_c[:, idx:idx + 1], (q, q)),
                        x_pair=x[:, pair * LANES:(pair + 1) * LANES], x_t=x_t[hd * p:(hd + 1) * p, :],
                        skip=dsk_ref[d:d + 1, pair * LANES:(pair + 1) * LANES],
                    ))
    lmat = [jnp.exp(jnp.where(pb["incl"], pb["colb"] - pb["row"], -1e30)) * pb["dt_row"] for pb in probs]
    y_diag = [_dot(pb["cb"] * lm, pb["x_pair"]) for pb, lm in zip(probs, lmat)]
    states = [_dot(pb["x_t"] * (pb["dt_row"] * jnp.exp(pb["tot"] - pb["row"])), pb["bm"]) for pb in probs]
    c_dec = [pb["cm"] * jnp.exp(pb["colb"]) for pb in probs]
    state = {(d, hd): h_scr[d, hd] for d in range(N_DIR) for hd in range(SSD_HEADS)}
    ys = {}
    for step in range(CHUNKS_PER_STEP):
        group = [(i, pb) for i, pb in enumerate(probs) if _scan_order(pb["d"])[step] == pb["k"]]
        pair_state = {
            (d, pr_): jnp.concatenate([state[d, 2 * pr_], state[d, 2 * pr_ + 1]], axis=0)
            for d in range(N_DIR) for pr_ in range(SSD_HEADS // 2)
        }
        y_off = [_dot_nt(c_dec[i], pair_state[pb["d"], pb["hd"] // 2]) for i, pb in group]
        for (i, pb), yo in zip(group, y_off):
            ys[pb["d"], pb["k"], pb["hd"]] = y_diag[i] + yo
            state[pb["d"], pb["hd"]] = state[pb["d"], pb["hd"]] * jnp.exp(pb["tot"]) + states[i]
    for (d, hd), h_new in state.items():
        h_scr[d, hd] = h_new
    by_key = {(pb["d"], pb["k"], pb["hd"]): pb for pb in probs}
    for d, y_ref in ((0, yf_ref), (1, yb_ref)):
        for k in range(CHUNKS_PER_STEP):
            for pr_ in range(SSD_HEADS // 2):
                pb = by_key[d, k, 2 * pr_]
                y_pair = jnp.where(even_half, ys[d, k, 2 * pr_], ys[d, k, 2 * pr_ + 1]) + pb["skip"] * pb["x_pair"]
                y_ref[k * q:(k + 1) * q, pr_ * LANES:(pr_ + 1) * LANES] = y_pair.astype(BF16)

    @pl.when(last_t[s] == 1)
    def _():
        hout_ref[...] = h_scr[...]


def _ssd_call(proj, small, tables, h0, dt_bias, a_log, d_skip, cfg):
    q, w = CHUNKS_PER_STEP * SSD_CHUNK, GROUP_WIDTH
    t = proj.shape[0]
    n_steps = tables[0].shape[0]
    nlane = N_DIR * SSD_HEADS
    pr = jnp.zeros((SUBLANES, LANES), F32)
    pr = pr.at[0, SM_SDT:SM_SDT + nlane].set(dt_bias.reshape(-1)).at[1, SM_SDT:SM_SDT + nlane].set(a_log.reshape(-1))
    pc = pr.T
    dsk = jnp.repeat(d_skip, SSD_HEAD_DIM, axis=-1)
    st_shape = (N_DIR, SSD_HEADS, SSD_HEAD_DIM, SSD_STATE)
    fmap = lambda col: (lambda s, fw, bw, fi, la, sq: (fw[s], col))
    bmap = lambda col: (lambda s, fw, bw, fi, la, sq: (bw[s], col))
    const = lambda shape: pl.BlockSpec(shape, lambda s, *_: tuple(0 for _ in shape))
    st_spec = pl.BlockSpec((None,) + st_shape, lambda s, fw, bw, fi, la, sq: (sq[s], 0, 0, 0, 0))
    grid_spec = pltpu.PrefetchScalarGridSpec(
        num_scalar_prefetch=5,
        grid=(n_steps,),
        in_specs=[
            pl.BlockSpec((q, w), fmap(COL_S_X)), pl.BlockSpec((q, w), fmap(COL_S_BC)), pl.BlockSpec((q, LANES), fmap(0)),
            pl.BlockSpec((q, w), bmap(COL_S_X)), pl.BlockSpec((q, w), bmap(COL_S_BC)), pl.BlockSpec((q, LANES), bmap(0)),
            const((SUBLANES, LANES)), const((LANES, SUBLANES)), const((N_DIR, w)), st_spec,
        ],
        out_specs=[pl.BlockSpec((q, w), fmap(0)), pl.BlockSpec((q, w), bmap(0)), st_spec],
        scratch_shapes=[pltpu.VMEM(st_shape, F32)],
    )
    n_seq = cfg["n_dec"] + cfg["n_ctx"]
    return pl.pallas_call(
        _ssd_kernel,
        out_shape=[
            jax.ShapeDtypeStruct((t, w), BF16), jax.ShapeDtypeStruct((t, w), BF16),
            jax.ShapeDtypeStruct((n_seq,) + st_shape, F32),
        ],
        grid_spec=grid_spec,
        compiler_params=_params("arbitrary"),
        name="ssd_scan",
    )(*tables, proj, proj, small, proj, proj, small, pr, pc, dsk, h0)


TRI_BLOCK = 2 * SUBLANES


def _unit_tri_inverses(mats, lowers):
    n = mats[0].shape[0]
    blk = TRI_BLOCK
    sub = blk // SUBLANES
    lane = lax.broadcasted_iota(jnp.int32, (blk, n), 1)
    row = lax.broadcasted_iota(jnp.int32, (blk, n), 0)
    group = (lax.broadcasted_iota(jnp.int32, (SUBLANES, LANES), 1) // blk) * blk
    tiles = lambda v: [v[t * SUBLANES:(t + 1) * SUBLANES, :] for t in range(sub)]
    eye = jnp.where(lane % blk == row, 1.0, 0.0)
    d_t, inv_t = [], []
    for a in mats:
        packed = jnp.zeros((blk, n), F32)
        for bi in range(n // blk):
            packed = jnp.where(lane // blk == bi, a[bi * blk:(bi + 1) * blk, :], packed)
        if n < LANES:
            packed = jnp.concatenate([packed, jnp.zeros((blk, LANES - n), F32)], axis=1)
        d_t.append(tiles(packed))
        inv_t.append(tiles(eye))
    for step in range(blk):
        for i, lower in enumerate(lowers):
            jj = step if lower else blk - 1 - step
            st, jr = divmod(jj, SUBLANES)
            done_row = inv_t[i][st][jr:jr + 1, :]
            for s in (range(st, sub) if lower else range(st + 1)):
                col = jnp.take_along_axis(d_t[i][s], group + jj, axis=1)[:, :n]
                inv_t[i][s] = inv_t[i][s] - col * done_row
    ts = []
    for i in range(len(mats)):
        inv = jnp.concatenate(inv_t[i], axis=0)
        ts.append(jnp.concatenate([jnp.where(lane // blk == bi, inv, 0.0) for bi in range(n // blk)], axis=0))
    r = lax.broadcasted_iota(jnp.int32, (n, n), 0)
    c = lax.broadcasted_iota(jnp.int32, (n, n), 1)
    size = blk
    while size < n:
        level = (r // (2 * size) == c // (2 * size)) & (r // size != c // size)
        half = [_dot(t, jnp.where(level, a, 0.0)) for t, a in zip(ts, mats)]
        ts = [t - _dot(h, t) for t, h in zip(ts, half)]
        size *= 2
    return ts


def _l2norm_heads(x, heads, dim, scale):
    outs = []
    for h in range(heads):
        xh = x[:, h * dim:(h + 1) * dim]
        outs.append(xh * (lax.rsqrt(jnp.sum(xh * xh, axis=-1, keepdims=True) + 1e-6) * scale))
    return outs


def _gdn_kernel(
    fwd_t, bwd_t, first_t, last_t, seq_t,
    qf_ref, kf_ref, vf_ref, smf_ref, qb_ref, kb_ref, vb_ref, smb_ref, pr_ref, pc_ref, s0_ref,
    of_ref, ob_ref, sout_ref, s_scr,
):
    s = pl.program_id(0)
    q, dim = GDN_CHUNK, GDN_HEAD_DIM

    @pl.when(first_t[s] == 1)
    def _():
        s_scr[...] = s0_ref[...]

    bias_r, alog_r = pr_ref[0:1], pr_ref[1:2]
    bias_c, alog_c = pc_ref[:, 0:1], pc_ref[:, 1:2]
    n_sub = GDN_CHUNKS_PER_STEP
    probs = []
    for d in range(N_DIR):
        q_ref, k_ref, v_ref, sm_ref = (qf_ref, kf_ref, vf_ref, smf_ref) if d == 0 else (qb_ref, kb_ref, vb_ref, smb_ref)
        incl, strict = _scan_masks(q, d)
        end = q - 1 if d == 0 else 0
        for c in _scan_order(d, n_sub):
            rows = slice(c * q, (c + 1) * q)
            qs = _l2norm_heads(_silu(q_ref[rows, :]), GDN_HEADS, dim, dim ** -0.5)
            ks = _l2norm_heads(_silu(k_ref[rows, :]), GDN_HEADS, dim, 1.0)
            v = _silu(v_ref[rows, :])
            sm = sm_ref[rows, :]
            beta_c = jax.nn.sigmoid(sm)
            g_c = -jnp.exp(alog_r) * _softplus(sm + bias_r)
            g_r = -jnp.exp(alog_c) * _softplus(sm.T + bias_c)
            gc_c, gc_r = _cumsum_pair(g_c, g_r[SM_GA:SM_GA + N_DIR * GDN_HEADS], q, d)
            for h in range(GDN_HEADS):
                idx = d * GDN_HEADS + h
                row = gc_r[idx:idx + 1, :]
                probs.append(dict(
                    d=d, c=c, h=h, q=qs[h], k=ks[h], v=v[:, h * dim:(h + 1) * dim], row=row, tot=row[:, end:end + 1],
                    colb=jnp.broadcast_to(gc_c[:, SM_GA + idx:SM_GA + idx + 1], (q, dim)),
                    betab=jnp.broadcast_to(beta_c[:, SM_GBETA + idx:SM_GBETA + idx + 1], (q, dim)),
                    incl=incl, strict=strict,
                ))
    k16 = [p["k"].astype(BF16) for p in probs]
    kk = [_dot_nt(k, k) for k in k16]
    qk = [_dot_nt(p["q"], k) for p, k in zip(probs, k16)]
    dmask = [jnp.exp(jnp.where(p["incl"], p["colb"][:, :q] - p["row"], -1e30)) for p in probs]
    a = [jnp.where(p["strict"], p["betab"][:, :q] * kk_i * dm, 0.0) for p, kk_i, dm in zip(probs, kk, dmask)]
    ecol = [jnp.exp(p["colb"]) for p in probs]
    rhs = [
        jnp.concatenate([p["v"] * p["betab"], p["k"] * (p["betab"] * e)], axis=-1).astype(BF16)
        for p, e in zip(probs, ecol)
    ]
    inv = _unit_tri_inverses(a, [p["d"] == 0 for p in probs])
    sol = [_dot(t, r) for t, r in zip(inv, rhs)]
    u_sol = [x[:, :dim] for x in sol]
    w_sol = [x[:, dim:].astype(BF16) for x in sol]
    attn = [(qk_i * dm).astype(BF16) for qk_i, dm in zip(qk, dmask)]
    q_dec = [(p["q"] * e).astype(BF16) for p, e in zip(probs, ecol)]
    k_dec = [(p["k"] * jnp.exp(p["tot"] - p["colb"])).astype(BF16) for p in probs]
    state = {(d, h): s_scr[d, h] for d in range(N_DIR) for h in range(GDN_HEADS)}
    outs = {}
    for step in range(n_sub):
        group = [(i, p) for i, p in enumerate(probs) if _scan_order(p["d"], n_sub)[step] == p["c"]]
        w_s = [_dot(w_sol[i], state[p["d"], p["h"]]) for i, p in group]
        q_s = [_dot(q_dec[i], state[p["d"], p["h"]]) for i, p in group]
        vv = [u_sol[i] - ws for (i, p), ws in zip(group, w_s)]
        o_at = [_dot(attn[i], vv_i) for (i, p), vv_i in zip(group, vv)]
        upd = [_dot_tn(k_dec[i], vv_i) for (i, p), vv_i in zip(group, vv)]
        for (i, p), qs_i, oa, u_i in zip(group, q_s, o_at, upd):
            outs[p["d"], p["c"], p["h"]] = qs_i + oa
            state[p["d"], p["h"]] = state[p["d"], p["h"]] * jnp.exp(p["tot"]) + u_i
    for (d, h), s_new in state.items():
        s_scr[d, h] = s_new
    for d, o_ref in ((0, of_ref), (1, ob_ref)):
        for c in range(n_sub):
            o_ref[c * q:(c + 1) * q, :] = jnp.concatenate([outs[d, c, h] for h in range(GDN_HEADS)], axis=-1).astype(BF16)

    @pl.when(last_t[s] == 1)
    def _():
        sout_ref[...] = s_scr[...]


def _gdn_call(proj, small, tables, s0, a_log, dt_bias, cfg):
    q, w = GDN_CHUNKS_PER_STEP * GDN_CHUNK, GROUP_WIDTH
    t = proj.shape[0]
    n_steps = tables[0].shape[0]
    nlane = N_DIR * GDN_HEADS
    pr = jnp.zeros((SUBLANES, LANES), F32)
    pr = pr.at[0, SM_GA:SM_GA + nlane].set(dt_bias.reshape(-1)).at[1, SM_GA:SM_GA + nlane].set(a_log.reshape(-1))
    pc = pr.T
    st_shape = (N_DIR, GDN_HEADS, GDN_HEAD_DIM, GDN_HEAD_DIM)
    fmap = lambda col: (lambda s, fw, bw, fi, la, sq: (fw[s], col))
    bmap = lambda col: (lambda s, fw, bw, fi, la, sq: (bw[s], col))
    const = lambda shape: pl.BlockSpec(shape, lambda s, *_: tuple(0 for _ in shape))
    st_spec = pl.BlockSpec((None,) + st_shape, lambda s, fw, bw, fi, la, sq: (sq[s], 0, 0, 0, 0))
    blk = lambda m, col: pl.BlockSpec((q, w), m(col))
    grid_spec = pltpu.PrefetchScalarGridSpec(
        num_scalar_prefetch=5,
        grid=(n_steps,),
        in_specs=[
            blk(fmap, COL_G_Q), blk(fmap, COL_G_K), blk(fmap, COL_G_V), pl.BlockSpec((q, LANES), fmap(0)),
            blk(bmap, COL_G_Q), blk(bmap, COL_G_K), blk(bmap, COL_G_V), pl.BlockSpec((q, LANES), bmap(0)),
            const((SUBLANES, LANES)), const((LANES, SUBLANES)), st_spec,
        ],
        out_specs=[pl.BlockSpec((q, w), fmap(0)), pl.BlockSpec((q, w), bmap(0)), st_spec],
        scratch_shapes=[pltpu.VMEM(st_shape, F32)],
    )
    n_seq = cfg["n_dec"] + cfg["n_ctx"]
    return pl.pallas_call(
        _gdn_kernel,
        out_shape=[
            jax.ShapeDtypeStruct((t, w), BF16), jax.ShapeDtypeStruct((t, w), BF16),
            jax.ShapeDtypeStruct((n_seq,) + st_shape, F32),
        ],
        grid_spec=grid_spec,
        compiler_params=_params("arbitrary"),
        name="gdn_scan",
    )(*tables, proj, proj, proj, small, proj, proj, proj, small, pr, pc, s0)


def _rope_tables(cfg):
    l = cfg["ls"]
    pos = np.arange(l)
    r = (pos // GRID_W).astype(np.float32)
    col = (pos % GRID_W).astype(np.float32)
    nf = RET_HEAD_DIM // 4
    inv = np.power(np.float32(ROPE_BASE), -np.arange(nf, dtype=np.float32) / np.float32(nf)).astype(np.float32)
    ang = np.concatenate([r[:, None] * inv, col[:, None] * inv], -1)
    cos, sin = np.cos(ang), np.sin(ang)
    cos2 = np.concatenate([cos, cos], -1)
    sin2 = np.concatenate([-sin, sin], -1)
    ident = np.ones((CHUNKS_PER_STEP * RET_CHUNK, RET_HEAD_DIM), np.float32)
    return np.concatenate([cos2, ident], 0).astype(np.float32), np.concatenate([sin2, 0.0 * ident], 0).astype(np.float32)


def _rope_block_table(cfg):
    q = CHUNKS_PER_STEP * RET_CHUNK
    nc_s, nc_p = cfg["ls"] // q, cfg["lp"] // q
    fwd = [c for _ in range(cfg["n_dec"]) for c in range(nc_s)] + [nc_s] * (cfg["n_ctx"] * nc_p)
    bwd = [nc_s - 1 - c for _ in range(cfg["n_dec"]) for c in range(nc_s)] + [nc_s] * (cfg["n_ctx"] * nc_p)
    return jnp.asarray(np.asarray(fwd, np.int32)), jnp.asarray(np.asarray(bwd, np.int32))


def _ret_kernel(
    fwd_t, bwd_t, first_t, last_t, seq_t, rf_t, rb_t,
    qf_ref, kf_ref, vf_ref, cf_ref, sf_ref, qb_ref, kb_ref, vb_ref, cb_ref, sb_ref, dec_ref, s0_ref,
    of_ref, ob_ref, sout_ref, s_scr,
):
    s = pl.program_id(0)
    q, dim = RET_CHUNK, RET_HEAD_DIM

    @pl.when(first_t[s] == 1)
    def _():
        s_scr[...] = s0_ref[...]

    lg_all = -jnp.exp(dec_ref[...])
    ri = lax.broadcasted_iota(jnp.int32, (q, q), 0)
    ci = lax.broadcasted_iota(jnp.int32, (q, q), 1)
    rpos = lax.broadcasted_iota(jnp.int32, (q, dim), 0)
    n_sub = CHUNKS_PER_STEP
    probs = []
    for d in range(N_DIR):
        q_ref, k_ref, v_ref, c_ref, sn_ref = (
            (qf_ref, kf_ref, vf_ref, cf_ref, sf_ref) if d == 0 else (qb_ref, kb_ref, vb_ref, cb_ref, sb_ref)
        )
        rel = ((ri - ci) if d == 0 else (ci - ri)).astype(F32)
        pos = (rpos if d == 0 else (q - 1 - rpos)).astype(F32)
        for c in _scan_order(d, n_sub):
            rows = slice(c * q, (c + 1) * q)
            qa, ka, va = q_ref[rows, :], k_ref[rows, :], v_ref[rows, :]
            cos, sin = c_ref[rows, :], sn_ref[rows, :]
            for h in range(RET_HEADS):
                idx = d * RET_HEADS + h
                qh = qa[:, h * dim:(h + 1) * dim]
                kh = ka[:, h * dim:(h + 1) * dim] * dim ** -0.5
                probs.append(dict(
                    d=d, c=c, h=h, lg=lg_all[idx:idx + 1, :], rel=rel, pos=pos, v=va[:, h * dim:(h + 1) * dim],
                    q=qh * cos + pltpu.roll(qh, dim // 2, axis=1) * sin,
                    k=kh * cos + pltpu.roll(kh, dim // 2, axis=1) * sin,
                ))
    qk = [_dot_nt(pb["q"], pb["k"]) for pb in probs]
    dmask = [jnp.exp(jnp.where(pb["rel"] >= 0, pb["rel"] * pb["lg"], -1e30)) for pb in probs]
    o_attn = [_dot(qk_i * dm, pb["v"]) for pb, qk_i, dm in zip(probs, qk, dmask)]
    q_dec = [pb["q"] * jnp.exp((pb["pos"] + 1.0) * pb["lg"]) for pb in probs]
    upd = [_dot_tn(pb["k"] * jnp.exp((q - 1.0 - pb["pos"]) * pb["lg"]), pb["v"]) for pb in probs]
    state = {(d, h): s_scr[d, h] for d in range(N_DIR) for h in range(RET_HEADS)}
    outs = {}
    for step in range(n_sub):
        group = [(i, pb) for i, pb in enumerate(probs) if _scan_order(pb["d"], n_sub)[step] == pb["c"]]
        o_state = [_dot(q_dec[i], state[pb["d"], pb["h"]]) for i, pb in group]
        for (i, pb), os_i in zip(group, o_state):
            outs[pb["d"], pb["c"], pb["h"]] = os_i + o_attn[i]
            state[pb["d"], pb["h"]] = state[pb["d"], pb["h"]] * jnp.exp(q * pb["lg"]) + upd[i]
    for (d, h), s_new in state.items():
        s_scr[d, h] = s_new
    for d, o_ref in ((0, of_ref), (1, ob_ref)):
        for c in range(n_sub):
            o_ref[c * q:(c + 1) * q, :] = jnp.concatenate([outs[d, c, h] for h in range(RET_HEADS)], axis=-1).astype(BF16)

    @pl.when(last_t[s] == 1)
    def _():
        sout_ref[...] = s_scr[...]


def _ret_call(proj, tables, rope_tabs, rope_blocks, s0, ret_decay, cfg):
    q, w, dim = CHUNKS_PER_STEP * RET_CHUNK, GROUP_WIDTH, RET_HEAD_DIM
    t = proj.shape[0]
    n_steps = tables[0].shape[0]
    dec = jnp.broadcast_to(ret_decay.reshape(-1, 1), (N_DIR * RET_HEADS, LANES))
    st_shape = (N_DIR, RET_HEADS, dim, dim)
    fmap = lambda col: (lambda s, fw, bw, fi, la, sq, rf, rb: (fw[s], col))
    bmap = lambda col: (lambda s, fw, bw, fi, la, sq, rf, rb: (bw[s], col))
    rfmap = lambda s, fw, bw, fi, la, sq, rf, rb: (rf[s], 0)
    rbmap = lambda s, fw, bw, fi, la, sq, rf, rb: (rb[s], 0)
    st_spec = pl.BlockSpec((None,) + st_shape, lambda s, fw, bw, fi, la, sq, rf, rb: (sq[s], 0, 0, 0, 0))
    blk = lambda m, col: pl.BlockSpec((q, w), m(col))
    rope = lambda m: pl.BlockSpec((q, dim), m)
    grid_spec = pltpu.PrefetchScalarGridSpec(
        num_scalar_prefetch=7,
        grid=(n_steps,),
        in_specs=[
            blk(fmap, COL_R_Q), blk(fmap, COL_R_K), blk(fmap, COL_R_V), rope(rfmap), rope(rfmap),
            blk(bmap, COL_R_Q), blk(bmap, COL_R_K), blk(bmap, COL_R_V), rope(rbmap), rope(rbmap),
            pl.BlockSpec((N_DIR * RET_HEADS, LANES), lambda s, *_: (0, 0)), st_spec,
        ],
        out_specs=[pl.BlockSpec((q, w), fmap(0)), pl.BlockSpec((q, w), bmap(0)), st_spec],
        scratch_shapes=[pltpu.VMEM(st_shape, F32)],
    )
    n_seq = cfg["n_dec"] + cfg["n_ctx"]
    cos2, sin2 = rope_tabs
    return pl.pallas_call(
        _ret_kernel,
        out_shape=[
            jax.ShapeDtypeStruct((t, w), BF16), jax.ShapeDtypeStruct((t, w), BF16),
            jax.ShapeDtypeStruct((n_seq,) + st_shape, F32),
        ],
        grid_spec=grid_spec,
        compiler_params=_params("arbitrary"),
        name="retention_scan",
    )(*tables, *rope_blocks, proj, proj, proj, cos2, sin2, proj, proj, proj, cos2, sin2, dec, s0)


def _layer_norm_rows(y, g, b):
    mu = jnp.mean(y, axis=-1, keepdims=True)
    yc = y - mu
    var = jnp.mean(yc * yc, axis=-1, keepdims=True)
    return yc * lax.rsqrt(var + LN_EPS) * g + b


OUTPROJ_ROW_CHUNK = 256


def _outproj_kernel(
    hys_ref, hyp_ref, sf_ref, sb_ref, sz_ref, gf_ref, gb_ref, gz_ref, rf_ref, rb_ref, rg_ref,
    snw_ref, gnw_ref, w_ref, xs_ref, xp_ref, g1_ref, sh2_ref, sc2_ref, lng_ref, lnb_ref,
    x1_ref, u2_ref, lhs_scr, *, n_sample_tiles, alpha, two_group_x,
):
    i = pl.program_id(0)
    is_sample = i < n_sample_tiles
    w = GROUP_WIDTH
    tm = x1_ref.shape[0]
    rc = min(OUTPROJ_ROW_CHUNK, tm)
    chunks = [slice(r, r + rc) for r in range(0, tm, rc)]
    gnw = gnw_ref[...]
    for rows in chunks:
        y_hy = jnp.where(is_sample, hys_ref[rows, :], hyp_ref[rows, :])
        y_ssd = (sf_ref[rows, :].astype(F32) + sb_ref[rows, :].astype(F32)) * _silu(sz_ref[rows, :])
        y_ssd = y_ssd * lax.rsqrt(jnp.mean(y_ssd * y_ssd, axis=-1, keepdims=True) + RMS_EPS) * snw_ref[...]
        lhs_scr[rows, 0:w] = y_hy.astype(BF16)
        lhs_scr[rows, w:2 * w] = y_ssd.astype(BF16)
        og = gf_ref[rows, :].astype(F32) + gb_ref[rows, :].astype(F32)
        orr = rf_ref[rows, :].astype(F32) + rb_ref[rows, :].astype(F32)
        gz = _silu(gz_ref[rows, :])
        rg = _silu(rg_ref[rows, :])
        for h in range(GDN_HEADS):
            sl = slice(h * GDN_HEAD_DIM, (h + 1) * GDN_HEAD_DIM)
            o = og[:, sl]
            o = o * lax.rsqrt(jnp.mean(o * o, axis=-1, keepdims=True) + RMS_EPS) * gnw * gz[:, sl]
            lhs_scr[rows, 2 * w + h * GDN_HEAD_DIM:2 * w + (h + 1) * GDN_HEAD_DIM] = o.astype(BF16)
        for h in range(RET_HEADS):
            sl = slice(h * RET_HEAD_DIM, (h + 1) * RET_HEAD_DIM)
            o = orr[:, sl]
            mu = jnp.mean(o, axis=-1, keepdims=True)
            oc = o - mu
            o = oc * lax.rsqrt(jnp.mean(oc * oc, axis=-1, keepdims=True) + LN_EPS) * rg[:, sl]
            lhs_scr[rows, 3 * w + h * RET_HEAD_DIM:3 * w + (h + 1) * RET_HEAD_DIM] = o.astype(BF16)
    accs = [jnp.dot(lhs_scr[rows, :], w_ref[...], preferred_element_type=F32) for rows in chunks]
    for rows, acc in zip(chunks, accs):
        x = jnp.where(is_sample, xs_ref[rows, :], xp_ref[rows, :]) if two_group_x else xs_ref[rows, :]
        x1 = _layer_norm_rows(alpha * x + g1_ref[...] * acc, lng_ref[...], lnb_ref[...])
        x1_ref[rows, :] = x1
        u2_ref[rows, :] = (x1 * (1.0 + sc2_ref[...]) + sh2_ref[...]).astype(BF16)


def _outproj_call(hy_s, hy_p, ssd, gdn, ret, proj, ssd_nw, gdn_nw, w_out, x_in, mod4, layer, ln_g, ln_b, cfg):
    t, d, w = cfg["t"], cfg["d"], GROUP_WIDTH
    two_group_x = isinstance(x_in, tuple)
    tm = OUTPROJ_ROW_CHUNK if two_group_x else 2 * OUTPROJ_ROW_CHUNK
    ns = cfg["ts"] // tm
    row = lambda col: pl.BlockSpec((tm, w), lambda i: (i, col))
    vec = lambda n: pl.BlockSpec((None, 1, n), lambda i: (layer, 0, 0))
    mspec = lambda k: _mod_spec(layer, k, d, tm, cfg["ls"], cfg["n_dec"])
    if two_group_x:
        x_specs = _two_group_specs(tm, d, ns)
        x_args = list(x_in)
    else:
        x_specs = [pl.BlockSpec((tm, d), lambda i: (i, 0)), pl.BlockSpec((SUBLANES, d), lambda i: (0, 0))]
        x_args = [x_in, x_in]
    kern = functools.partial(_outproj_kernel, n_sample_tiles=ns, alpha=cfg["alpha"], two_group_x=two_group_x)
    return pl.pallas_call(
        kern,
        out_shape=[jax.ShapeDtypeStruct((t, d), F32), jax.ShapeDtypeStruct((t, d), BF16)],
        grid=(t // tm,),
        in_specs=_two_group_specs(tm, w, ns)
        + [row(0), row(0), row(COL_S_Z), row(0), row(0), row(COL_G_Z), row(0), row(0), row(COL_R_G)]
        + [vec(w), vec(GDN_HEAD_DIM),
           pl.BlockSpec((None, d, d), lambda i: (layer, 0, 0), pipeline_mode=pl.Buffered(1))]
        + x_specs
        + [mspec(2), mspec(3), mspec(4), vec(d), vec(d)],
        out_specs=[pl.BlockSpec((tm, d), lambda i: (i, 0)), pl.BlockSpec((tm, d), lambda i: (i, 0))],
        scratch_shapes=[pltpu.VMEM((tm, d), BF16)],
        compiler_params=_params("arbitrary"),
        name="out_proj_ln",
    )(
        hy_s, hy_p, ssd[0], ssd[1], proj, gdn[0], gdn[1], proj, ret[0], ret[1], proj,
        ssd_nw, gdn_nw, w_out, *x_args, mod4, mod4, mod4, ln_g, ln_b,
    )


def _ffn_up_kernel(u_ref, wg_ref, wv_ref, cwg_ref, cbg_ref, cwv_ref, cbv_ref, o_ref, w_scr, h_scr, *, n_sample_tiles, ls, lp):
    i = pl.program_id(1)
    tn = wg_ref.shape[1]
    seq_len = jnp.where(i < n_sample_tiles, ls, lp)

    @pl.when(i == 0)
    def _():
        w_scr[:, :tn] = wg_ref[...].astype(BF16)
        w_scr[:, tn:] = wv_ref[...].astype(BF16)

    cw = jnp.concatenate([cwg_ref[...], cwv_ref[...]], axis=-1)
    cb = jnp.concatenate([cbg_ref[...], cbv_ref[...]], axis=-1)

    def finish(rows, conv):
        o_ref[rows, :] = (_silu(conv[:, :tn]) * conv[:, tn:]).astype(BF16)

    _matmul_conv_chunks(u_ref, w_scr[...], cw, cb, h_scr, seq_len, min(ls, lp), finish)


def _ffn_up_call(u2, w_up, cw, cb, layer, cfg):
    t, d = u2.shape
    dff = w_up.shape[2] // 2
    tm, tn = cfg["ls"], 512
    nj = dff // tn
    kern = functools.partial(_ffn_up_kernel, n_sample_tiles=cfg["ts"] // tm, ls=cfg["ls"], lp=cfg["lp"])
    gcol = lambda j, i: (layer, 0, j)
    vcol = lambda j, i: (layer, 0, j + nj)
    return pl.pallas_call(
        kern,
        out_shape=jax.ShapeDtypeStruct((t, dff), BF16),
        grid=(nj, t // tm),
        in_specs=[
            pl.BlockSpec((tm, d), lambda j, i: (i, 0)),
            pl.BlockSpec((None, d, tn), gcol), pl.BlockSpec((None, d, tn), vcol),
            pl.BlockSpec((None, CONV_W, tn), gcol), pl.BlockSpec((None, 1, tn), gcol),
            pl.BlockSpec((None, CONV_W, tn), vcol), pl.BlockSpec((None, 1, tn), vcol),
        ],
        out_specs=pl.BlockSpec((tm, tn), lambda j, i: (i, j)),
        scratch_shapes=[pltpu.VMEM((d, 2 * tn), BF16), pltpu.VMEM((tm + 2 * SUBLANES, 2 * tn), F32)],
        compiler_params=_params("arbitrary", "arbitrary"),
        name="ffn_up_conv_glu",
    )(u2, w_up, w_up, cw, cb, cw, cb)


FFN_DOWN_ROW_CHUNK = 256


def _ffn_down_kernel(a_ref, w_ref, x1_ref, g2_ref, shn_ref, scn_ref, lng_ref, lnb_ref, oa_ref, ob_ref, acc_scr,
                     *, alpha, n_sample_tiles, last_layer, n_k):
    i = pl.program_id(0)
    k = pl.program_id(1)
    tm = acc_scr.shape[0]

    if n_k > 1:
        @pl.when(k == 0)
        def _():
            acc_scr[...] = jnp.dot(a_ref[...], w_ref[...], preferred_element_type=F32)

        @pl.when(jnp.logical_and(k > 0, k < n_k - 1))
        def _():
            acc_scr[...] += jnp.dot(a_ref[...], w_ref[...], preferred_element_type=F32)

    @pl.when(k == n_k - 1)
    def _():
        rc = min(FFN_DOWN_ROW_CHUNK, tm)
        chunks = [slice(r, r + rc) for r in range(0, tm, rc)]
        parts = [jnp.dot(a_ref[rows, :], w_ref[...], preferred_element_type=F32) for rows in chunks]
        x2s = []
        for rows, part in zip(chunks, parts):
            f = part + acc_scr[rows, :] if n_k > 1 else part
            x2 = _layer_norm_rows(alpha * x1_ref[rows, :] + g2_ref[...] * f, lng_ref[...], lnb_ref[...])
            x2s.append(x2)
            if not last_layer:
                oa_ref[rows, :] = x2
                ob_ref[rows, :] = (x2 * (1.0 + scn_ref[...]) + shn_ref[...]).astype(BF16)
        if last_layer:
            @pl.when(i < n_sample_tiles)
            def _():
                for rows, x2 in zip(chunks, x2s):
                    oa_ref[rows, :] = x2

            @pl.when(i >= n_sample_tiles)
            def _():
                for rows, x2 in zip(chunks, x2s):
                    ob_ref[rows, :] = x2


def _ffn_down_call(act, w_down, x1, mod4, layer, ln_g, ln_b, cfg):
    t, dff = act.shape
    d = cfg["d"]
    depth = w_down.shape[0]
    last_layer = layer == depth - 1
    next_layer = min(layer + 1, depth - 1)
    tm = 512
    tk = dff // 4 if (dff // 4) % LANES == 0 else dff
    ns = cfg["ts"] // tm
    vec = lambda: pl.BlockSpec((None, 1, d), lambda i, k: (layer, 0, 0))
    mspec = lambda lay, kk: _mod_spec(lay, kk, d, tm, cfg["ls"], cfg["n_dec"])
    if last_layer:
        out_shape = [jax.ShapeDtypeStruct((cfg["ts"], d), F32), jax.ShapeDtypeStruct((cfg["tp"], d), F32)]
        out_specs = [
            pl.BlockSpec((tm, d), lambda i, k: (jnp.minimum(i, ns - 1), 0)),
            pl.BlockSpec((tm, d), lambda i, k: (jnp.maximum(i - ns, 0), 0)),
        ]
    else:
        out_shape = [jax.ShapeDtypeStruct((t, d), F32), jax.ShapeDtypeStruct((t, d), BF16)]
        out_specs = [pl.BlockSpec((tm, d), lambda i, k: (i, 0)), pl.BlockSpec((tm, d), lambda i, k: (i, 0))]
    return pl.pallas_call(
        functools.partial(_ffn_down_kernel, alpha=cfg["alpha"], n_sample_tiles=ns, last_layer=last_layer, n_k=dff // tk),
        out_shape=out_shape,
        grid=(t // tm, dff // tk),
        in_specs=[
            pl.BlockSpec((tm, tk), lambda i, k: (i, k)),
            pl.BlockSpec((None, tk, d), lambda i, k: (layer, k, 0)),
            pl.BlockSpec((tm, d), lambda i, k: (i, 0)),
            mspec(layer, 5), mspec(next_layer, 0), mspec(next_layer, 1), vec(), vec(),
        ],
        out_specs=out_specs,
        scratch_shapes=[pltpu.VMEM((tm, d), F32)],
        compiler_params=_params("arbitrary", "arbitrary"),
        name="ffn_down_ln",
    )(act, w_down, x1, mod4, mod4, mod4, ln_g, ln_b)


def _reorder_w_in(w_in):
    w = GROUP_WIDTH
    o = N_CONV
    s_z = w_in[..., o:o + w]
    o += w
    s_dt = w_in[..., o:o + N_DIR * SSD_HEADS]
    o += N_DIR * SSD_HEADS
    g_z = w_in[..., o:o + w]
    o += w
    g_beta = w_in[..., o:o + N_DIR * GDN_HEADS]
    o += N_DIR * GDN_HEADS
    g_a = w_in[..., o:o + N_DIR * GDN_HEADS]
    o += N_DIR * GDN_HEADS
    rest = w_in[..., o:]
    main = jnp.concatenate([w_in[..., :N_CONV], s_z, g_z, rest], axis=-1).astype(BF16)
    small = jnp.concatenate([s_dt, g_beta, g_a], axis=-1)
    small = jnp.pad(small, ((0, 0), (0, 0), (0, LANES - small.shape[-1]))).astype(BF16)
    return main, small


def kernel(x_prompt, x_sample, state_ssd, state_gdn, state_ret, c, c_ctx, w_mod, b_mod, w_in, conv_w, conv_b,
           hy_w1, hy_b1, hy_w2, hy_b2, hy_w3, hy_freq, hy_bias, ssd_A_log, ssd_dt_bias, ssd_D, ssd_norm_w,
           gdn_A_log, gdn_dt_bias, gdn_norm_w, ret_decay, w_out, ln1_g, ln1_b, w_up, ffn_conv_w, ffn_conv_b,
           w_down, ln2_g, ln2_b):
    n_ctx, lp, d = x_prompt.shape
    n_dec, ls, _ = x_sample.shape
    depth = w_in.shape[0]
    ts, tp = n_dec * ls, n_ctx * lp
    assert ls & (ls - 1) == 0 and lp & (lp - 1) == 0 and tp % ls == 0 and ls % lp == 0
    assert d == 4 * GROUP_WIDTH and w_in.shape[2] == N_MAIN + 2 * N_DIR * (SSD_HEADS // 2 + GDN_HEADS)
    cfg = dict(d=d, ls=ls, lp=lp, n_dec=n_dec, n_ctx=n_ctx, ts=ts, tp=tp, t=ts + tp, alpha=(2 * depth) ** 0.25)

    xs = x_sample.reshape(ts, d)
    xp = x_prompt.reshape(tp, d)

    mod_rows = -(-(n_dec + 1) // SUBLANES) * SUBLANES
    cond = jnp.concatenate([c, c_ctx[None, :], jnp.zeros((mod_rows - n_dec - 1, d), F32)], axis=0)
    mod = _mod_call(cond, w_mod, b_mod)
    mod4 = mod.reshape(depth, mod_rows, 1, 6 * d)

    w_main, w_small = _reorder_w_in(w_in)
    w_out_b, w_down_b = w_out.astype(BF16), w_down.astype(BF16)
    stack_rows = lambda v: v[:, None, :]

    dft = {l: tuple(jnp.asarray(tab).astype(BF16) for tab in _dft_tables(l)) for l in (ls, lp)}
    rope_tabs = _rope_tables(cfg)
    rope_blocks = _rope_block_table(cfg)
    ssd_rows, gdn_rows, ret_rows = CHUNKS_PER_STEP * SSD_CHUNK, GDN_CHUNKS_PER_STEP * GDN_CHUNK, CHUNKS_PER_STEP * RET_CHUNK
    tabs = {q: _chunk_tables(cfg, q) for q in sorted({ssd_rows, gdn_rows, ret_rows})}

    def init_state(st, l):
        zeros = jnp.zeros((n_ctx,) + st.shape[2:], F32)
        return jnp.concatenate([st[:, l], zeros], axis=0)

    u = _modulate_call(xs, xp, mod4, 0, cfg)
    x_res = (xs, xp)
    new_ssd, new_gdn, new_ret = [], [], []
    for l in range(depth):
        proj, small = _inproj_call(u, w_main, w_small, conv_w, stack_rows(conv_b), l, cfg)

        hy = {}
        for name, seq_len, n_seq, row0 in (("s", ls, n_dec, 0), ("p", lp, n_ctx, ts)):
            a_tab, b_tab = dft[seq_len]
            prs = _hy_filter_call(seq_len, a_tab, hy_w1[l], hy_b1[l], hy_w2[l], hy_b2[l], hy_w3[l], hy_freq[l])
            z1 = _hy_conv_call(proj, COL_HY_V, row0, proj, COL_HY_X1, row0, hy_bias[l], a_tab, b_tab, prs, 0, seq_len, n_seq)
            hy[name] = _hy_conv_call(z1, 0, 0, proj, COL_HY_X2, row0, hy_bias[l], a_tab, b_tab, prs, 1, seq_len, n_seq)

        ssd = _ssd_call(proj, small, tabs[ssd_rows], init_state(state_ssd, l), ssd_dt_bias[l], ssd_A_log[l], ssd_D[l], cfg)
        gdn = _gdn_call(proj, small, tabs[gdn_rows], init_state(state_gdn, l), gdn_A_log[l], gdn_dt_bias[l], cfg)
        ret = _ret_call(proj, tabs[ret_rows], rope_tabs, rope_blocks, init_state(state_ret, l), ret_decay[l], cfg)
        new_ssd.append(ssd[2][n_dec:])
        new_gdn.append(gdn[2][n_dec:])
        new_ret.append(ret[2][n_dec:])

        x1, u2 = _outproj_call(hy["s"], hy["p"], ssd, gdn, ret, proj, stack_rows(ssd_norm_w), stack_rows(gdn_norm_w),
                               w_out_b, x_res, mod4, l, stack_rows(ln1_g), stack_rows(ln1_b), cfg)
        act = _ffn_up_call(u2, w_up, ffn_conv_w, stack_rows(ffn_conv_b), l, cfg)
        x_res, u = _ffn_down_call(act, w_down_b, x1, mod4, l, stack_rows(ln2_g), stack_rows(ln2_b), cfg)

    y_sample = x_res.reshape(n_dec, ls, d)
    y_prompt = u.reshape(n_ctx, lp, d)
    return (y_prompt, y_sample, jnp.stack(new_ssd, 1), jnp.stack(new_gdn, 1), jnp.stack(new_ret, 1))
```
